```python
import math
import jax
import jax.numpy as jnp
from jax import lax
import numpy as np

D_MODEL = 1024
BATCH = 4
SEQ = 4096
DEPTH = 4
DEC_BATCH = 32
DEC_SEQ = 1
PAST_LEN = 8192
PAGE_SIZE = 128

HEAD_DIM = 64
MIX_DIM = D_MODEL
D_A = MIX_DIM // 4
H_A = D_A // HEAD_DIM
D_B = MIX_DIM // 4
G_B = 4
CB = D_B // G_B
D_C = MIX_DIM - D_A - D_B
H_C = D_C // HEAD_DIM
KV_C = max(H_C // 4, 1)
R_C = H_C // KV_C
KV_W = KV_C * HEAD_DIM
N_IN = 3 * D_A + 2 * D_B + D_C + 6 * KV_W + 3 * H_C
CHUNK = 128
Q_BLOCK = 128
CMP_LEN = 32
CMP_STRIDE = 16
CMP_HID = HEAD_DIM
SLC_BLOCK = 64
TOP_N = 16
N_LOCAL = 2
WINDOW = 512
NUM_BUCKETS = 32
REL_MAX_DIST = 2048
D_FF = -(-8 * D_MODEL // (3 * 256)) * 256
EPS = 1e-6
SCALE = HEAD_DIM ** -0.5

kernel_name = 'hybrid_sb_gmlp_nsa_step'


def _rmsnorm(x, g):
    xf = x.astype(jnp.float32)
    y = xf * lax.rsqrt(jnp.mean(xf * xf, axis=-1, keepdims=True) + EPS)
    return (y * g.astype(jnp.float32)).astype(x.dtype)


def _layernorm(x):
    xf = x.astype(jnp.float32)
    xc = xf - jnp.mean(xf, axis=-1, keepdims=True)
    return xc * lax.rsqrt(jnp.mean(xc * xc, axis=-1, keepdims=True) + EPS)


def _masked_softmax(s, mask):
    s = jnp.where(mask, s, -jnp.inf)
    m = jnp.max(s, axis=-1, keepdims=True)
    m = jnp.where(jnp.isfinite(m), m, 0.0)
    p = jnp.exp(s - m)
    return p / jnp.maximum(jnp.sum(p, axis=-1, keepdims=True), 1e-30)


def _rel_bucket(dist):
    n = jnp.maximum(dist, 0)
    exact = NUM_BUCKETS // 2
    nf = jnp.maximum(n, 1).astype(jnp.float32)
    big = exact + (jnp.log(nf / exact) / math.log(REL_MAX_DIST / exact)
                   * (NUM_BUCKETS - exact)).astype(jnp.int32)
    return jnp.where(n < exact, n, jnp.minimum(big, NUM_BUCKETS - 1))


def _bias_grtk(dist, rel_bias):
    t, k = dist.shape
    bias = rel_bias[_rel_bucket(dist)].astype(jnp.float32)
    return bias.reshape(t, k, KV_C, R_C).transpose(2, 3, 0, 1)


def _project(h, w):
    z = h @ w
    sizes = [D_A, D_A, D_A, D_B, D_B, D_C] + [KV_W] * 6 + [3 * H_C]
    idx = [int(v) for v in np.cumsum(sizes)[:-1]]
    qa, ka, va, ub, vb, qc, ck, cv, sk, sv, wk, wv, gate = jnp.split(z, idx, axis=-1)
    b, t, _ = z.shape
    hd = lambda a, n: a.reshape(b, t, n, HEAD_DIM)
    return dict(qa=hd(qa, H_A), ka=hd(ka, H_A), va=hd(va, H_A), ub=ub, vb=vb,
                qc=hd(qc, H_C), ck=hd(ck, KV_C), cv=hd(cv, KV_C), sk=hd(sk, KV_C),
                sv=hd(sv, KV_C), wk=hd(wk, KV_C), wv=hd(wv, KV_C), gate=gate)


def _stick_breaking(q, k, v, qpos, kpos):
    z = jnp.einsum('bthd,bshd->bhts', q, k).astype(jnp.float32) * SCALE
    mask = kpos[None, :] < qpos[:, None]
    log_beta = jax.nn.log_sigmoid(z)
    log_keep = jnp.where(mask, jax.nn.log_sigmoid(-z), 0.0)
    after = lax.cumsum(log_keep, axis=3, reverse=True) - log_keep
    a = jnp.where(mask, jnp.exp(log_beta + after), 0.0)
    return jnp.einsum('bhts,bshd->bthd', a, v.astype(jnp.float32))


def _chunk_sgu(u, v, ws, bs, rows):
    b, t, _ = v.shape
    vn = _layernorm(v)
    vr = vn.reshape(b, t // rows, rows, G_B, CB)
    w = ws[:, :rows, :rows].astype(jnp.float32) * jnp.tril(jnp.ones((rows, rows), jnp.float32))
    mixed = (jnp.einsum('gij,bcjgd->bcigd', w, vr)
             + bs[:, :rows].T.astype(jnp.float32)[None, None, :, :, None])
    return u.astype(jnp.float32) * mixed.reshape(b, t, D_B), vn


def _compress(kv, pe, w1, w2):
    b, L, g, d = kv.shape
    n_cb = (L - CMP_LEN) // CMP_STRIDE + 1
    n_sub = CMP_LEN // CMP_STRIDE
    ch = kv[:, :(n_cb + n_sub - 1) * CMP_STRIDE].reshape(b, n_cb + n_sub - 1, CMP_STRIDE, g, d)
    blocks = jnp.concatenate([ch[:, m:m + n_cb] for m in range(n_sub)], axis=2)
    hid = jax.nn.gelu(jnp.einsum('bnjgd,jdh->bngh', blocks + pe[:, None, :], w1))
    return jnp.einsum('bngh,he->bnge', hid, w2)


def _nsa_compressed(q, qpos, kc, vc, rel_bias):
    b, t, h, d = q.shape
    qg = q.reshape(b, t, KV_C, R_C, d)
    ends = jnp.arange(kc.shape[1]) * CMP_STRIDE + (CMP_LEN - 1)
    dist = qpos[:, None] - ends[None, :]
    s = jnp.einsum('btgrd,bngd->bgrtn', qg, kc).astype(jnp.float32) * SCALE + _bias_grtk(dist, rel_bias)
    p = _masked_softmax(s, dist >= 0)
    o = jnp.einsum('bgrtn,bngd->btgrd', p, vc.astype(jnp.float32))
    return o.reshape(b, t, h, d), p


def _to_sel_blocks(x):
    b, L, g, d = x.shape
    n_sb = -(-L // SLC_BLOCK)
    x = jnp.pad(x, ((0, 0), (0, n_sb * SLC_BLOCK - L), (0, 0), (0, 0)))
    return x.reshape(b, n_sb, SLC_BLOCK, g, d).transpose(0, 3, 1, 2, 4)


def _nsa_selected(q, qpos, p_cmp, kb, vb, rel_bias):
    b, t, h, d = q.shape
    n_sb = kb.shape[2]
    n_cb = p_cmp.shape[-1]
    c0 = np.arange(n_cb) * CMP_STRIDE
    s0 = np.arange(n_sb) * SLC_BLOCK
    overlap = (c0[:, None] < s0[None, :] + SLC_BLOCK) & (c0[:, None] + CMP_LEN > s0[None, :])
    imp = jnp.einsum('bgrtn,nj->bgtj', p_cmp, jnp.asarray(overlap, jnp.float32))
    cur = qpos // SLC_BLOCK
    j = jnp.arange(n_sb)
    causal = j[None, :] <= cur[:, None]
    forced = (j[None, :] == 0) | (j[None, :] > cur[:, None] - N_LOCAL)
    score = jnp.where(causal, jnp.where(forced, jnp.inf, imp), -jnp.inf)
    n_top = min(TOP_N, n_sb)
    vals, idx = lax.top_k(score, n_top)
    ok = vals > -jnp.inf
    bi = jnp.arange(b)[:, None, None, None]
    gi = jnp.arange(KV_C)[None, :, None, None]
    n_k = n_top * SLC_BLOCK
    kg = kb[bi, gi, idx].reshape(b, KV_C, t, n_k, d)
    vg = vb[bi, gi, idx].reshape(b, KV_C, t, n_k, d)
    pos = (idx[..., None] * SLC_BLOCK + jnp.arange(SLC_BLOCK)).reshape(b, KV_C, t, n_k)
    dist = qpos[None, None, :, None] - pos
    mask = jnp.repeat(ok, SLC_BLOCK, axis=-1) & (dist >= 0)
    tab = rel_bias.reshape(NUM_BUCKETS, KV_C, R_C).astype(jnp.float32)
    bias = tab[_rel_bucket(dist)[:, :, None], jnp.arange(KV_C)[None, :, None, None, None],
               jnp.arange(R_C)[None, None, :, None, None]]
    qg = q.reshape(b, t, KV_C, R_C, d)
    s = jnp.einsum('btgrd,bgtkd->bgrtk', qg, kg).astype(jnp.float32) * SCALE + bias
    p = _masked_softmax(s, mask[:, :, None])
    o = jnp.einsum('bgrtk,bgtkd->btgrd', p, vg.astype(jnp.float32))
    return o.reshape(b, t, h, d)


def _nsa_window(q, qpos, kw, vw, kpos, rel_bias):
    b, t, h, d = q.shape
    dist = qpos[:, None] - kpos[None, :]
    mask = (dist >= 0) & (dist <= WINDOW) & (kpos[None, :] >= 0)
    qg = q.reshape(b, t, KV_C, R_C, d)
    s = jnp.einsum('btgrd,bkgd->bgrtk', qg, kw).astype(jnp.float32) * SCALE + _bias_grtk(dist, rel_bias)
    p = _masked_softmax(s, mask)
    o = jnp.einsum('bgrtk,bkgd->btgrd', p, vw.astype(jnp.float32))
    return o.reshape(b, t, h, d)


def _nsa_combine(gate, o_cmp, o_slc, o_win):
    b, t = gate.shape[:2]
    g = jax.nn.sigmoid(gate.astype(jnp.float32)).reshape(b, t, 3, H_C, 1)
    o = g[:, :, 0] * o_cmp + g[:, :, 1] * o_slc + g[:, :, 2] * o_win
    return o.reshape(b, t, D_C)


def _mixer_prompt(p, cmp_pe, cmp_w1, cmp_w2, ws, bs, rel_bias):
    b, S = p['qa'].shape[:2]
    n_qb = S // Q_BLOCK
    kc = _compress(p['ck'], cmp_pe[0], cmp_w1[0], cmp_w2[0])
    vc = _compress(p['cv'], cmp_pe[1], cmp_w1[1], cmp_w2[1])
    kb = _to_sel_blocks(p['sk'])
    vb = _to_sel_blocks(p['sv'])
    pad = ((0, 0), (WINDOW, 0), (0, 0), (0, 0))
    kw_pad = jnp.pad(p['wk'], pad)
    vw_pad = jnp.pad(p['wv'], pad)
    kpos_all = jnp.arange(S)
    blk = lambda a: a.reshape(b, n_qb, Q_BLOCK, *a.shape[2:]).swapaxes(0, 1)

    def body(xs):
        i, qa, qc, gt = xs
        s0 = i * Q_BLOCK
        qpos = s0 + jnp.arange(Q_BLOCK)
        o_a = _stick_breaking(qa, p['ka'], p['va'], qpos, kpos_all)
        o_cmp, p_cmp = _nsa_compressed(qc, qpos, kc, vc, rel_bias)
        o_slc = _nsa_selected(qc, qpos, p_cmp, kb, vb, rel_bias)
        kw = lax.dynamic_slice_in_dim(kw_pad, s0, WINDOW + Q_BLOCK, axis=1)
        vw = lax.dynamic_slice_in_dim(vw_pad, s0, WINDOW + Q_BLOCK, axis=1)
        wpos = s0 - WINDOW + jnp.arange(WINDOW + Q_BLOCK)
        o_win = _nsa_window(qc, qpos, kw, vw, wpos, rel_bias)
        return o_a, _nsa_combine(gt, o_cmp, o_slc, o_win)

    o_a, o_c = lax.map(body, (jnp.arange(n_qb), blk(p['qa']), blk(p['qc']), blk(p['gate'])))
    unblk = lambda a: a.swapaxes(0, 1).reshape(b, S, *a.shape[3:])
    o_b, _ = _chunk_sgu(p['ub'], p['vb'], ws, bs, CHUNK)
    return unblk(o_a), o_b, unblk(o_c)


def _mixer_sample(p, sb_past, cmp_past, slc_past, win_buf, cmp_pe, cmp_w1, cmp_w2, ws, bs, rel_bias):
    t = p['qa'].shape[1]
    past = sb_past.shape[1]
    qpos = past + jnp.arange(t)
    kpos = jnp.arange(past + t)
    cat = lambda old, new: jnp.concatenate([old, new.astype(old.dtype)], axis=1)
    o_a = _stick_breaking(p['qa'], cat(sb_past[:, :, 0], p['ka']), cat(sb_past[:, :, 1], p['va']), qpos, kpos)
    kc = _compress(cat(cmp_past[:, :, 0], p['ck']), cmp_pe[0], cmp_w1[0], cmp_w2[0])
    vc = _compress(cat(cmp_past[:, :, 1], p['cv']), cmp_pe[1], cmp_w1[1], cmp_w2[1])
    o_cmp, p_cmp = _nsa_compressed(p['qc'], qpos, kc, vc, rel_bias)
    kb = _to_sel_blocks(cat(slc_past[:, :, 0], p['sk']))
    vb = _to_sel_blocks(cat(slc_past[:, :, 1], p['sv']))
    o_slc = _nsa_selected(p['qc'], qpos, p_cmp, kb, vb, rel_bias)
    w_len = win_buf.shape[1]
    kw = cat(win_buf[:, :, 0], p['wk'])
    vw = cat(win_buf[:, :, 1], p['wv'])
    wpos = past - w_len + jnp.arange(w_len + t)
    o_win = _nsa_window(p['qc'], qpos, kw, vw, wpos, rel_bias)
    o_c = _nsa_combine(p['gate'], o_cmp, o_slc, o_win)
    o_b, vn = _chunk_sgu(p['ub'], p['vb'], ws, bs, t)
    new_win = jnp.stack([kw[:, -w_len:], vw[:, -w_len:]], axis=2)
    return o_a, o_b, o_c, new_win, vn


def _mix_out(o_a, o_b, o_c, g, w_out, dtype):
    b, t = o_b.shape[:2]
    oa = _rmsnorm(o_a.reshape(b, t, D_A), g[:D_A])
    ob = _rmsnorm(o_b, g[D_A:D_A + D_B])
    oc = _rmsnorm(o_c, g[D_A + D_B:])
    return jnp.concatenate([oa, ob, oc], axis=-1).astype(dtype) @ w_out


def _swiglu(h, w_up, w_down):
    g, u = jnp.split(h @ w_up, 2, axis=-1)
    return (jax.nn.silu(g) * u) @ w_down


def setup_inputs(seed: int = 0) -> dict:
    key = jax.random.key(seed)
    ks = jax.random.split(key, 24)
    n_pages = PAST_LEN // PAGE_SIZE
    n_used = DEC_BATCH * n_pages
    n_phys = n_used + max(n_used // 4, 1)
    w_buf = min(WINDOW, PAST_LEN)
    nrm = lambda k, shape, scale: scale * jax.random.normal(k, shape, jnp.float32)
    gain = lambda k, shape: 1.0 + 0.05 * jax.random.normal(k, shape, jnp.float32)
    page_table = jax.random.permutation(ks[6], n_phys)[:n_used].reshape(DEC_BATCH, n_pages).astype(jnp.int32)
    return {
        'x_prompt': nrm(ks[0], (BATCH, SEQ, D_MODEL), 1.0),
        'x_sample': nrm(ks[1], (DEC_BATCH, DEC_SEQ, D_MODEL), 1.0),
        'cache_sb_kv': nrm(ks[2], (DEPTH, n_phys, PAGE_SIZE, 2, H_A, HEAD_DIM), 1.0),
        'cache_cmp_kv': nrm(ks[3], (DEPTH, n_phys, PAGE_SIZE, 2, KV_C, HEAD_DIM), 1.0),
        'cache_slc_kv': nrm(ks[4], (DEPTH, n_phys, PAGE_SIZE, 2, KV_C, HEAD_DIM), 1.0),
        'cache_win_kv': nrm(ks[5], (DEPTH, DEC_BATCH, w_buf, 2, KV_C, HEAD_DIM), 1.0),
        'page_table': page_table,
        'rel_bias': nrm(ks[7], (NUM_BUCKETS, H_C), 0.2),
        'norm_mix_pre': gain(ks[8], (DEPTH, D_MODEL)),
        'norm_mix_post': gain(ks[9], (DEPTH, D_MODEL)),
        'w_in': nrm(ks[10], (DEPTH, D_MODEL, N_IN), D_MODEL ** -0.5),
        'cmp_pe': nrm(ks[11], (DEPTH, 2, CMP_LEN, HEAD_DIM), 0.1),
        'cmp_w1': nrm(ks[12], (DEPTH, 2, CMP_LEN, HEAD_DIM, CMP_HID), (CMP_LEN * HEAD_DIM) ** -0.5),
        'cmp_w2': nrm(ks[13], (DEPTH, 2, CMP_HID, HEAD_DIM), CMP_HID ** -0.5),
        'gmlp_ws': nrm(ks[14], (DEPTH, G_B, CHUNK, CHUNK), CHUNK ** -0.5),
        'gmlp_b': nrm(ks[15], (DEPTH, G_B, CHUNK), 0.1),
        'norm_group_out': gain(ks[16], (DEPTH, MIX_DIM)),
        'w_out': nrm(ks[17], (DEPTH, MIX_DIM, D_MODEL), MIX_DIM ** -0.5),
        'norm_ffn_pre': gain(ks[18], (DEPTH, D_MODEL)),
        'norm_ffn_post': gain(ks[19], (DEPTH, D_MODEL)),
        'w_ffn_up': nrm(ks[20], (DEPTH, D_MODEL, 2 * D_FF), D_MODEL ** -0.5),
        'w_ffn_down': nrm(ks[21], (DEPTH, D_FF, D_MODEL), D_FF ** -0.5),
    }


def reference(x_prompt, x_sample, cache_sb_kv, cache_cmp_kv, cache_slc_kv, cache_win_kv, page_table,
              rel_bias, norm_mix_pre, norm_mix_post, w_in, cmp_pe, cmp_w1, cmp_w2, gmlp_ws, gmlp_b,
              norm_group_out, w_out, norm_ffn_pre, norm_ffn_post, w_ffn_up, w_ffn_down):
    db = x_sample.shape[0]
    past = page_table.shape[1] * PAGE_SIZE
    s_len = x_prompt.shape[1]
    xp, xs = x_prompt, x_sample
    p_sb, p_cmp, p_slc, p_win = [], [], [], []
    s_sb, s_cmp, s_slc, s_win, s_gv = [], [], [], [], []
    for l in range(DEPTH):
        hp = _project(_rmsnorm(xp, norm_mix_pre[l]), w_in[l])
        oa, ob, oc = _mixer_prompt(hp, cmp_pe[l], cmp_w1[l], cmp_w2[l], gmlp_ws[l], gmlp_b[l], rel_bias)
        xp = xp + _rmsnorm(_mix_out(oa, ob, oc, norm_group_out[l], w_out[l], xp.dtype), norm_mix_post[l])
        xp = xp + _rmsnorm(_swiglu(_rmsnorm(xp, norm_ffn_pre[l]), w_ffn_up[l], w_ffn_down[l]), norm_ffn_post[l])
        p_sb.append(jnp.stack([hp['ka'], hp['va']], axis=2))
        p_cmp.append(jnp.stack([hp['ck'], hp['cv']], axis=2))
        p_slc.append(jnp.stack([hp['sk'], hp['sv']], axis=2))
        p_win.append(jnp.stack([hp['wk'], hp['wv']], axis=2)[:, s_len - min(WINDOW, s_len):])
        gather = lambda pool: pool[l][page_table].reshape(db, past, *pool.shape[3:])
        hs = _project(_rmsnorm(xs, norm_mix_pre[l]), w_in[l])
        oa, ob, oc, new_win, vn = _mixer_sample(hs, gather(cache_sb_kv), gather(cache_cmp_kv),
                                                gather(cache_slc_kv), cache_win_kv[l], cmp_pe[l],
                                                cmp_w1[l], cmp_w2[l], gmlp_ws[l], gmlp_b[l], rel_bias)
        xs = xs + _rmsnorm(_mix_out(oa, ob, oc, norm_group_out[l], w_out[l], xs.dtype), norm_mix_post[l])
        xs = xs + _rmsnorm(_swiglu(_rmsnorm(xs, norm_ffn_pre[l]), w_ffn_up[l], w_ffn_down[l]), norm_ffn_post[l])
        s_sb.append(jnp.stack([hs['ka'], hs['va']], axis=2))
        s_cmp.append(jnp.stack([hs['ck'], hs['cv']], axis=2))
        s_slc.append(jnp.stack([hs['sk'], hs['sv']], axis=2))
        s_win.append(new_win)
        s_gv.append(vn.astype(xs.dtype))
    return (xp, xs, jnp.stack(p_sb), jnp.stack(p_cmp), jnp.stack(p_slc), jnp.stack(p_win),
            jnp.stack(s_sb), jnp.stack(s_cmp), jnp.stack(s_slc), jnp.stack(s_win), jnp.stack(s_gv))
```

```python
import functools
import math

import jax
import jax.numpy as jnp
import numpy as np
from jax import lax
from jax.experimental import pallas as pl
from jax.experimental.pallas import tpu as pltpu

F32 = jnp.float32
BF16 = jnp.bfloat16

D_MODEL = 1024
HEAD_DIM = 64
D_A = 256
H_A = 4
D_B = 256
G_B = 4
CB = 64
D_C = 512
H_C = 8
KV_C = 2
R_C = 4
KV_W = 128
N_IN = 3 * D_A + 2 * D_B + D_C + 6 * KV_W + 3 * H_C
NP_IN = 2688
CHUNK = 128
CMP_LEN = 32
CMP_STRIDE = 16
SLC_BLOCK = 64
TOP_N = 16
N_LOCAL = 2
WINDOW = 512
NUM_BUCKETS = 32
REL_MAX_DIST = 2048
D_FF = 2816
EPS = 1e-6
SCALE = HEAD_DIM ** -0.5
PAGE = 128

C_QA, C_KA, C_VA, C_UB, C_VB, C_QC = 0, 256, 512, 768, 1024, 1280
C_CK, C_CV, C_SK, C_SV, C_WK, C_WV, C_GATE = 1792, 1920, 2048, 2176, 2304, 2432, 2560

LANES = 128
TQ_ATT = 128
TK_ATT = 128
VMEM_LIMIT = 56 * 1024 * 1024
NEG = -1e30
DECAYED = -110.0
BIG_SCORE = 3e38
PAGES_PER_STEP = 8


def _cparams(sem):
    return pltpu.CompilerParams(dimension_semantics=sem, vmem_limit_bytes=VMEM_LIMIT)


def _dot(a, b):
    return jnp.dot(a, b, preferred_element_type=F32)


def _dot_nt(a, b):
    return lax.dot_general(a, b, (((1,), (1,)), ((), ())), preferred_element_type=F32)


def _rms(x, g):
    return x * lax.rsqrt(jnp.mean(x * x, axis=-1, keepdims=True) + EPS) * g


def _sigmoid(x):
    return 1.0 / (1.0 + jnp.exp(-x))


def _gelu_tanh(x):
    return 0.5 * x * (1.0 + jnp.tanh(math.sqrt(2.0 / math.pi) * (x + 0.044715 * (x * x * x))))


def _iota(shape, dim):
    return lax.broadcasted_iota(jnp.int32, shape, dim)


def _bucket_thresholds():
    n = np.arange(0, 1 << 15)
    exact = NUM_BUCKETS // 2
    nf = np.maximum(n, 1).astype(np.float32)
    big = exact + (np.log(nf / np.float32(exact)) / np.float32(math.log(REL_MAX_DIST / exact))
                   * np.float32(NUM_BUCKETS - exact)).astype(np.int32)
    bucket = np.where(n < exact, n, np.minimum(big, NUM_BUCKETS - 1))
    assert np.all(np.diff(bucket) >= 0)
    return [int(np.argmax(bucket >= k)) for k in range(1, NUM_BUCKETS)]


_THR = _bucket_thresholds()


def _bias_table_kernel(tab_ref, o_ref, *, a, rs, cs, c0):
    shape = o_ref.shape[2:]
    dist = a * pl.program_id(0) + rs * _iota(shape, 0) + cs * _iota(shape, 1) + c0
    outs = [jnp.full(shape, tab_ref[h], F32) for h in range(H_C)]
    for k in range(1, NUM_BUCKETS):
        ge = dist >= _THR[k - 1]
        for h in range(H_C):
            outs[h] = jnp.where(ge, tab_ref[k * H_C + h], outs[h])
    for h in range(H_C):
        o_ref[h, 0] = outs[h]


def _bias_table(rel_bias, steps, rows, cols, a, rs, cs, c0):
    return pl.pallas_call(
        functools.partial(_bias_table_kernel, a=a, rs=rs, cs=cs, c0=c0),
        out_shape=jax.ShapeDtypeStruct((H_C, steps, rows, cols), F32),
        grid=(steps,),
        in_specs=[pl.BlockSpec(memory_space=pltpu.SMEM)],
        out_specs=pl.BlockSpec((H_C, 1, rows, cols), lambda i: (0, i, 0, 0)),
        compiler_params=_cparams(("arbitrary",)),
        name="bias_table",
    )(rel_bias.reshape(-1))


def _proj_kernel(x_ref, g_ref, w_ref, o_ref):
    h = _rms(x_ref[...], g_ref[...])
    o_ref[...] = _dot(h.astype(BF16), w_ref[...])


def _project(x, g, w, tm):
    t = x.shape[0]
    return pl.pallas_call(
        _proj_kernel,
        out_shape=jax.ShapeDtypeStruct((t, NP_IN), F32),
        grid=(t // tm,),
        in_specs=[pl.BlockSpec((tm, D_MODEL), lambda i: (i, 0)),
                  pl.BlockSpec((1, D_MODEL), lambda i: (0, 0)),
                  pl.BlockSpec((D_MODEL, NP_IN), lambda i: (0, 0))],
        out_specs=pl.BlockSpec((tm, NP_IN), lambda i: (i, 0)),
        compiler_params=_cparams(("parallel",)),
        name="in_proj",
    )(x, g, w)


def _mixout_kernel(oa_ref, ob_ref, oc_ref, x_ref, gg_ref, w_ref, gp_ref, o_ref):
    gg = gg_ref[...]
    a = _rms(oa_ref[...], gg[:, :D_A]).astype(BF16)
    b = _rms(ob_ref[...], gg[:, D_A:D_A + D_B]).astype(BF16)
    c = _rms(oc_ref[...], gg[:, D_A + D_B:]).astype(BF16)
    y = (_dot(a, w_ref[0:D_A, :]) + _dot(b, w_ref[D_A:D_A + D_B, :])
         + _dot(c, w_ref[D_A + D_B:, :]))
    o_ref[...] = x_ref[...] + _rms(y, gp_ref[...])


def _mix_out(oa, ob, oc, x, gg, w, gp, tm):
    t = x.shape[0]
    row = lambda width: pl.BlockSpec((tm, width), lambda i: (i, 0))
    full = lambda r, c: pl.BlockSpec((r, c), lambda i: (0, 0))
    return pl.pallas_call(
        _mixout_kernel,
        out_shape=jax.ShapeDtypeStruct((t, D_MODEL), F32),
        grid=(t // tm,),
        in_specs=[row(D_A), row(D_B), row(D_C), row(D_MODEL), full(1, D_MODEL),
                  full(D_MODEL, D_MODEL), full(1, D_MODEL)],
        out_specs=row(D_MODEL),
        compiler_params=_cparams(("parallel",)),
        name="mix_out",
    )(oa, ob, oc, x, gg, w, gp)


def _ffn_kernel(x_ref, gpre_ref, wg_ref, wu_ref, wd_ref, gpost_ref, o_ref, h_ref, acc_ref):
    j = pl.program_id(1)

    @pl.when(j == 0)
    def _():
        h_ref[...] = _rms(x_ref[...], gpre_ref[...]).astype(BF16)
        acc_ref[...] = jnp.zeros_like(acc_ref)

    h = h_ref[...]
    g = _dot(h, wg_ref[...])
    u = _dot(h, wu_ref[...])
    act = (g * _sigmoid(g)) * u
    acc_ref[...] += _dot(act.astype(BF16), wd_ref[...])

    @pl.when(j == pl.num_programs(1) - 1)
    def _():
        o_ref[...] = x_ref[...] + _rms(acc_ref[...], gpost_ref[...])


def _ffn(x, gpre, w_up, w_down, gpost, tm, tf):
    t = x.shape[0]
    nf = D_FF // tf
    return pl.pallas_call(
        _ffn_kernel,
        out_shape=jax.ShapeDtypeStruct((t, D_MODEL), F32),
        grid=(t // tm, nf),
        in_specs=[pl.BlockSpec((tm, D_MODEL), lambda i, j: (i, 0)),
                  pl.BlockSpec((1, D_MODEL), lambda i, j: (0, 0)),
                  pl.BlockSpec((D_MODEL, tf), lambda i, j: (0, j)),
                  pl.BlockSpec((D_MODEL, tf), lambda i, j: (0, j + nf)),
                  pl.BlockSpec((tf, D_MODEL), lambda i, j: (j, 0)),
                  pl.BlockSpec((1, D_MODEL), lambda i, j: (0, 0))],
        out_specs=pl.BlockSpec((tm, D_MODEL), lambda i, j: (i, 0)),
        scratch_shapes=[pltpu.VMEM((tm, D_MODEL), BF16), pltpu.VMEM((tm, D_MODEL), F32)],
        compiler_params=_cparams(("parallel", "arbitrary")),
        name="ffn",
    )(x, gpre, w_up, w_up, w_down, gpost)


def _log_sig_pair(z):
    soft = jnp.log1p(jnp.exp(-jnp.abs(z)))
    return jnp.minimum(z, 0.0) - soft, jnp.minimum(-z, 0.0) - soft


def _suffix_sum_exclusive(x, upper_bf16):
    hi = x.astype(BF16)
    rem = x - hi.astype(F32)
    mid = rem.astype(BF16)
    lo = (rem - mid.astype(F32)).astype(BF16)
    return _dot(hi, upper_bf16) + _dot(mid, upper_bf16) + _dot(lo, upper_bf16)


def _strict_upper(n):
    return jnp.where(_iota((n, n), 0) > _iota((n, n), 1), 1.0, 0.0).astype(BF16)


def _sb_prompt_kernel(q_ref, k_ref, v_ref, o_ref, run_ref, acc_ref, *, tq, tk):
    i = pl.program_id(2)
    lane = _iota((tq, LANES), 1)
    qs = q_ref[0] * SCALE
    qh = [jnp.where(lane < HEAD_DIM, qs, 0.0).astype(BF16),
          jnp.where(lane >= HEAD_DIM, qs, 0.0).astype(BF16)]
    upper = _strict_upper(tk)
    qpos = i * tq + _iota((tq, tk), 0)
    run_ref[...] = jnp.zeros_like(run_ref)
    acc_ref[...] = jnp.zeros_like(acc_ref)

    def cond(c):
        j, alive = c
        return jnp.logical_and(j >= 0, alive > 0)

    def body(c):
        j, _ = c
        ks = pl.multiple_of(j * tk, tk)
        kt = k_ref[0, pl.ds(ks, tk), :].astype(BF16)
        vt = v_ref[0, pl.ds(ks, tk), :].astype(BF16)
        mask = (ks + _iota((tq, tk), 1)) < qpos
        worst = jnp.float32(-jnp.inf)
        for h in range(2):
            z = _dot_nt(qh[h], kt)
            lb, lk = _log_sig_pair(z)
            lk = jnp.where(mask, lk, 0.0)
            run = run_ref[h]
            after = _suffix_sum_exclusive(lk, upper) + run
            a = jnp.where(mask, jnp.exp(lb + after), 0.0)
            acc_ref[h] += _dot(a.astype(BF16), vt)
            new_run = after[:, 0:1] + lk[:, 0:1]
            run_ref[h] = new_run
            worst = jnp.maximum(worst, jnp.max(new_run))
        return j - 1, (worst > DECAYED).astype(jnp.int32)

    n_tiles = ((i + 1) * tq) // tk
    lax.while_loop(cond, body, (n_tiles - 1, jnp.int32(1)))
    o_ref[0] = jnp.where(lane < HEAD_DIM, acc_ref[0], acc_ref[1])


def _sb_prompt(z3, tq=256, tk=TK_ATT):
    b, s, _ = z3.shape
    return pl.pallas_call(
        functools.partial(_sb_prompt_kernel, tq=tq, tk=tk),
        out_shape=jax.ShapeDtypeStruct((b, s, D_A), F32),
        grid=(b, D_A // LANES, s // tq),
        in_specs=[pl.BlockSpec((1, tq, LANES), lambda bi, hp, i: (bi, i, C_QA // LANES + hp)),
                  pl.BlockSpec((1, s, LANES), lambda bi, hp, i: (bi, 0, C_KA // LANES + hp)),
                  pl.BlockSpec((1, s, LANES), lambda bi, hp, i: (bi, 0, C_VA // LANES + hp))],
        out_specs=pl.BlockSpec((1, tq, LANES), lambda bi, hp, i: (bi, i, hp)),
        scratch_shapes=[pltpu.VMEM((2, tq, 1), F32), pltpu.VMEM((2, tq, LANES), F32)],
        compiler_params=_cparams(("parallel", "parallel", "arbitrary")),
        name="sb_prompt",
    )(z3, z3, z3)


def _head_rows(row_vec, width):
    full = jnp.broadcast_to(row_vec, (8, width))
    return jnp.where(_iota((8, width), 1) // HEAD_DIM == _iota((8, width), 0), full, 0.0)


def _sb_decode_kernel(pt_ref, q_ref, *rest, n_steps):
    del pt_ref
    pages = rest[:PAGES_PER_STEP]
    o_ref, run_ref, acc_ref = rest[PAGES_PER_STEP:]
    step = pl.program_id(1)

    @pl.when(step == 0)
    def _():
        run_ref[...] = jnp.zeros_like(run_ref)
        acc_ref[...] = jnp.zeros_like(acc_ref)

    q8 = (_head_rows(q_ref[0], D_A) * SCALE).astype(BF16)
    upper = _strict_upper(PAGE)
    for page in pages:
        kt = page[0, 0, :, 0:D_A].astype(BF16)
        vt = page[0, 0, :, D_A:2 * D_A].astype(BF16)
        z = _dot_nt(q8, kt)
        lb, lk = _log_sig_pair(z)
        after = _suffix_sum_exclusive(lk, upper) + run_ref[...]
        a = jnp.exp(lb + after)
        acc_ref[...] += _dot(a.astype(BF16), vt)
        run_ref[...] = after[:, 0:1] + lk[:, 0:1]

    @pl.when(step == n_steps - 1)
    def _():
        own = _iota((8, D_A), 1) // HEAD_DIM == _iota((8, D_A), 0)
        o_ref[0] = jnp.sum(jnp.where(own, acc_ref[...], 0.0), axis=0, keepdims=True)


def _sb_decode(qa, cache4, page_table, layer):
    db, n_pages = page_table.shape
    n_steps = n_pages // PAGES_PER_STEP

    def page_spec(k):
        def imap(b, s, pt):
            return (layer, pt[b, n_pages - 1 - (s * PAGES_PER_STEP + k)], 0, 0)
        return pl.BlockSpec((1, 1, PAGE, 2 * D_A), imap)

    grid_spec = pltpu.PrefetchScalarGridSpec(
        num_scalar_prefetch=1,
        grid=(db, n_steps),
        in_specs=[pl.BlockSpec((1, 1, D_A), lambda b, s, pt: (b, 0, 0))]
                 + [page_spec(k) for k in range(PAGES_PER_STEP)],
        out_specs=pl.BlockSpec((1, 1, D_A), lambda b, s, pt: (b, 0, 0)),
        scratch_shapes=[pltpu.VMEM((8, 1), F32), pltpu.VMEM((8, D_A), F32)],
    )
    return pl.pallas_call(
        functools.partial(_sb_decode_kernel, n_steps=n_steps),
        out_shape=jax.ShapeDtypeStruct((db, 1, D_A), F32),
        grid_spec=grid_spec,
        compiler_params=_cparams(("parallel", "arbitrary")),
        name="sb_decode",
    )(page_table, qa, *([cache4] * PAGES_PER_STEP))


def _layernorm(x):
    xc = x - jnp.mean(x, axis=-1, keepdims=True)
    return xc * lax.rsqrt(jnp.mean(xc * xc, axis=-1, keepdims=True) + EPS)


def _gmlp_prompt_kernel(u_ref, v_ref, w_ref, b_ref, o_ref, *, n_chunks):
    tril = _iota((CHUNK, CHUNK), 0) >= _iota((CHUNK, CHUNK), 1)
    ws = [jnp.where(tril, w_ref[g], 0.0).astype(BF16) for g in range(G_B)]
    group = _iota((CHUNK, D_B), 1) // CB
    for c in range(n_chunks):
        rows = slice(c * CHUNK, (c + 1) * CHUNK)
        vn = _layernorm(v_ref[0, rows, :]).astype(BF16)
        mixed = b_ref[...]
        for g in range(G_B):
            mixed = mixed + jnp.where(group == g, _dot(ws[g], vn), 0.0)
        o_ref[0, rows, :] = u_ref[0, rows, :] * mixed


def _gmlp_prompt(z3, ws, bias_rows, n_chunks=4):
    b, s, _ = z3.shape
    tr = n_chunks * CHUNK
    return pl.pallas_call(
        functools.partial(_gmlp_prompt_kernel, n_chunks=n_chunks),
        out_shape=jax.ShapeDtypeStruct((b, s, D_B), F32),
        grid=(b, s // tr),
        in_specs=[pl.BlockSpec((1, tr, D_B), lambda bi, i: (bi, i, C_UB // D_B)),
                  pl.BlockSpec((1, tr, D_B), lambda bi, i: (bi, i, C_VB // D_B)),
                  pl.BlockSpec((G_B, CHUNK, CHUNK), lambda bi, i: (0, 0, 0)),
                  pl.BlockSpec((CHUNK, D_B), lambda bi, i: (0, 0))],
        out_specs=pl.BlockSpec((1, tr, D_B), lambda bi, i: (bi, i, 0)),
        compiler_params=_cparams(("parallel", "parallel")),
        name="gmlp_prompt",
    )(z3, z3, ws, bias_rows)


def _gmlp_sample_kernel(u_ref, v_ref, w_ref, b_ref, o_ref, vn_ref):
    vn = _layernorm(v_ref[...])
    vn_ref[...] = vn
    o_ref[...] = u_ref[...] * (w_ref[...] * vn + b_ref[...])


def _gmlp_sample(u, v, w_row, b_row):
    return pl.pallas_call(
        _gmlp_sample_kernel,
        out_shape=(jax.ShapeDtypeStruct(u.shape, F32), jax.ShapeDtypeStruct(u.shape, F32)),
        name="gmlp_sample",
    )(u, v, w_row, b_row)


def _compress_hidden(load_rows, pe_ref, w1_fn, n_chunks, width):
    hid_a = jnp.zeros((n_chunks, width), F32)
    hid_b = jnp.zeros((n_chunks, width), F32)
    for j in range(CMP_STRIDE):
        x = load_rows(j)
        hid_a += _dot((x + pe_ref[j:j + 1, :]).astype(BF16), w1_fn(j))
        hid_b += _dot((x + pe_ref[CMP_STRIDE + j:CMP_STRIDE + j + 1, :]).astype(BF16),
                      w1_fn(CMP_STRIDE + j))
    return hid_a + pltpu.roll(hid_b, n_chunks - 1, 0)


def _compress_prompt_kernel(z_ref, pe_ref, w1_ref, w2_ref, o_ref, *, n_chunks):
    hid = _compress_hidden(lambda j: z_ref[0, pl.ds(j, n_chunks, stride=CMP_STRIDE), :],
                           pe_ref.at[0], lambda j: w1_ref[0, j], n_chunks, KV_W)
    act = _gelu_tanh(hid).astype(BF16)
    valid = _iota((n_chunks, 4 * HEAD_DIM), 0) < n_chunks - 1
    for g in range(KV_C):
        o_ref[0, 0, g] = jnp.where(valid, _dot(act, w2_ref[0, g]), 0.0)


def _compress_prompt(z3, pe2, w1bd, w2rep):
    b, s, _ = z3.shape
    n_chunks = s // CMP_STRIDE
    return pl.pallas_call(
        functools.partial(_compress_prompt_kernel, n_chunks=n_chunks),
        out_shape=jax.ShapeDtypeStruct((b, 2, KV_C, n_chunks, 4 * HEAD_DIM), F32),
        grid=(b, 2),
        in_specs=[pl.BlockSpec((1, s, KV_W), lambda bi, kv: (bi, 0, C_CK // KV_W + kv)),
                  pl.BlockSpec((1, CMP_LEN, KV_W), lambda bi, kv: (kv, 0, 0)),
                  pl.BlockSpec((1, CMP_LEN, KV_W, KV_W), lambda bi, kv: (kv, 0, 0, 0)),
                  pl.BlockSpec((1, KV_C, KV_W, 4 * HEAD_DIM), lambda bi, kv: (kv, 0, 0, 0))],
        out_specs=pl.BlockSpec((1, 1, KV_C, n_chunks, 4 * HEAD_DIM),
                               lambda bi, kv: (bi, kv, 0, 0, 0)),
        compiler_params=_cparams(("parallel", "parallel")),
        name="nsa_compress_prompt",
    )(z3, pe2, w1bd, w2rep)


def _select_blocks_t(score, causal, n_top):
    n_sb = score.shape[0]
    jt = _iota(score.shape, 0)
    rank = jnp.zeros(score.shape, F32)
    for jp in range(n_sb):
        row = score[jp:jp + 1, :]
        ge = jnp.where(row >= score, 1.0, 0.0)
        gt = jnp.where(row > score, 1.0, 0.0)
        rank += jnp.where(jt > jp, ge, gt)
    return jnp.where(causal, jnp.where(rank < n_top, 1.0, 0.0), 0.0)


def _nsa_cmp_prompt_kernel(q0_ref, q1_ref, kvc_ref, bias_ref, ocmp_ref, sel_ref, *, n_sb, n_cb):
    i = pl.program_id(1)
    tq = TQ_ATT
    n_pad = kvc_ref.shape[3]
    lane_head = _iota((tq, 4 * HEAD_DIM), 1) // HEAD_DIM
    n_idx = _iota((tq, n_pad), 1)
    dist = (i * tq + _iota((tq, n_pad), 0)) - (n_idx * CMP_STRIDE + (CMP_LEN - 1))
    mask = jnp.logical_and(dist >= 0, n_idx < n_cb)
    c0 = _iota((n_sb, n_pad), 1) * CMP_STRIDE
    s0 = _iota((n_sb, n_pad), 0) * SLC_BLOCK
    ov_t = jnp.where(jnp.logical_and(c0 < s0 + SLC_BLOCK, c0 + CMP_LEN > s0), 1.0, 0.0).astype(BF16)
    t_row = i * tq + _iota((n_sb, tq), 1)
    cur = t_row // SLC_BLOCK
    jt = _iota((n_sb, tq), 0)
    causal = jt <= cur
    forced = jnp.logical_or(jt == 0, jt > cur - N_LOCAL)
    sel_t = []
    for g, q_ref in enumerate((q0_ref, q1_ref)):
        qs = q_ref[0] * SCALE
        qm = jnp.concatenate([jnp.where(lane_head == r, qs, 0.0) for r in range(R_C)],
                             axis=0).astype(BF16)
        kc = kvc_ref[0, 0, g].astype(BF16)
        vc = kvc_ref[0, 1, g].astype(BF16)
        s_all = _dot_nt(qm, kc)
        o_acc = jnp.zeros((tq, 4 * HEAD_DIM), F32)
        imp_t = jnp.zeros((n_sb, tq), F32)
        for r in range(R_C):
            s = s_all[r * tq:(r + 1) * tq] + bias_ref[R_C * g + r, 0]
            sm = jnp.where(mask, s, NEG)
            m = jnp.max(sm, axis=-1, keepdims=True)
            p = jnp.where(mask, jnp.exp(sm - m), 0.0)
            p = p / jnp.maximum(jnp.sum(p, axis=-1, keepdims=True), 1e-30)
            pb = p.astype(BF16)
            o_acc += jnp.where(lane_head == r, _dot(pb, vc), 0.0)
            imp_t += _dot_nt(ov_t, pb)
        ocmp_ref[0, :, g * 4 * HEAD_DIM:(g + 1) * 4 * HEAD_DIM] = o_acc
        score = jnp.where(causal, jnp.where(forced, BIG_SCORE, imp_t), -1.0)
        sel_t.append(_select_blocks_t(score, causal, min(TOP_N, n_sb)))
    pad = jnp.zeros((SLC_BLOCK - n_sb, tq), F32)
    both = jnp.concatenate([sel_t[0], pad, sel_t[1], pad] if n_sb < SLC_BLOCK else sel_t, axis=0)
    eye = jnp.where(_iota((tq, tq), 0) == _iota((tq, tq), 1), 1.0, 0.0).astype(BF16)
    sel_ref[0] = _dot_nt(eye, both.astype(BF16))


def _nsa_cmp_prompt(z3, kvc, bias_cmp):
    b, s, _ = z3.shape
    n_sb = s // SLC_BLOCK
    n_cb = (s - CMP_LEN) // CMP_STRIDE + 1
    n_pad = kvc.shape[3]
    wq = 4 * HEAD_DIM
    return pl.pallas_call(
        functools.partial(_nsa_cmp_prompt_kernel, n_sb=n_sb, n_cb=n_cb),
        out_shape=(jax.ShapeDtypeStruct((b, s, D_C), F32),
                   jax.ShapeDtypeStruct((b, s, 2 * SLC_BLOCK), F32)),
        grid=(b, s // TQ_ATT),
        in_specs=[pl.BlockSpec((1, TQ_ATT, wq), lambda bi, i: (bi, i, C_QC // wq)),
                  pl.BlockSpec((1, TQ_ATT, wq), lambda bi, i: (bi, i, C_QC // wq + 1)),
                  pl.BlockSpec((1, 2, KV_C, n_pad, wq), lambda bi, i: (bi, 0, 0, 0, 0)),
                  pl.BlockSpec((H_C, 1, TQ_ATT, n_pad), lambda bi, i: (0, i, 0, 0))],
        out_specs=(pl.BlockSpec((1, TQ_ATT, D_C), lambda bi, i: (bi, i, 0)),
                   pl.BlockSpec((1, TQ_ATT, 2 * SLC_BLOCK), lambda bi, i: (bi, i, 0))),
        compiler_params=_cparams(("parallel", "parallel")),
        name="nsa_cmp_prompt",
    )(z3, z3, kvc, bias_cmp)


def _flash_step(s, valid_f, vt, m_ref, l_ref, acc_ref):
    valid = jnp.concatenate([valid_f] * R_C, axis=0) > 0.5
    sm = jnp.where(valid, s, NEG)
    m_prev = m_ref[...]
    m_new = jnp.maximum(m_prev, jnp.max(sm, axis=-1, keepdims=True))
    alpha = jnp.exp(m_prev - m_new)
    p = jnp.where(valid, jnp.exp(sm - m_new), 0.0)
    l_ref[...] = alpha * l_ref[...] + jnp.sum(p, axis=-1, keepdims=True)
    acc_ref[...] = alpha * acc_ref[...] + _dot(p.astype(BF16), vt)
    m_ref[...] = m_new


def _nsa_attn_prompt_kernel(q_ref, sk_ref, sv_ref, wk_ref, wv_ref, sel_ref, tz_ref, gate_ref,
                            ocmp_ref, o_ref, m_ref, l_ref, acc_ref, *, n_diag):
    g = pl.program_id(1)
    i = pl.program_id(2)
    tq, tk = TQ_ATT, TK_ATT
    wq = 4 * HEAD_DIM
    lane = _iota((tq, LANES), 1)
    own_half = (lane // HEAD_DIM) == g
    q = q_ref[0]
    pieces = []
    for r in range(R_C):
        a = pltpu.roll(q, (wq - r * HEAD_DIM) % wq, 1)[:, :LANES]
        both = jnp.where(g == 0, a, pltpu.roll(a, HEAD_DIM, 1))
        pieces.append(jnp.where(own_half, both, 0.0))
    q4 = (jnp.concatenate(pieces, axis=0) * SCALE).astype(BF16)
    qpos = i * tq + _iota((tq, tk), 0)
    sel = sel_ref[0].astype(BF16)

    def reset():
        m_ref[...] = jnp.full(m_ref.shape, NEG, F32)
        l_ref[...] = jnp.zeros_like(l_ref)
        acc_ref[...] = jnp.zeros_like(acc_ref)

    def bias_tile(jt):
        d = jnp.minimum(i - jt, n_diag - 1)
        return tz_ref[d].reshape(R_C * tq, tk)

    def slc_body(jt, carry):
        ks = pl.multiple_of(jt * tk, tk)
        kt = sk_ref[0, pl.ds(ks, tk), :].astype(BF16)
        vt = sv_ref[0, pl.ds(ks, tk), :].astype(BF16)
        s = _dot_nt(q4, kt) + bias_tile(jt)
        target = g * SLC_BLOCK + 2 * jt + (_iota((LANES, tk), 1) // SLC_BLOCK)
        expand = jnp.where(_iota((LANES, tk), 0) == target, 1.0, 0.0).astype(BF16)
        chosen = _dot(sel, expand)
        kpos = ks + _iota((tq, tk), 1)
        _flash_step(s, jnp.where(kpos <= qpos, chosen, 0.0), vt, m_ref, l_ref, acc_ref)
        return carry

    def win_body(jt, carry):
        ks = pl.multiple_of(jt * tk, tk)
        kt = wk_ref[0, pl.ds(ks, tk), :].astype(BF16)
        vt = wv_ref[0, pl.ds(ks, tk), :].astype(BF16)
        s = _dot_nt(q4, kt) + bias_tile(jt)
        dist = qpos - (ks + _iota((tq, tk), 1))
        valid_f = jnp.where(jnp.logical_and(dist >= 0, dist <= WINDOW), 1.0, 0.0)
        _flash_step(s, valid_f, vt, m_ref, l_ref, acc_ref)
        return carry

    def to_out_lanes(acc):
        out = jnp.zeros((tq, wq), F32)
        for r in range(R_C):
            x = acc[r * tq:(r + 1) * tq]
            x = jnp.where(g == 0, x, pltpu.roll(x, HEAD_DIM, 1))
            x = jnp.where(lane < HEAD_DIM, x, 0.0)
            x = jnp.concatenate([x, jnp.zeros((tq, wq - LANES), F32)], axis=1)
            out += x if r == 0 else pltpu.roll(x, r * HEAD_DIM, 1)
        return out

    reset()
    lax.fori_loop(0, i + 1, slc_body, 0)
    o_slc = to_out_lanes(acc_ref[...] / l_ref[...])
    reset()
    lax.fori_loop(jnp.maximum(i - WINDOW // tk, 0), i + 1, win_body, 0)
    o_win = to_out_lanes(acc_ref[...] / l_ref[...])

    sig = _sigmoid(gate_ref[0])
    lane_head = _iota((tq, wq), 1) // HEAD_DIM
    out = jnp.zeros((tq, wq), F32)
    for k, branch in enumerate((ocmp_ref[0], o_slc, o_win)):
        gk = jnp.zeros((tq, wq), F32)
        for r in range(R_C):
            col = jnp.sum(jnp.where(lane == k * H_C + R_C * g + r, sig, 0.0), axis=-1, keepdims=True)
            gk = jnp.where(lane_head == r, col, gk)
        out += gk * branch
    o_ref[0] = out


def _nsa_attn_prompt(z3, sel, tz, ocmp):
    b, s, _ = z3.shape
    wq = 4 * HEAD_DIM
    n_diag = tz.shape[0]
    kv_spec = lambda col: pl.BlockSpec((1, s, KV_W), lambda bi, g, i: (bi, 0, col // KV_W))
    return pl.pallas_call(
        functools.partial(_nsa_attn_prompt_kernel, n_diag=n_diag),
        out_shape=jax.ShapeDtypeStruct((b, s, D_C), F32),
        grid=(b, KV_C, s // TQ_ATT),
        in_specs=[pl.BlockSpec((1, TQ_ATT, wq), lambda bi, g, i: (bi, i, C_QC // wq + g)),
                  kv_spec(C_SK), kv_spec(C_SV), kv_spec(C_WK), kv_spec(C_WV),
                  pl.BlockSpec((1, TQ_ATT, 2 * SLC_BLOCK), lambda bi, g, i: (bi, i, 0)),
                  pl.BlockSpec((n_diag, R_C, TQ_ATT, TK_ATT), lambda bi, g, i: (0, g, 0, 0)),
                  pl.BlockSpec((1, TQ_ATT, LANES), lambda bi, g, i: (bi, i, C_GATE // LANES)),
                  pl.BlockSpec((1, TQ_ATT, wq), lambda bi, g, i: (bi, i, g))],
        out_specs=pl.BlockSpec((1, TQ_ATT, wq), lambda bi, g, i: (bi, i, g)),
        scratch_shapes=[pltpu.VMEM((R_C * TQ_ATT, 1), F32), pltpu.VMEM((R_C * TQ_ATT, 1), F32),
                        pltpu.VMEM((R_C * TQ_ATT, LANES), F32)],
        compiler_params=_cparams(("parallel", "parallel", "arbitrary")),
        name="nsa_attn_prompt",
    )(z3, z3, z3, z3, z3, sel, tz, z3, ocmp)


def _own_lanes(width):
    return _iota((8, width), 1) // HEAD_DIM == _iota((8, width), 0)


def _nsa_cmp_decode_kernel(pt_ref, q_ref, pe_ref, w1_ref, w2_ref, bias_ref, *rest,
                           n_steps, n_sb):
    del pt_ref
    pages = rest[:PAGES_PER_STEP]
    f_ref, imp_ref, buf_ref = rest[PAGES_PER_STEP:]
    step = pl.program_id(1)
    for k, page in enumerate(pages):
        row0 = pl.multiple_of((step * PAGES_PER_STEP + k) * PAGE, PAGE)
        buf_ref[0, pl.ds(row0, PAGE), :] = page[0, 0, :, 0:KV_W]
        buf_ref[1, pl.ds(row0, PAGE), :] = page[0, 0, :, KV_W:2 * KV_W]

    @pl.when(step == n_steps - 1)
    def _():
        n_chunks = buf_ref.shape[1] // CMP_STRIDE
        n_cb = n_chunks - 1
        kv8 = []
        for kv in range(2):
            hid = _compress_hidden(
                lambda j, kv=kv: buf_ref[kv, pl.ds(j, n_chunks, stride=CMP_STRIDE), :],
                pe_ref.at[kv], lambda j, kv=kv: w1_ref[kv, j], n_chunks, KV_W)
            kv8.append(_dot(_gelu_tanh(hid).astype(BF16), w2_ref[kv]).astype(BF16))
        k8, v8 = kv8
        q8 = (_head_rows(q_ref[0], D_C) * SCALE).astype(BF16)
        s = _dot_nt(q8, k8) + bias_ref[...]
        n_idx = _iota((8, n_chunks), 1)
        mask = n_idx < n_cb
        sm = jnp.where(mask, s, NEG)
        m = jnp.max(sm, axis=-1, keepdims=True)
        p = jnp.where(mask, jnp.exp(sm - m), 0.0)
        p = p / jnp.maximum(jnp.sum(p, axis=-1, keepdims=True), 1e-30)
        pb = p.astype(BF16)
        f_ref[0] = jnp.where(_own_lanes(D_C), _dot(pb, v8), 0.0)
        n_pad = imp_ref.shape[2]
        c0 = _iota((n_chunks, n_pad), 0) * CMP_STRIDE
        j_idx = _iota((n_chunks, n_pad), 1)
        s0 = j_idx * SLC_BLOCK
        ov = jnp.logical_and(jnp.logical_and(c0 < s0 + SLC_BLOCK, c0 + CMP_LEN > s0),
                             jnp.logical_and(_iota((n_chunks, n_pad), 0) < n_cb, j_idx < n_sb))
        imp8 = _dot(pb, jnp.where(ov, 1.0, 0.0).astype(BF16))
        row = _iota((8, n_pad), 0)
        g0 = jnp.sum(jnp.where(row < R_C, imp8, 0.0), axis=0, keepdims=True)
        g1 = jnp.sum(jnp.where(row >= R_C, imp8, 0.0), axis=0, keepdims=True)
        imp_ref[0] = jnp.where(row == 0, g0, jnp.where(row == 1, g1, 0.0))


def _nsa_cmp_decode(qc, cache4, page_table, layer, pe2, w1bd, w2heads, bias_cmp, n_pad):
    db, n_pages = page_table.shape
    past = n_pages * PAGE
    n_steps = n_pages // PAGES_PER_STEP
    n_sb = -(-(past + 1) // SLC_BLOCK)
    n_chunks = past // CMP_STRIDE
    const = lambda shape: pl.BlockSpec(shape, lambda b, s, pt: (0,) * len(shape))

    def page_spec(k):
        def imap(b, s, pt):
            return (layer, pt[b, s * PAGES_PER_STEP + k], 0, 0)
        return pl.BlockSpec((1, 1, PAGE, 2 * KV_W), imap)

    grid_spec = pltpu.PrefetchScalarGridSpec(
        num_scalar_prefetch=1,
        grid=(db, n_steps),
        in_specs=[pl.BlockSpec((1, 1, D_C), lambda b, s, pt: (b, 0, 0)),
                  const((2, CMP_LEN, KV_W)), const((2, CMP_LEN, KV_W, KV_W)),
                  const((2, KV_W, D_C)), const((8, n_chunks))]
                 + [page_spec(k) for k in range(PAGES_PER_STEP)],
        out_specs=(pl.BlockSpec((1, 8, D_C), lambda b, s, pt: (b, 0, 0)),
                   pl.BlockSpec((1, 8, n_pad), lambda b, s, pt: (b, 0, 0))),
        scratch_shapes=[pltpu.VMEM((2, past, KV_W), F32)],
    )
    return pl.pallas_call(
        functools.partial(_nsa_cmp_decode_kernel, n_steps=n_steps, n_sb=n_sb),
        out_shape=(jax.ShapeDtypeStruct((db, 8, D_C), F32),
                   jax.ShapeDtypeStruct((db, 8, n_pad), F32)),
        grid_spec=grid_spec,
        compiler_params=_cparams(("parallel", "arbitrary")),
        name="nsa_cmp_decode",
    )(page_table, qc, pe2, w1bd, w2heads, bias_cmp, *([cache4] * PAGES_PER_STEP))


def _topk_decode_kernel(imp_ref, idx_ref, *, cur, n_sb, n_top):
    shape = imp_ref.shape
    lane = _iota(shape, 1)
    lane_f = lane.astype(F32)
    causal = jnp.logical_and(lane <= cur, lane < n_sb)
    forced = jnp.logical_or(lane == 0, lane > cur - N_LOCAL)
    score = jnp.where(causal, jnp.where(forced, BIG_SCORE, imp_ref[...]), -1.0)
    out_lane = _iota(idx_ref.shape, 1)
    out = jnp.full(idx_ref.shape, -1, jnp.int32)
    for k in range(n_top):
        m = jnp.max(score, axis=-1, keepdims=True)
        first = jnp.min(jnp.where(score == m, lane_f, 1e9), axis=-1, keepdims=True)
        pick = jnp.where(m > -0.5, first, -1.0).astype(jnp.int32)
        out = jnp.where(out_lane == k, pick, out)
        score = jnp.where(lane_f == first, -2.0, score)
    idx_ref[...] = out


def _topk_decode(imp_rows, cur, n_sb):
    return pl.pallas_call(
        functools.partial(_topk_decode_kernel, cur=cur, n_sb=n_sb, n_top=min(TOP_N, n_sb)),
        out_shape=jax.ShapeDtypeStruct((imp_rows.shape[0], LANES), jnp.int32),
        name="nsa_topk_decode",
    )(imp_rows)


def _softmax_with_new(s, valid, s_new, has_new):
    sm = jnp.where(valid, s, NEG)
    sn = jnp.where(has_new, s_new, NEG)
    m = jnp.maximum(jnp.max(sm, axis=-1, keepdims=True), sn)
    p = jnp.where(valid, jnp.exp(sm - m), 0.0)
    pn = jnp.where(has_new, jnp.exp(sn - m), 0.0)
    inv = 1.0 / jnp.maximum(jnp.sum(p, axis=-1, keepdims=True) + pn, 1e-30)
    return p * inv, pn * inv


def _to_f_form(o8):
    x = jnp.concatenate([o8, jnp.zeros((8, D_C - 2 * KV_W), F32)], axis=1)
    row = _iota((8, D_C), 0)
    out = jnp.zeros((8, D_C), F32)
    for h in range(H_C):
        shift = (h * HEAD_DIM - (KV_W + (h // R_C) * HEAD_DIM)) % D_C
        out = jnp.where(row == h, x if shift == 0 else pltpu.roll(x, shift, 1), out)
    return jnp.where(_own_lanes(D_C), out, 0.0)


def _nsa_attn_decode_kernel(pt_ref, idx_ref, q_ref, gate_ref, nslc_ref, nwin_ref, fcmp_ref, tsel_ref,
                            twin_ref, b0_ref, win_ref, *rest, n_top, new_blk):
    del pt_ref
    blocks = rest[:KV_C * n_top]
    o_ref, nw_ref = rest[KV_C * n_top:]
    b = pl.program_id(0)
    w_len = win_ref.shape[2]

    qb = jnp.broadcast_to(q_ref[0], (8, D_C))
    row5 = _iota((8, D_C), 0)
    qr = jnp.zeros((8, D_C), F32)
    for h in range(H_C):
        shift = ((h // R_C) * HEAD_DIM - h * HEAD_DIM) % D_C
        qr = jnp.where(row5 == h, qb if shift == 0 else pltpu.roll(qb, shift, 1), qr)
    lane2 = _iota((8, 2 * KV_W), 1)
    row2 = _iota((8, 2 * KV_W), 0)
    q8k_f = jnp.where(lane2 // HEAD_DIM == row2 // R_C, qr[:, :2 * KV_W], 0.0) * SCALE
    q8k = q8k_f.astype(BF16)
    b0 = b0_ref[...]

    lane1 = _iota((8, LANES), 1)
    o_groups = []
    for g in range(KV_C):
        ids = [idx_ref[(b * KV_C + g) * n_top + k] for k in range(n_top)]
        ks = jnp.concatenate([blocks[g * n_top + k][0, 0] for k in range(n_top)], axis=0).astype(BF16)
        s = _dot_nt(q8k, ks)
        bias_parts, valid_parts = [], []
        has_new = jnp.int32(0)
        for kk in range(n_top // 2):
            ia, ib = ids[2 * kk], ids[2 * kk + 1]
            ta = tsel_ref[jnp.clip(ia, 0, new_blk)]
            tb = tsel_ref[jnp.clip(ib, 0, new_blk)]
            bias_parts.append(jnp.where(lane1 < SLC_BLOCK, ta, tb))
            va = jnp.logical_and(ia >= 0, ia < new_blk).astype(F32)
            vb = jnp.logical_and(ib >= 0, ib < new_blk).astype(F32)
            valid_parts.append(jnp.where(lane1 < SLC_BLOCK, va, vb))
            has_new = has_new | (ia == new_blk).astype(jnp.int32) | (ib == new_blk).astype(jnp.int32)
        s = s + jnp.concatenate(bias_parts, axis=1)
        valid = jnp.concatenate(valid_parts, axis=1) > 0.5
        new_row = nslc_ref[0]
        s_new = jnp.sum(q8k_f * new_row, axis=-1, keepdims=True) + b0
        p, pn = _softmax_with_new(s, valid, s_new, has_new > 0)
        o_groups.append(_dot(p.astype(BF16), ks) + pn * new_row)
    f_slc = _to_f_form(jnp.where(row2 < R_C, o_groups[0], o_groups[1]))

    win = win_ref[0, 0]
    new_w = nwin_ref[0]
    s = _dot_nt(q8k, win.astype(BF16)) + twin_ref[...]
    s_new = jnp.sum(q8k_f * new_w, axis=-1, keepdims=True) + b0
    p, pn = _softmax_with_new(s, s == s, s_new, True)
    f_win = _to_f_form(_dot(p.astype(BF16), win.astype(BF16)) + pn * new_w)
    rows = _iota((w_len, 2 * KV_W), 0)
    nw_ref[0] = jnp.where(rows == w_len - 1, jnp.broadcast_to(new_w, (w_len, 2 * KV_W)),
                          pltpu.roll(win, w_len - 1, 0))

    sig = jnp.broadcast_to(_sigmoid(gate_ref[0]), (8, LANES))
    row1 = _iota((8, LANES), 0)
    total = jnp.zeros((8, D_C), F32)
    for k, f in enumerate((fcmp_ref[0], f_slc, f_win)):
        gk = jnp.sum(jnp.where(lane1 == k * H_C + row1, sig, 0.0), axis=-1, keepdims=True)
        total += gk * f
    o_ref[0] = jnp.sum(total, axis=0, keepdims=True)


def _nsa_attn_decode(qc, gate, new_slc, new_win, fcmp, tsel, twin, b0, cache_slc5, cache_win4,
                     page_table, sel_idx, layer, n_top):
    db, n_pages = page_table.shape
    new_blk = n_pages * PAGE // SLC_BLOCK
    w_len = cache_win4.shape[2]
    per_page = PAGE // SLC_BLOCK
    const = lambda shape: pl.BlockSpec(shape, lambda b, pt, si: (0,) * len(shape))
    per_b = lambda shape: pl.BlockSpec(shape, lambda b, pt, si: (b,) + (0,) * (len(shape) - 1))

    def block_spec(g, k):
        def imap(b, pt, si):
            blk = jnp.clip(si[(b * KV_C + g) * n_top + k], 0, new_blk - 1)
            return (layer, pt[b, blk // per_page] * per_page + blk % per_page, 0, 0)
        return pl.BlockSpec((1, 1, SLC_BLOCK, 2 * KV_W), imap)

    grid_spec = pltpu.PrefetchScalarGridSpec(
        num_scalar_prefetch=2,
        grid=(db,),
        in_specs=[per_b((1, 1, D_C)), per_b((1, 1, LANES)), per_b((1, 1, 2 * KV_W)),
                  per_b((1, 1, 2 * KV_W)), per_b((1, 8, D_C)),
                  const(tsel.shape), const(twin.shape), const((8, 1)),
                  pl.BlockSpec((1, 1, w_len, 2 * KV_W), lambda b, pt, si: (layer, b, 0, 0))]
                 + [block_spec(g, k) for g in range(KV_C) for k in range(n_top)],
        out_specs=(per_b((1, 1, D_C)), per_b((1, w_len, 2 * KV_W))),
    )
    return pl.pallas_call(
        functools.partial(_nsa_attn_decode_kernel, n_top=n_top, new_blk=new_blk),
        out_shape=(jax.ShapeDtypeStruct((db, 1, D_C), F32),
                   jax.ShapeDtypeStruct((db, w_len, 2 * KV_W), F32)),
        grid_spec=grid_spec,
        compiler_params=_cparams(("arbitrary",)),
        name="nsa_attn_decode",
    )(page_table, sel_idx, qc, gate, new_slc, new_win, fcmp, tsel, twin, b0, cache_win4,
      *([cache_slc5] * (KV_C * n_top)))


def _block_diag(mats):
    n = len(mats)
    rows = []
    for a, m in enumerate(mats):
        rows.append(jnp.concatenate([m if a == c else jnp.zeros_like(m) for c in range(n)], axis=-1))
    return jnp.concatenate(rows, axis=-2)


def _nsa_weights(cmp_pe, cmp_w1, cmp_w2):
    pe2 = jnp.concatenate([cmp_pe, cmp_pe], axis=-1)
    w1bd = jnp.stack([_block_diag([cmp_w1[kv]] * KV_C) for kv in range(2)]).astype(BF16)
    zero = jnp.zeros((HEAD_DIM, 4 * HEAD_DIM), F32)
    w2rep = []
    for kv in range(2):
        rep = jnp.concatenate([cmp_w2[kv]] * 4, axis=-1)
        w2rep.append(jnp.stack([jnp.concatenate([rep, zero], axis=0),
                                jnp.concatenate([zero, rep], axis=0)]))
    w2rep = jnp.stack(w2rep).astype(BF16)
    w2heads = jnp.concatenate([w2rep[:, 0], w2rep[:, 1]], axis=-1)
    return pe2, w1bd, w2rep, w2heads


def kernel(x_prompt, x_sample, cache_sb_kv, cache_cmp_kv, cache_slc_kv, cache_win_kv, page_table,
           rel_bias, norm_mix_pre, norm_mix_post, w_in, cmp_pe, cmp_w1, cmp_w2, gmlp_ws, gmlp_b,
           norm_group_out, w_out, norm_ffn_pre, norm_ffn_post, w_ffn_up, w_ffn_down):
    depth = w_in.shape[0]
    bsz, seq, _ = x_prompt.shape
    db = x_sample.shape[0]
    n_pages = page_table.shape[1]
    past = n_pages * PAGE
    n_phys = cache_sb_kv.shape[1]
    w_len = cache_win_kv.shape[2]
    assert x_sample.shape[1] == 1 and seq % 512 == 0 and seq // SLC_BLOCK <= SLC_BLOCK
    assert n_pages % PAGES_PER_STEP == 0 and w_len == WINDOW and past >= WINDOW

    t_prompt = bsz * seq
    tm_p = 512
    n_sb_dec = -(-(past + 1) // SLC_BLOCK)
    n_pad_dec = -(-n_sb_dec // LANES) * LANES
    n_top_dec = min(TOP_N, n_sb_dec)
    n_chunks_dec = past // CMP_STRIDE

    n_q = seq // TQ_ATT
    n_diag = min(n_q, REL_MAX_DIST // TK_ATT + 2)
    tz = _bias_table(rel_bias, n_diag, TQ_ATT, TK_ATT, TK_ATT, 1, -1, 0)
    tz = jnp.swapaxes(tz, 0, 1)
    bias_cmp_p = _bias_table(rel_bias, n_q, TQ_ATT, seq // CMP_STRIDE, TQ_ATT, 1, -CMP_STRIDE,
                             -(CMP_LEN - 1))
    bias_cmp_d = _bias_table(rel_bias, 1, 8, n_chunks_dec, 0, 0, -CMP_STRIDE,
                             past - (CMP_LEN - 1))[:, 0, 0, :]
    n_tsel = -(-(n_sb_dec + 1) // 8) * 8
    tsel = _bias_table(rel_bias, 1, n_tsel, LANES, 0, -SLC_BLOCK, -1, past)[:, 0]
    tsel = jnp.swapaxes(tsel, 0, 1)
    tsel = jnp.concatenate([tsel[:, :, :SLC_BLOCK], tsel[:, :, :SLC_BLOCK]], axis=-1)
    twin = _bias_table(rel_bias, 1, 8, w_len, 0, 0, -1, w_len)[:, 0, 0, :]
    b0 = rel_bias[0].reshape(H_C, 1)

    cache_sb4 = cache_sb_kv.reshape(depth, n_phys, PAGE, 2 * D_A)
    cache_cmp4 = cache_cmp_kv.reshape(depth, n_phys, PAGE, 2 * KV_W)
    cache_slc5 = cache_slc_kv.reshape(depth, n_phys * (PAGE // SLC_BLOCK), SLC_BLOCK, 2 * KV_W)
    cache_win4 = cache_win_kv.reshape(depth, db, w_len, 2 * KV_W)

    xp = x_prompt.reshape(t_prompt, D_MODEL)
    xs = x_sample.reshape(db, D_MODEL)
    outs = {k: [] for k in ("p_sb", "p_cmp", "p_slc", "p_win", "s_sb", "s_cmp", "s_slc", "s_win", "s_gv")}
    row = lambda v: v.reshape(1, -1)

    for l in range(depth):
        w_in_l = jnp.pad(w_in[l], ((0, 0), (0, NP_IN - N_IN))).astype(BF16)
        w_out_l = w_out[l].astype(BF16)
        w_up_l = w_ffn_up[l].astype(BF16)
        w_down_l = w_ffn_down[l].astype(BF16)
        pe2, w1bd, w2rep, w2heads = _nsa_weights(cmp_pe[l], cmp_w1[l], cmp_w2[l])
        g_pre, g_post = row(norm_mix_pre[l]), row(norm_mix_post[l])
        g_grp = row(norm_group_out[l])
        gf_pre, gf_post = row(norm_ffn_pre[l]), row(norm_ffn_post[l])

        z = _project(xp, g_pre, w_in_l, tm_p)
        z3 = z.reshape(bsz, seq, NP_IN)
        o_a = _sb_prompt(z3)
        bias_rows = jnp.repeat(gmlp_b[l].T, CB, axis=1)
        o_b = _gmlp_prompt(z3, gmlp_ws[l], bias_rows)
        kvc = _compress_prompt(z3, pe2, w1bd, w2rep)
        o_cmp, sel = _nsa_cmp_prompt(z3, kvc, bias_cmp_p)
        o_c = _nsa_attn_prompt(z3, sel, tz, o_cmp)
        xp = _mix_out(o_a.reshape(t_prompt, D_A), o_b.reshape(t_prompt, D_B),
                      o_c.reshape(t_prompt, D_C), xp, g_grp, w_out_l, g_post, tm_p)
        xp = _ffn(xp, gf_pre, w_up_l, w_down_l, gf_post, tm_p, D_FF // 2)
        outs["p_sb"].append(z3[:, :, C_KA:C_KA + 2 * D_A].reshape(bsz, seq, 2, H_A, HEAD_DIM))
        kv_shape = (bsz, seq, 2, KV_C, HEAD_DIM)
        outs["p_cmp"].append(z3[:, :, C_CK:C_CK + 2 * KV_W].reshape(kv_shape))
        outs["p_slc"].append(z3[:, :, C_SK:C_SK + 2 * KV_W].reshape(kv_shape))
        n_win = min(WINDOW, seq)
        outs["p_win"].append(z3[:, seq - n_win:, C_WK:C_WK + 2 * KV_W]
                             .reshape(bsz, n_win, 2, KV_C, HEAD_DIM))

        zs = _project(xs, g_pre, w_in_l, db)
        zs3 = zs.reshape(db, 1, NP_IN)
        o_a_s = _sb_decode(zs3[:, :, C_QA:C_QA + D_A], cache_sb4, page_table, l)
        w_row = jnp.repeat(gmlp_ws[l][:, 0, 0], CB).reshape(1, D_B)
        b_row = jnp.repeat(gmlp_b[l][:, 0], CB).reshape(1, D_B)
        o_b_s, vn_s = _gmlp_sample(zs[:, C_UB:C_UB + D_B], zs[:, C_VB:C_VB + D_B], w_row, b_row)
        qc_s = zs3[:, :, C_QC:C_QC + D_C]
        f_cmp, imp = _nsa_cmp_decode(qc_s, cache_cmp4, page_table, l, pe2, w1bd, w2heads,
                                     bias_cmp_d, n_pad_dec)
        idx = _topk_decode(imp[:, :KV_C, :].reshape(db * KV_C, n_pad_dec), past // SLC_BLOCK, n_sb_dec)
        sel_idx = idx[:, :n_top_dec].reshape(-1)
        gate_s = zs3[:, :, C_GATE:C_GATE + LANES]
        new_slc = zs3[:, :, C_SK:C_SK + 2 * KV_W]
        new_win = zs3[:, :, C_WK:C_WK + 2 * KV_W]
        o_c_s, win_out = _nsa_attn_decode(qc_s, gate_s, new_slc, new_win, f_cmp, tsel, twin, b0,
                                          cache_slc5, cache_win4, page_table, sel_idx, l, n_top_dec)
        xs = _mix_out(o_a_s.reshape(db, D_A), o_b_s, o_c_s.reshape(db, D_C), xs, g_grp, w_out_l,
                      g_post, db)
        xs = _ffn(xs, gf_pre, w_up_l, w_down_l, gf_post, db, D_FF // 2)
        outs["s_sb"].append(zs[:, C_KA:C_KA + 2 * D_A].reshape(db, 1, 2, H_A, HEAD_DIM))
        outs["s_cmp"].append(zs[:, C_CK:C_CK + 2 * KV_W].reshape(db, 1, 2, KV_C, HEAD_DIM))
        outs["s_slc"].append(zs[:, C_SK:C_SK + 2 * KV_W].reshape(db, 1, 2, KV_C, HEAD_DIM))
        outs["s_win"].append(win_out.reshape(db, w_len, 2, KV_C, HEAD_DIM))
        outs["s_gv"].append(vn_s.reshape(db, 1, D_B))

    st = lambda k: jnp.stack(outs[k])
    return (xp.reshape(bsz, seq, D_MODEL), xs.reshape(db, 1, D_MODEL), st("p_sb"), st("p_cmp"),
            st("p_slc"), st("p_win"), st("s_sb"), st("s_cmp"), st("s_slc"), st("s_win"), st("s_gv"))
```

```python
import functools
import math

import jax
import jax.numpy as jnp
import numpy as np
from jax import lax
from jax.experimental import pallas as pl
from jax.experimental.pallas import tpu as pltpu

F32 = jnp.float32
BF16 = jnp.bfloat16

D_MODEL = 1024
HEAD_DIM = 64
D_A = 256
H_A = 4
D_B = 256
G_B = 4
CB = 64
D_C = 512
H_C = 8
KV_C = 2
R_C = 4
KV_W = 128
N_IN = 3 * D_A + 2 * D_B + D_C + 6 * KV_W + 3 * H_C
NP_IN = 2688
CHUNK = 128
CMP_LEN = 32
CMP_STRIDE = 16
SLC_BLOCK = 64
TOP_N = 16
N_LOCAL = 2
WINDOW = 512
NUM_BUCKETS = 32
REL_MAX_DIST = 2048
D_FF = 2816
EPS = 1e-6
SCALE = HEAD_DIM ** -0.5
PAGE = 128

C_QA, C_KA, C_VA, C_UB, C_VB, C_QC = 0, 256, 512, 768, 1024, 1280
C_CK, C_CV, C_SK, C_SV, C_WK, C_WV, C_GATE = 1792, 1920, 2048, 2176, 2304, 2432, 2560

LANES = 128
TQ_ATT = 128
TK_ATT = 128
TQ_NSA = 256
TK_NSA = 256
VMEM_LIMIT = 56 * 1024 * 1024
NEG = -1e30
DECAYED = -110.0
BIG_SCORE = 3e38
PAGES_PER_STEP = 8


def _cparams(sem):
    return pltpu.CompilerParams(dimension_semantics=sem, vmem_limit_bytes=VMEM_LIMIT)


def _dot(a, b):
    return jnp.dot(a, b, preferred_element_type=F32)


def _dot_nt(a, b):
    return lax.dot_general(a, b, (((1,), (1,)), ((), ())), preferred_element_type=F32)


def _rms(x, g):
    return x * lax.rsqrt(jnp.mean(x * x, axis=-1, keepdims=True) + EPS) * g


def _sigmoid(x):
    return 1.0 / (1.0 + jnp.exp(-x))


def _gelu_tanh(x):
    return 0.5 * x * (1.0 + jnp.tanh(math.sqrt(2.0 / math.pi) * (x + 0.044715 * (x * x * x))))


def _iota(shape, dim):
    return lax.broadcasted_iota(jnp.int32, shape, dim)


def _bucket_thresholds():
    n = np.arange(0, 1 << 15)
    exact = NUM_BUCKETS // 2
    nf = np.maximum(n, 1).astype(np.float32)
    big = exact + (np.log(nf / np.float32(exact)) / np.float32(math.log(REL_MAX_DIST / exact))
                   * np.float32(NUM_BUCKETS - exact)).astype(np.int32)
    bucket = np.where(n < exact, n, np.minimum(big, NUM_BUCKETS - 1))
    assert np.all(np.diff(bucket) >= 0)
    return [int(np.argmax(bucket >= k)) for k in range(1, NUM_BUCKETS)]


_THR = _bucket_thresholds()


def _bias_table_kernel(tab_ref, o_ref, *, a, rs, cs, c0):
    shape = o_ref.shape[2:]
    dist = a * pl.program_id(0) + rs * _iota(shape, 0) + cs * _iota(shape, 1) + c0
    outs = [jnp.full(shape, tab_ref[h], F32) for h in range(H_C)]
    for k in range(1, NUM_BUCKETS):
        ge = dist >= _THR[k - 1]
        for h in range(H_C):
            outs[h] = jnp.where(ge, tab_ref[k * H_C + h], outs[h])
    for h in range(H_C):
        o_ref[h, 0] = outs[h]


def _bias_table(rel_bias, steps, rows, cols, a, rs, cs, c0):
    return pl.pallas_call(
        functools.partial(_bias_table_kernel, a=a, rs=rs, cs=cs, c0=c0),
        out_shape=jax.ShapeDtypeStruct((H_C, steps, rows, cols), F32),
        grid=(steps,),
        in_specs=[pl.BlockSpec(memory_space=pltpu.SMEM)],
        out_specs=pl.BlockSpec((H_C, 1, rows, cols), lambda i: (0, i, 0, 0)),
        compiler_params=_cparams(("arbitrary",)),
        name="bias_table",
    )(rel_bias.reshape(-1))


def _proj_kernel(x_ref, g_ref, w_ref, o_ref):
    h = _rms(x_ref[...], g_ref[...])
    o_ref[...] = _dot(h.astype(BF16), w_ref[...])


def _project(x, g, w, tm):
    t = x.shape[0]
    return pl.pallas_call(
        _proj_kernel,
        out_shape=jax.ShapeDtypeStruct((t, NP_IN), F32),
        grid=(t // tm,),
        in_specs=[pl.BlockSpec((tm, D_MODEL), lambda i: (i, 0)),
                  pl.BlockSpec((1, D_MODEL), lambda i: (0, 0)),
                  pl.BlockSpec((D_MODEL, NP_IN), lambda i: (0, 0))],
        out_specs=pl.BlockSpec((tm, NP_IN), lambda i: (i, 0)),
        compiler_params=_cparams(("parallel",)),
        name="in_proj",
    )(x, g, w)


def _mixout_kernel(oa_ref, ob_ref, oc_ref, x_ref, gg_ref, w_ref, gp_ref, o_ref):
    gg = gg_ref[...]
    a = _rms(oa_ref[...], gg[:, :D_A]).astype(BF16)
    b = _rms(ob_ref[...], gg[:, D_A:D_A + D_B]).astype(BF16)
    c = _rms(oc_ref[...], gg[:, D_A + D_B:]).astype(BF16)
    y = (_dot(a, w_ref[0:D_A, :]) + _dot(b, w_ref[D_A:D_A + D_B, :])
         + _dot(c, w_ref[D_A + D_B:, :]))
    o_ref[...] = x_ref[...] + _rms(y, gp_ref[...])


def _mix_out(oa, ob, oc, x, gg, w, gp, tm):
    t = x.shape[0]
    row = lambda width: pl.BlockSpec((tm, width), lambda i: (i, 0))
    full = lambda r, c: pl.BlockSpec((r, c), lambda i: (0, 0))
    return pl.pallas_call(
        _mixout_kernel,
        out_shape=jax.ShapeDtypeStruct((t, D_MODEL), F32),
        grid=(t // tm,),
        in_specs=[row(D_A), row(D_B), row(D_C), row(D_MODEL), full(1, D_MODEL),
                  full(D_MODEL, D_MODEL), full(1, D_MODEL)],
        out_specs=row(D_MODEL),
        compiler_params=_cparams(("parallel",)),
        name="mix_out",
    )(oa, ob, oc, x, gg, w, gp)


def _ffn_kernel(x_ref, gpre_ref, wg_ref, wu_ref, wd_ref, gpost_ref, o_ref, h_ref, acc_ref):
    j = pl.program_id(1)

    @pl.when(j == 0)
    def _():
        h_ref[...] = _rms(x_ref[...], gpre_ref[...]).astype(BF16)
        acc_ref[...] = jnp.zeros_like(acc_ref)

    h = h_ref[...]
    g = _dot(h, wg_ref[...])
    u = _dot(h, wu_ref[...])
    act = (g * _sigmoid(g)) * u
    acc_ref[...] += _dot(act.astype(BF16), wd_ref[...])

    @pl.when(j == pl.num_programs(1) - 1)
    def _():
        o_ref[...] = x_ref[...] + _rms(acc_ref[...], gpost_ref[...])


def _ffn(x, gpre, w_up, w_down, gpost, tm, tf):
    t = x.shape[0]
    nf = D_FF // tf
    return pl.pallas_call(
        _ffn_kernel,
        out_shape=jax.ShapeDtypeStruct((t, D_MODEL), F32),
        grid=(t // tm, nf),
        in_specs=[pl.BlockSpec((tm, D_MODEL), lambda i, j: (i, 0)),
                  pl.BlockSpec((1, D_MODEL), lambda i, j: (0, 0)),
                  pl.BlockSpec((D_MODEL, tf), lambda i, j: (0, j)),
                  pl.BlockSpec((D_MODEL, tf), lambda i, j: (0, j + nf)),
                  pl.BlockSpec((tf, D_MODEL), lambda i, j: (j, 0)),
                  pl.BlockSpec((1, D_MODEL), lambda i, j: (0, 0))],
        out_specs=pl.BlockSpec((tm, D_MODEL), lambda i, j: (i, 0)),
        scratch_shapes=[pltpu.VMEM((tm, D_MODEL), BF16), pltpu.VMEM((tm, D_MODEL), F32)],
        compiler_params=_cparams(("parallel", "arbitrary")),
        name="ffn",
    )(x, gpre, w_up, w_up, w_down, gpost)


def _log_sig_pair(z):
    soft = jnp.log1p(jnp.exp(-jnp.abs(z)))
    return jnp.minimum(z, 0.0) - soft, jnp.minimum(-z, 0.0) - soft


def _suffix_sum_exclusive(x, upper_bf16):
    hi = x.astype(BF16)
    rem = x - hi.astype(F32)
    mid = rem.astype(BF16)
    lo = (rem - mid.astype(F32)).astype(BF16)
    return _dot(hi, upper_bf16) + _dot(mid, upper_bf16) + _dot(lo, upper_bf16)


def _strict_upper(n):
    return jnp.where(_iota((n, n), 0) > _iota((n, n), 1), 1.0, 0.0).astype(BF16)


def _sb_prompt_kernel(q_ref, k_ref, v_ref, o_ref, run_ref, acc_ref, *, tq, tk):
    i = pl.program_id(2)
    lane = _iota((tq, LANES), 1)
    qs = q_ref[0] * SCALE
    qh = [jnp.where(lane < HEAD_DIM, qs, 0.0).astype(BF16),
          jnp.where(lane >= HEAD_DIM, qs, 0.0).astype(BF16)]
    upper = _strict_upper(tk)
    qpos = i * tq + _iota((tq, tk), 0)
    run_ref[...] = jnp.zeros_like(run_ref)
    acc_ref[...] = jnp.zeros_like(acc_ref)

    def cond(c):
        j, alive = c
        return jnp.logical_and(j >= 0, alive > 0)

    def body(c):
        j, _ = c
        ks = pl.multiple_of(j * tk, tk)
        kt = k_ref[0, pl.ds(ks, tk), :].astype(BF16)
        vt = v_ref[0, pl.ds(ks, tk), :].astype(BF16)
        mask = (ks + _iota((tq, tk), 1)) < qpos
        worst = jnp.float32(-jnp.inf)
        for h in range(2):
            z = _dot_nt(qh[h], kt)
            lb, lk = _log_sig_pair(z)
            lk = jnp.where(mask, lk, 0.0)
            run = run_ref[h]
            after = _suffix_sum_exclusive(lk, upper) + run
            a = jnp.where(mask, jnp.exp(lb + after), 0.0)
            acc_ref[h] += _dot(a.astype(BF16), vt)
            new_run = after[:, 0:1] + lk[:, 0:1]
            run_ref[h] = new_run
            worst = jnp.maximum(worst, jnp.max(new_run))
        return j - 1, (worst > DECAYED).astype(jnp.int32)

    n_tiles = ((i + 1) * tq) // tk
    lax.while_loop(cond, body, (n_tiles - 1, jnp.int32(1)))
    o_ref[0] = jnp.where(lane < HEAD_DIM, acc_ref[0], acc_ref[1])


def _sb_prompt(z3, tq=256, tk=TK_ATT):
    b, s, _ = z3.shape
    return pl.pallas_call(
        functools.partial(_sb_prompt_kernel, tq=tq, tk=tk),
        out_shape=jax.ShapeDtypeStruct((b, s, D_A), F32),
        grid=(b, D_A // LANES, s // tq),
        in_specs=[pl.BlockSpec((1, tq, LANES), lambda bi, hp, i: (bi, i, C_QA // LANES + hp)),
                  pl.BlockSpec((1, s, LANES), lambda bi, hp, i: (bi, 0, C_KA // LANES + hp)),
                  pl.BlockSpec((1, s, LANES), lambda bi, hp, i: (bi, 0, C_VA // LANES + hp))],
        out_specs=pl.BlockSpec((1, tq, LANES), lambda bi, hp, i: (bi, i, hp)),
        scratch_shapes=[pltpu.VMEM((2, tq, 1), F32), pltpu.VMEM((2, tq, LANES), F32)],
        compiler_params=_cparams(("parallel", "parallel", "arbitrary")),
        name="sb_prompt",
    )(z3, z3, z3)


def _head_rows(row_vec, width):
    full = jnp.broadcast_to(row_vec, (8, width))
    return jnp.where(_iota((8, width), 1) // HEAD_DIM == _iota((8, width), 0), full, 0.0)


def _sb_decode_kernel(pt_ref, alive_ref, q_ref, run_in_ref, acc_in_ref, *rest, n_pages_step, n_steps):
    del pt_ref
    pages = rest[:n_pages_step]
    run_ref, acc_ref, o_ref, live_ref = rest[n_pages_step:]
    b = pl.program_id(0)
    step = pl.program_id(1)

    @pl.when(step == 0)
    def _():
        run_ref[...] = run_in_ref[...]
        acc_ref[...] = acc_in_ref[...]
        live_ref[0] = alive_ref[b]

    @pl.when(live_ref[0] > 0)
    def _():
        q8 = (_head_rows(q_ref[0], D_A) * SCALE).astype(BF16)
        upper = _strict_upper(PAGE)
        run = run_ref[0][:, 0:1]
        acc = acc_ref[0]
        for page in pages:
            kt = page[0, 0, 0:D_A, :].astype(BF16)
            vt = page[0, 0, D_A:2 * D_A, :].astype(BF16)
            z = _dot(q8, kt)
            lb, lk = _log_sig_pair(z)
            after = _suffix_sum_exclusive(lk, upper) + run
            a = jnp.exp(lb + after)
            acc = acc + _dot_nt(a.astype(BF16), vt)
            run = after[:, 0:1] + lk[:, 0:1]
        run_ref[0] = jnp.broadcast_to(run, (8, LANES))
        acc_ref[0] = acc
        head_rows = _iota((8, 1), 0) < H_A
        live_ref[0] = (jnp.max(jnp.where(head_rows, run, -jnp.inf)) > DECAYED).astype(jnp.int32)

    @pl.when(step == n_steps - 1)
    def _():
        o_ref[0] = jnp.sum(jnp.where(_own_lanes(D_A), acc_ref[0], 0.0), axis=0, keepdims=True)


def _sb_decode_phase(qa, cache_t, page_table, layer, alive, run, acc, first_back, n_pages_step,
                     n_steps):
    db, n_pages = page_table.shape

    def page_spec(k):
        def imap(b, s, pt, al):
            page = pt[b, n_pages - 1 - first_back - (s * n_pages_step + k)]
            return (layer, jnp.where(al[b] > 0, page, 0), 0, 0)
        return pl.BlockSpec((1, 1, 2 * D_A, PAGE), imap)

    per_b = lambda shape: pl.BlockSpec(shape, lambda b, s, pt, al: (b,) + (0,) * (len(shape) - 1))
    grid_spec = pltpu.PrefetchScalarGridSpec(
        num_scalar_prefetch=2,
        grid=(db, n_steps),
        in_specs=[per_b((1, 1, D_A)), per_b((1, 8, LANES)), per_b((1, 8, D_A))]
                 + [page_spec(k) for k in range(n_pages_step)],
        out_specs=(per_b((1, 8, LANES)), per_b((1, 8, D_A)), per_b((1, 1, D_A))),
        scratch_shapes=[pltpu.SMEM((1,), jnp.int32)],
    )
    return pl.pallas_call(
        functools.partial(_sb_decode_kernel, n_pages_step=n_pages_step, n_steps=n_steps),
        out_shape=(jax.ShapeDtypeStruct((db, 8, LANES), F32), jax.ShapeDtypeStruct((db, 8, D_A), F32),
                   jax.ShapeDtypeStruct((db, 1, D_A), F32)),
        grid_spec=grid_spec,
        compiler_params=_cparams(("arbitrary", "arbitrary")),
        name="sb_decode",
    )(page_table, alive, qa, run, acc, *([cache_t] * n_pages_step))


SB_FIRST_PAGES = 4


def _sb_decode(qa, cache_t, page_table, layer):
    db, n_pages = page_table.shape
    first = min(SB_FIRST_PAGES, n_pages)
    ones = jnp.ones((db,), jnp.int32)
    run0 = jnp.zeros((db, 8, LANES), F32)
    acc0 = jnp.zeros((db, 8, D_A), F32)
    run1, acc1, o1 = _sb_decode_phase(qa, cache_t, page_table, layer, ones, run0, acc0, 0, first, 1)
    rest = n_pages - first
    if rest == 0:
        return o1
    per_step = max(d for d in range(1, 13) if rest % d == 0)
    alive = (jnp.max(run1[:, :H_A, 0], axis=1) > DECAYED).astype(jnp.int32)
    return lax.cond(
        jnp.any(alive > 0),
        lambda: _sb_decode_phase(qa, cache_t, page_table, layer, alive, run1, acc1, first, per_step,
                                 rest // per_step)[2],
        lambda: o1)


def _layernorm(x):
    xc = x - jnp.mean(x, axis=-1, keepdims=True)
    return xc * lax.rsqrt(jnp.mean(xc * xc, axis=-1, keepdims=True) + EPS)


def _gmlp_prompt_kernel(u_ref, v_ref, w_ref, b_ref, o_ref, *, n_chunks):
    tril = _iota((CHUNK, CHUNK), 0) >= _iota((CHUNK, CHUNK), 1)
    ws = [jnp.where(tril, w_ref[g], 0.0).astype(BF16) for g in range(G_B)]
    group = _iota((CHUNK, D_B), 1) // CB
    for c in range(n_chunks):
        rows = slice(c * CHUNK, (c + 1) * CHUNK)
        vn = _layernorm(v_ref[0, rows, :]).astype(BF16)
        mixed = b_ref[...]
        for g in range(G_B):
            mixed = mixed + jnp.where(group == g, _dot(ws[g], vn), 0.0)
        o_ref[0, rows, :] = u_ref[0, rows, :] * mixed


def _gmlp_prompt(z3, ws, bias_rows, n_chunks=4):
    b, s, _ = z3.shape
    tr = n_chunks * CHUNK
    return pl.pallas_call(
        functools.partial(_gmlp_prompt_kernel, n_chunks=n_chunks),
        out_shape=jax.ShapeDtypeStruct((b, s, D_B), F32),
        grid=(b, s // tr),
        in_specs=[pl.BlockSpec((1, tr, D_B), lambda bi, i: (bi, i, C_UB // D_B)),
                  pl.BlockSpec((1, tr, D_B), lambda bi, i: (bi, i, C_VB // D_B)),
                  pl.BlockSpec((G_B, CHUNK, CHUNK), lambda bi, i: (0, 0, 0)),
                  pl.BlockSpec((CHUNK, D_B), lambda bi, i: (0, 0))],
        out_specs=pl.BlockSpec((1, tr, D_B), lambda bi, i: (bi, i, 0)),
        compiler_params=_cparams(("parallel", "parallel")),
        name="gmlp_prompt",
    )(z3, z3, ws, bias_rows)


def _gmlp_sample_kernel(u_ref, v_ref, w_ref, b_ref, o_ref, vn_ref):
    vn = _layernorm(v_ref[...])
    vn_ref[...] = vn
    o_ref[...] = u_ref[...] * (w_ref[...] * vn + b_ref[...])


def _gmlp_sample(u, v, w_row, b_row):
    return pl.pallas_call(
        _gmlp_sample_kernel,
        out_shape=(jax.ShapeDtypeStruct(u.shape, F32), jax.ShapeDtypeStruct(u.shape, F32)),
        name="gmlp_sample",
    )(u, v, w_row, b_row)


def _compress_hidden(load_rows, pe_ref, w1_fn, n_chunks, width):
    hid_a = jnp.zeros((n_chunks, width), F32)
    hid_b = jnp.zeros((n_chunks, width), F32)
    for j in range(CMP_STRIDE):
        x = load_rows(j)
        hid_a += _dot((x + pe_ref[j:j + 1, :]).astype(BF16), w1_fn(j))
        hid_b += _dot((x + pe_ref[CMP_STRIDE + j:CMP_STRIDE + j + 1, :]).astype(BF16),
                      w1_fn(CMP_STRIDE + j))
    return hid_a + pltpu.roll(hid_b, n_chunks - 1, 0)


def _compress_prompt_kernel(z_ref, pe_ref, w1_ref, w2_ref, o_ref, *, n_chunks):
    hid = _compress_hidden(lambda j: z_ref[0, pl.ds(j, n_chunks, stride=CMP_STRIDE), :],
                           pe_ref.at[0], lambda j: w1_ref[0, j], n_chunks, KV_W)
    act = _gelu_tanh(hid).astype(BF16)
    valid = _iota((n_chunks, 4 * HEAD_DIM), 0) < n_chunks - 1
    for g in range(KV_C):
        o_ref[0, 0, g] = jnp.where(valid, _dot(act, w2_ref[0, g]), 0.0)


def _compress_prompt(z3, pe2, w1bd, w2rep):
    b, s, _ = z3.shape
    n_chunks = s // CMP_STRIDE
    return pl.pallas_call(
        functools.partial(_compress_prompt_kernel, n_chunks=n_chunks),
        out_shape=jax.ShapeDtypeStruct((b, 2, KV_C, n_chunks, 4 * HEAD_DIM), F32),
        grid=(b, 2),
        in_specs=[pl.BlockSpec((1, s, KV_W), lambda bi, kv: (bi, 0, C_CK // KV_W + kv)),
                  pl.BlockSpec((1, CMP_LEN, KV_W), lambda bi, kv: (kv, 0, 0)),
                  pl.BlockSpec((1, CMP_LEN, KV_W, KV_W), lambda bi, kv: (kv, 0, 0, 0)),
                  pl.BlockSpec((1, KV_C, KV_W, 4 * HEAD_DIM), lambda bi, kv: (kv, 0, 0, 0))],
        out_specs=pl.BlockSpec((1, 1, KV_C, n_chunks, 4 * HEAD_DIM),
                               lambda bi, kv: (bi, kv, 0, 0, 0)),
        compiler_params=_cparams(("parallel", "parallel")),
        name="nsa_compress_prompt",
    )(z3, pe2, w1bd, w2rep)


def _select_blocks_t(score, causal, n_top):
    n_sb = score.shape[0]
    jt = _iota(score.shape, 0)
    rank = jnp.zeros(score.shape, F32)
    for jp in range(n_sb):
        row = score[jp:jp + 1, :]
        ge = jnp.where(row >= score, 1.0, 0.0)
        gt = jnp.where(row > score, 1.0, 0.0)
        rank += jnp.where(jt > jp, ge, gt)
    return jnp.where(causal, jnp.where(rank < n_top, 1.0, 0.0), 0.0)


def _nsa_cmp_prompt_kernel(q0_ref, q1_ref, kvc_ref, bias_ref, ocmp_ref, sel_ref, *, n_sb, n_cb):
    i = pl.program_id(1)
    tq = TQ_ATT
    n_pad = kvc_ref.shape[3]
    lane_head = _iota((tq, 4 * HEAD_DIM), 1) // HEAD_DIM
    n_idx = _iota((tq, n_pad), 1)
    dist = (i * tq + _iota((tq, n_pad), 0)) - (n_idx * CMP_STRIDE + (CMP_LEN - 1))
    mask = jnp.logical_and(dist >= 0, n_idx < n_cb)
    c0 = _iota((n_sb, n_pad), 1) * CMP_STRIDE
    s0 = _iota((n_sb, n_pad), 0) * SLC_BLOCK
    ov_t = jnp.where(jnp.logical_and(c0 < s0 + SLC_BLOCK, c0 + CMP_LEN > s0), 1.0, 0.0).astype(BF16)
    t_row = i * tq + _iota((n_sb, tq), 1)
    cur = t_row // SLC_BLOCK
    jt = _iota((n_sb, tq), 0)
    causal = jt <= cur
    forced = jnp.logical_or(jt == 0, jt > cur - N_LOCAL)
    sel_t = []
    for g, q_ref in enumerate((q0_ref, q1_ref)):
        qs = q_ref[0] * SCALE
        qm = jnp.concatenate([jnp.where(lane_head == r, qs, 0.0) for r in range(R_C)],
                             axis=0).astype(BF16)
        kc = kvc_ref[0, 0, g].astype(BF16)
        vc = kvc_ref[0, 1, g].astype(BF16)
        s_all = _dot_nt(qm, kc)
        o_acc = jnp.zeros((tq, 4 * HEAD_DIM), F32)
        imp_t = jnp.zeros((n_sb, tq), F32)
        for r in range(R_C):
            s = s_all[r * tq:(r + 1) * tq] + bias_ref[R_C * g + r, 0]
            sm = jnp.where(mask, s, NEG)
            m = jnp.max(sm, axis=-1, keepdims=True)
            p = jnp.where(mask, jnp.exp(sm - m), 0.0)
            p = p / jnp.maximum(jnp.sum(p, axis=-1, keepdims=True), 1e-30)
            pb = p.astype(BF16)
            o_acc += jnp.where(lane_head == r, _dot(pb, vc), 0.0)
            imp_t += _dot_nt(ov_t, pb)
        ocmp_ref[0, :, g * 4 * HEAD_DIM:(g + 1) * 4 * HEAD_DIM] = o_acc
        score = jnp.where(causal, jnp.where(forced, BIG_SCORE, imp_t), -1.0)
        sel_t.append(_select_blocks_t(score, causal, min(TOP_N, n_sb)))
    pad = jnp.zeros((SLC_BLOCK - n_sb, tq), F32)
    sel_ref[0] = jnp.concatenate([sel_t[0], pad, sel_t[1], pad] if n_sb < SLC_BLOCK else sel_t,
                                 axis=0)


def _nsa_cmp_prompt(z3, kvc, bias_cmp):
    b, s, _ = z3.shape
    n_sb = s // SLC_BLOCK
    n_cb = (s - CMP_LEN) // CMP_STRIDE + 1
    n_pad = kvc.shape[3]
    wq = 4 * HEAD_DIM
    return pl.pallas_call(
        functools.partial(_nsa_cmp_prompt_kernel, n_sb=n_sb, n_cb=n_cb),
        out_shape=(jax.ShapeDtypeStruct((b, s, D_C), F32),
                   jax.ShapeDtypeStruct((b, 2 * SLC_BLOCK, s), F32)),
        grid=(b, s // TQ_ATT),
        in_specs=[pl.BlockSpec((1, TQ_ATT, wq), lambda bi, i: (bi, i, C_QC // wq)),
                  pl.BlockSpec((1, TQ_ATT, wq), lambda bi, i: (bi, i, C_QC // wq + 1)),
                  pl.BlockSpec((1, 2, KV_C, n_pad, wq), lambda bi, i: (bi, 0, 0, 0, 0)),
                  pl.BlockSpec((H_C, 1, TQ_ATT, n_pad), lambda bi, i: (0, i, 0, 0))],
        out_specs=(pl.BlockSpec((1, TQ_ATT, D_C), lambda bi, i: (bi, i, 0)),
                   pl.BlockSpec((1, 2 * SLC_BLOCK, TQ_ATT), lambda bi, i: (bi, 0, i))),
        compiler_params=_cparams(("parallel", "parallel")),
        name="nsa_cmp_prompt",
    )(z3, z3, kvc, bias_cmp)


def _flash_step_t(s, valid_f, vt, m_ref, l_ref, acc_ref):
    valid = jnp.concatenate([valid_f] * R_C, axis=1) > 0.5
    sm = jnp.where(valid, s, NEG)
    m_prev = m_ref[...]
    m_new = jnp.maximum(m_prev, jnp.max(sm, axis=0, keepdims=True))
    alpha = jnp.exp(m_prev - m_new)
    p = jnp.where(valid, jnp.exp(sm - m_new), 0.0)
    l_ref[...] = alpha * l_ref[...] + jnp.sum(p, axis=0, keepdims=True)
    acc_ref[...] = alpha * acc_ref[...] + _dot(vt, p.astype(BF16))
    m_ref[...] = m_new


def _nsa_attn_prompt_kernel(q_ref, sk_ref, svt_ref, wk_ref, wvt_ref, selt_ref, tz_ref, gate_ref,
                            ocmp_ref, o_ref, m_ref, l_ref, acc_ref, *, n_diag):
    g = pl.program_id(1)
    i = pl.program_id(2)
    tq, tk, tb = TQ_NSA, TK_NSA, TK_ATT
    wq = 4 * HEAD_DIM
    lane = _iota((tq, LANES), 1)
    own_half = (lane // HEAD_DIM) == g
    q = q_ref[0]
    cols = []
    for r in range(R_C):
        a = pltpu.roll(q, (wq - r * HEAD_DIM) % wq, 1)[:, :LANES]
        both = jnp.where(g == 0, a, pltpu.roll(a, HEAD_DIM, 1))
        cols.append((jnp.where(own_half, both, 0.0) * SCALE).T)
    qt = jnp.concatenate(cols, axis=1).astype(BF16)
    selt = selt_ref[0].astype(BF16)
    qpos = i * tq + _iota((tk, tq), 1)
    key_in_tile = _iota((tk, tq), 0)

    def reset():
        m_ref[...] = jnp.full(m_ref.shape, NEG, F32)
        l_ref[...] = jnp.zeros_like(l_ref)
        acc_ref[...] = jnp.zeros_like(acc_ref)

    def bias_tile(jt):
        d0 = (i * tq - jt * tk) // tb
        steps = range(-(tk // tb - 1), tq // tb)
        tabs = {e: tz_ref[jnp.clip(d0 + e, 0, n_diag - 1), 0] for e in steps}
        rows = []
        for ck in range(tk // tb):
            rows.append(jnp.concatenate(
                [tabs[cq - ck][:, r * tb:(r + 1) * tb] for r in range(R_C) for cq in range(tq // tb)],
                axis=1))
        return jnp.concatenate(rows, axis=0)

    def slc_body(jt, carry):
        ks = pl.multiple_of(jt * tk, tk)
        kt = sk_ref[0, pl.ds(ks, tk), :].astype(BF16)
        vt = svt_ref[0, jt].astype(BF16)
        s = _dot(kt, qt) + bias_tile(jt)
        target = (g * SLC_BLOCK + (tk // SLC_BLOCK) * jt + (_iota((tk, LANES), 0) // SLC_BLOCK))
        expand = jnp.where(_iota((tk, LANES), 1) == target, 1.0, 0.0).astype(BF16)
        chosen = _dot(expand, selt)
        valid_f = jnp.where(ks + key_in_tile <= qpos, chosen, 0.0)
        _flash_step_t(s, valid_f, vt, m_ref, l_ref, acc_ref)
        return carry

    def win_body(jt, carry):
        ks = pl.multiple_of(jt * tk, tk)
        kt = wk_ref[0, pl.ds(ks, tk), :].astype(BF16)
        vt = wvt_ref[0, jt].astype(BF16)
        s = _dot(kt, qt) + bias_tile(jt)
        dist = qpos - (ks + key_in_tile)
        valid_f = jnp.where(jnp.logical_and(dist >= 0, dist <= WINDOW), 1.0, 0.0)
        _flash_step_t(s, valid_f, vt, m_ref, l_ref, acc_ref)
        return carry

    def to_out_lanes(acc_t):
        out = jnp.zeros((tq, wq), F32)
        for r in range(R_C):
            x = acc_t[:, r * tq:(r + 1) * tq].T
            x = jnp.where(g == 0, x, pltpu.roll(x, HEAD_DIM, 1))
            x = jnp.where(lane < HEAD_DIM, x, 0.0)
            x = jnp.concatenate([x, jnp.zeros((tq, wq - LANES), F32)], axis=1)
            out += x if r == 0 else pltpu.roll(x, r * HEAD_DIM, 1)
        return out

    last_tile = ((i + 1) * tq - 1) // tk
    reset()
    lax.fori_loop(0, last_tile + 1, slc_body, 0)
    o_slc = to_out_lanes(acc_ref[...] / l_ref[...])
    reset()
    lax.fori_loop(jnp.maximum(i * tq - WINDOW, 0) // tk, last_tile + 1, win_body, 0)
    o_win = to_out_lanes(acc_ref[...] / l_ref[...])

    sig = _sigmoid(gate_ref[0])
    lane_head = _iota((tq, wq), 1) // HEAD_DIM
    out = jnp.zeros((tq, wq), F32)
    for k, branch in enumerate((ocmp_ref[0], o_slc, o_win)):
        gk = jnp.zeros((tq, wq), F32)
        for r in range(R_C):
            col = jnp.sum(jnp.where(lane == k * H_C + R_C * g + r, sig, 0.0), axis=-1, keepdims=True)
            gk = jnp.where(lane_head == r, col, gk)
        out += gk * branch
    o_ref[0] = out


def _value_tiles(kv_t):
    b, _, s = kv_t.shape
    v_t = kv_t[:, KV_W:, :].reshape(b, KV_W, s // TK_NSA, TK_NSA)
    return jnp.swapaxes(v_t, 1, 2)


def _nsa_attn_prompt(z3, slc_t, win_t, sel_t, tz, ocmp):
    b, s, _ = z3.shape
    wq = 4 * HEAD_DIM
    n_diag = tz.shape[0]
    n_kt = s // TK_NSA
    k_spec = lambda col: pl.BlockSpec((1, s, KV_W), lambda bi, g, i: (bi, 0, col // KV_W))
    vt_spec = pl.BlockSpec((1, n_kt, KV_W, TK_NSA), lambda bi, g, i: (bi, 0, 0, 0))
    return pl.pallas_call(
        functools.partial(_nsa_attn_prompt_kernel, n_diag=n_diag),
        out_shape=jax.ShapeDtypeStruct((b, s, D_C), F32),
        grid=(b, KV_C, s // TQ_NSA),
        in_specs=[pl.BlockSpec((1, TQ_NSA, wq), lambda bi, g, i: (bi, i, C_QC // wq + g)),
                  k_spec(C_SK), vt_spec, k_spec(C_WK), vt_spec,
                  pl.BlockSpec((1, 2 * SLC_BLOCK, TQ_NSA), lambda bi, g, i: (bi, 0, i)),
                  pl.BlockSpec((n_diag, 1, TK_ATT, R_C * TQ_ATT), lambda bi, g, i: (0, g, 0, 0)),
                  pl.BlockSpec((1, TQ_NSA, LANES), lambda bi, g, i: (bi, i, C_GATE // LANES)),
                  pl.BlockSpec((1, TQ_NSA, wq), lambda bi, g, i: (bi, i, g))],
        out_specs=pl.BlockSpec((1, TQ_NSA, wq), lambda bi, g, i: (bi, i, g)),
        scratch_shapes=[pltpu.VMEM((1, R_C * TQ_NSA), F32), pltpu.VMEM((1, R_C * TQ_NSA), F32),
                        pltpu.VMEM((KV_W, R_C * TQ_NSA), F32)],
        compiler_params=_cparams(("parallel", "parallel", "arbitrary")),
        name="nsa_attn_prompt",
    )(z3, z3, _value_tiles(slc_t), z3, _value_tiles(win_t), sel_t, tz, z3, ocmp)


def _own_lanes(width):
    return _iota((8, width), 1) // HEAD_DIM == _iota((8, width), 0)


def _nsa_cmp_decode_kernel(pt_ref, q_ref, pe_ref, w1_ref, w2_ref, bias_ref, *rest,
                           n_steps, n_sb):
    del pt_ref
    pages = rest[:PAGES_PER_STEP]
    f_ref, imp_ref, buf_ref = rest[PAGES_PER_STEP:]
    step = pl.program_id(1)
    for k, page in enumerate(pages):
        row0 = pl.multiple_of((step * PAGES_PER_STEP + k) * PAGE, PAGE)
        buf_ref[0, pl.ds(row0, PAGE), :] = page[0, 0, 0:KV_W, :].T
        buf_ref[1, pl.ds(row0, PAGE), :] = page[0, 0, KV_W:2 * KV_W, :].T

    @pl.when(step == n_steps - 1)
    def _():
        n_chunks = buf_ref.shape[1] // CMP_STRIDE
        n_cb = n_chunks - 1
        kv8 = []
        for kv in range(2):
            hid = _compress_hidden(
                lambda j, kv=kv: buf_ref[kv, pl.ds(j, n_chunks, stride=CMP_STRIDE), :],
                pe_ref.at[kv], lambda j, kv=kv: w1_ref[kv, j], n_chunks, KV_W)
            kv8.append(_dot(_gelu_tanh(hid).astype(BF16), w2_ref[kv]).astype(BF16))
        k8, v8 = kv8
        q8 = (_head_rows(q_ref[0], D_C) * SCALE).astype(BF16)
        s = _dot_nt(q8, k8) + bias_ref[...]
        n_idx = _iota((8, n_chunks), 1)
        mask = n_idx < n_cb
        sm = jnp.where(mask, s, NEG)
        m = jnp.max(sm, axis=-1, keepdims=True)
        p = jnp.where(mask, jnp.exp(sm - m), 0.0)
        p = p / jnp.maximum(jnp.sum(p, axis=-1, keepdims=True), 1e-30)
        pb = p.astype(BF16)
        f_ref[0] = jnp.where(_own_lanes(D_C), _dot(pb, v8), 0.0)
        n_pad = imp_ref.shape[2]
        c0 = _iota((n_chunks, n_pad), 0) * CMP_STRIDE
        j_idx = _iota((n_chunks, n_pad), 1)
        s0 = j_idx * SLC_BLOCK
        ov = jnp.logical_and(jnp.logical_and(c0 < s0 + SLC_BLOCK, c0 + CMP_LEN > s0),
                             jnp.logical_and(_iota((n_chunks, n_pad), 0) < n_cb, j_idx < n_sb))
        imp8 = _dot(pb, jnp.where(ov, 1.0, 0.0).astype(BF16))
        row = _iota((8, n_pad), 0)
        g0 = jnp.sum(jnp.where(row < R_C, imp8, 0.0), axis=0, keepdims=True)
        g1 = jnp.sum(jnp.where(row >= R_C, imp8, 0.0), axis=0, keepdims=True)
        imp_ref[0] = jnp.where(row == 0, g0, jnp.where(row == 1, g1, 0.0))


def _nsa_cmp_decode(qc, cache4, page_table, layer, pe2, w1bd, w2heads, bias_cmp, n_pad):
    db, n_pages = page_table.shape
    past = n_pages * PAGE
    n_steps = n_pages // PAGES_PER_STEP
    n_sb = -(-(past + 1) // SLC_BLOCK)
    n_chunks = past // CMP_STRIDE
    const = lambda shape: pl.BlockSpec(shape, lambda b, s, pt: (0,) * len(shape))

    def page_spec(k):
        def imap(b, s, pt):
            return (layer, pt[b, s * PAGES_PER_STEP + k], 0, 0)
        return pl.BlockSpec((1, 1, 2 * KV_W, PAGE), imap)

    grid_spec = pltpu.PrefetchScalarGridSpec(
        num_scalar_prefetch=1,
        grid=(db, n_steps),
        in_specs=[pl.BlockSpec((1, 1, D_C), lambda b, s, pt: (b, 0, 0)),
                  const((2, CMP_LEN, KV_W)), const((2, CMP_LEN, KV_W, KV_W)),
                  const((2, KV_W, D_C)), const((8, n_chunks))]
                 + [page_spec(k) for k in range(PAGES_PER_STEP)],
        out_specs=(pl.BlockSpec((1, 8, D_C), lambda b, s, pt: (b, 0, 0)),
                   pl.BlockSpec((1, 8, n_pad), lambda b, s, pt: (b, 0, 0))),
        scratch_shapes=[pltpu.VMEM((2, past, KV_W), F32)],
    )
    return pl.pallas_call(
        functools.partial(_nsa_cmp_decode_kernel, n_steps=n_steps, n_sb=n_sb),
        out_shape=(jax.ShapeDtypeStruct((db, 8, D_C), F32),
                   jax.ShapeDtypeStruct((db, 8, n_pad), F32)),
        grid_spec=grid_spec,
        compiler_params=_cparams(("parallel", "arbitrary")),
        name="nsa_cmp_decode",
    )(page_table, qc, pe2, w1bd, w2heads, bias_cmp, *([cache4] * PAGES_PER_STEP))


def _topk_decode_kernel(imp_ref, idx_ref, *, cur, n_sb, n_top):
    shape = imp_ref.shape
    lane = _iota(shape, 1)
    lane_f = lane.astype(F32)
    causal = jnp.logical_and(lane <= cur, lane < n_sb)
    forced = jnp.logical_or(lane == 0, lane > cur - N_LOCAL)
    score = jnp.where(causal, jnp.where(forced, BIG_SCORE, imp_ref[...]), -1.0)
    out_lane = _iota(idx_ref.shape, 1)
    out = jnp.full(idx_ref.shape, -1, jnp.int32)
    for k in range(n_top):
        m = jnp.max(score, axis=-1, keepdims=True)
        first = jnp.min(jnp.where(score == m, lane_f, 1e9), axis=-1, keepdims=True)
        pick = jnp.where(m > -0.5, first, -1.0).astype(jnp.int32)
        out = jnp.where(out_lane == k, pick, out)
        score = jnp.where(lane_f == first, -2.0, score)
    idx_ref[...] = out


def _topk_decode(imp_rows, cur, n_sb):
    return pl.pallas_call(
        functools.partial(_topk_decode_kernel, cur=cur, n_sb=n_sb, n_top=min(TOP_N, n_sb)),
        out_shape=jax.ShapeDtypeStruct((imp_rows.shape[0], LANES), jnp.int32),
        name="nsa_topk_decode",
    )(imp_rows)


def _softmax_with_new(s, valid, s_new, has_new):
    sm = jnp.where(valid, s, NEG)
    sn = jnp.where(has_new, s_new, NEG)
    m = jnp.maximum(jnp.max(sm, axis=-1, keepdims=True), sn)
    p = jnp.where(valid, jnp.exp(sm - m), 0.0)
    pn = jnp.where(has_new, jnp.exp(sn - m), 0.0)
    inv = 1.0 / jnp.maximum(jnp.sum(p, axis=-1, keepdims=True) + pn, 1e-30)
    return p * inv, pn * inv


def _to_f_form(o8):
    x = jnp.concatenate([o8, jnp.zeros((8, D_C - 2 * KV_W), F32)], axis=1)
    row = _iota((8, D_C), 0)
    out = jnp.zeros((8, D_C), F32)
    for h in range(H_C):
        shift = (h * HEAD_DIM - (KV_W + (h // R_C) * HEAD_DIM)) % D_C
        out = jnp.where(row == h, x if shift == 0 else pltpu.roll(x, shift, 1), out)
    return jnp.where(_own_lanes(D_C), out, 0.0)


def _nsa_attn_decode_kernel(pt_ref, idx_ref, q_ref, gate_ref, nslc_ref, nwin_ref, fcmp_ref, tsel_ref,
                            twin_ref, b0_ref, win_ref, *rest, n_top, new_blk):
    del pt_ref
    pages = rest[:KV_C * n_top]
    o_ref, nw_ref = rest[KV_C * n_top:]
    b = pl.program_id(0)
    w_len = win_ref.shape[3]
    per_page = PAGE // SLC_BLOCK

    qb = jnp.broadcast_to(q_ref[0], (8, D_C))
    row5 = _iota((8, D_C), 0)
    qr = jnp.zeros((8, D_C), F32)
    for h in range(H_C):
        shift = ((h // R_C) * HEAD_DIM - h * HEAD_DIM) % D_C
        qr = jnp.where(row5 == h, qb if shift == 0 else pltpu.roll(qb, shift, 1), qr)
    lane2 = _iota((8, 2 * KV_W), 1)
    row2 = _iota((8, 2 * KV_W), 0)
    q8k_f = jnp.where(lane2 // HEAD_DIM == row2 // R_C, qr[:, :2 * KV_W], 0.0) * SCALE
    q8k = q8k_f.astype(BF16)
    b0 = b0_ref[...]

    lane1 = _iota((8, LANES), 1)
    o_groups = []
    for g in range(KV_C):
        ids = [idx_ref[(b * KV_C + g) * n_top + k] for k in range(n_top)]
        kst = jnp.concatenate([pages[g * n_top + k][0, 0] for k in range(n_top)], axis=1).astype(BF16)
        s = _dot(q8k, kst)
        bias_parts, valid_parts = [], []
        has_new = jnp.int32(0)
        for k in range(n_top):
            blk = ids[k]
            in_cache = jnp.clip(blk, 0, new_blk - 1)
            bias_parts.append(tsel_ref[in_cache // per_page])
            ok = jnp.logical_and(blk >= 0, blk < new_blk).astype(F32)
            valid_parts.append(jnp.where(lane1 // SLC_BLOCK == in_cache % per_page, ok, 0.0))
            has_new = has_new | (blk == new_blk).astype(jnp.int32)
        s = s + jnp.concatenate(bias_parts, axis=1)
        valid = jnp.concatenate(valid_parts, axis=1) > 0.5
        new_row = nslc_ref[0]
        s_new = jnp.sum(q8k_f * new_row, axis=-1, keepdims=True) + b0
        p, pn = _softmax_with_new(s, valid, s_new, has_new > 0)
        o_groups.append(_dot_nt(p.astype(BF16), kst) + pn * new_row)
    f_slc = _to_f_form(jnp.where(row2 < R_C, o_groups[0], o_groups[1]))

    win_t = win_ref[0, 0]
    new_w = nwin_ref[0]
    s = _dot(q8k, win_t.astype(BF16)) + twin_ref[...]
    s_new = jnp.sum(q8k_f * new_w, axis=-1, keepdims=True) + b0
    p, pn = _softmax_with_new(s, s == s, s_new, True)
    f_win = _to_f_form(_dot_nt(p.astype(BF16), win_t.astype(BF16)) + pn * new_w)
    sq = (2 * KV_W, 2 * KV_W)
    new_col = jnp.sum(jnp.where(_iota(sq, 0) == _iota(sq, 1), jnp.broadcast_to(new_w, sq), 0.0),
                      axis=-1, keepdims=True)
    nw_ref[0] = jnp.where(_iota((2 * KV_W, w_len), 1) == w_len - 1, new_col,
                          pltpu.roll(win_t, w_len - 1, 1))

    sig = jnp.broadcast_to(_sigmoid(gate_ref[0]), (8, LANES))
    row1 = _iota((8, LANES), 0)
    total = jnp.zeros((8, D_C), F32)
    for k, f in enumerate((fcmp_ref[0], f_slc, f_win)):
        gk = jnp.sum(jnp.where(lane1 == k * H_C + row1, sig, 0.0), axis=-1, keepdims=True)
        total += gk * f
    o_ref[0] = jnp.sum(total, axis=0, keepdims=True)


def _nsa_attn_decode(qc, gate, new_slc, new_win, fcmp, tsel, twin, b0, cache_slc_t, cache_win_t,
                     page_table, sel_idx, layer, n_top):
    db, n_pages = page_table.shape
    new_blk = n_pages * PAGE // SLC_BLOCK
    w_len = cache_win_t.shape[3]
    per_page = PAGE // SLC_BLOCK
    const = lambda shape: pl.BlockSpec(shape, lambda b, pt, si: (0,) * len(shape))
    per_b = lambda shape: pl.BlockSpec(shape, lambda b, pt, si: (b,) + (0,) * (len(shape) - 1))

    def page_spec(g, k):
        def imap(b, pt, si):
            blk = jnp.clip(si[(b * KV_C + g) * n_top + k], 0, new_blk - 1)
            return (layer, pt[b, blk // per_page], 0, 0)
        return pl.BlockSpec((1, 1, 2 * KV_W, PAGE), imap)

    grid_spec = pltpu.PrefetchScalarGridSpec(
        num_scalar_prefetch=2,
        grid=(db,),
        in_specs=[per_b((1, 1, D_C)), per_b((1, 1, LANES)), per_b((1, 1, 2 * KV_W)),
                  per_b((1, 1, 2 * KV_W)), per_b((1, 8, D_C)),
                  const(tsel.shape), const(twin.shape), const((8, 1)),
                  pl.BlockSpec((1, 1, 2 * KV_W, w_len), lambda b, pt, si: (layer, b, 0, 0))]
                 + [page_spec(g, k) for g in range(KV_C) for k in range(n_top)],
        out_specs=(per_b((1, 1, D_C)), per_b((1, 2 * KV_W, w_len))),
    )
    return pl.pallas_call(
        functools.partial(_nsa_attn_decode_kernel, n_top=n_top, new_blk=new_blk),
        out_shape=(jax.ShapeDtypeStruct((db, 1, D_C), F32),
                   jax.ShapeDtypeStruct((db, 2 * KV_W, w_len), F32)),
        grid_spec=grid_spec,
        compiler_params=_cparams(("arbitrary",)),
        name="nsa_attn_decode",
    )(page_table, sel_idx, qc, gate, new_slc, new_win, fcmp, tsel, twin, b0, cache_win_t,
      *([cache_slc_t] * (KV_C * n_top)))


def _block_diag(mats):
    n = len(mats)
    rows = []
    for a, m in enumerate(mats):
        rows.append(jnp.concatenate([m if a == c else jnp.zeros_like(m) for c in range(n)], axis=-1))
    return jnp.concatenate(rows, axis=-2)


def _nsa_weights(cmp_pe, cmp_w1, cmp_w2):
    pe2 = jnp.concatenate([cmp_pe, cmp_pe], axis=-1)
    w1bd = jnp.stack([_block_diag([cmp_w1[kv]] * KV_C) for kv in range(2)]).astype(BF16)
    zero = jnp.zeros((HEAD_DIM, 4 * HEAD_DIM), F32)
    w2rep = []
    for kv in range(2):
        rep = jnp.concatenate([cmp_w2[kv]] * 4, axis=-1)
        w2rep.append(jnp.stack([jnp.concatenate([rep, zero], axis=0),
                                jnp.concatenate([zero, rep], axis=0)]))
    w2rep = jnp.stack(w2rep).astype(BF16)
    w2heads = jnp.concatenate([w2rep[:, 0], w2rep[:, 1]], axis=-1)
    return pe2, w1bd, w2rep, w2heads


def kernel(x_prompt, x_sample, cache_sb_kv, cache_cmp_kv, cache_slc_kv, cache_win_kv, page_table,
           rel_bias, norm_mix_pre, norm_mix_post, w_in, cmp_pe, cmp_w1, cmp_w2, gmlp_ws, gmlp_b,
           norm_group_out, w_out, norm_ffn_pre, norm_ffn_post, w_ffn_up, w_ffn_down):
    depth = w_in.shape[0]
    bsz, seq, _ = x_prompt.shape
    db = x_sample.shape[0]
    n_pages = page_table.shape[1]
    past = n_pages * PAGE
    n_phys = cache_sb_kv.shape[1]
    w_len = cache_win_kv.shape[2]
    assert x_sample.shape[1] == 1 and seq % 512 == 0 and seq // SLC_BLOCK <= SLC_BLOCK
    assert n_pages % PAGES_PER_STEP == 0 and w_len == WINDOW and past >= WINDOW

    t_prompt = bsz * seq
    tm_p = 512
    n_sb_dec = -(-(past + 1) // SLC_BLOCK)
    n_pad_dec = -(-n_sb_dec // LANES) * LANES
    n_top_dec = min(TOP_N, n_sb_dec)
    n_chunks_dec = past // CMP_STRIDE

    n_q = seq // TQ_ATT
    n_diag = min(n_q, REL_MAX_DIST // TK_ATT + 2)
    tz = _bias_table(rel_bias, n_diag, TK_ATT, TQ_ATT, TK_ATT, -1, 1, 0)
    tz = tz.reshape(KV_C, R_C, n_diag, TK_ATT, TQ_ATT).transpose(2, 0, 3, 1, 4)
    tz = tz.reshape(n_diag, KV_C, TK_ATT, R_C * TQ_ATT)
    bias_cmp_p = _bias_table(rel_bias, n_q, TQ_ATT, seq // CMP_STRIDE, TQ_ATT, 1, -CMP_STRIDE,
                             -(CMP_LEN - 1))
    bias_cmp_d = _bias_table(rel_bias, 1, 8, n_chunks_dec, 0, 0, -CMP_STRIDE,
                             past - (CMP_LEN - 1))[:, 0, 0, :]
    n_tsel = -(-n_pages // 8) * 8
    tsel = _bias_table(rel_bias, 1, n_tsel, PAGE, 0, -PAGE, -1, past)[:, 0]
    tsel = jnp.swapaxes(tsel, 0, 1)
    twin = _bias_table(rel_bias, 1, 8, w_len, 0, 0, -1, w_len)[:, 0, 0, :]
    b0 = rel_bias[0].reshape(H_C, 1)

    pos_minor = lambda c: c.transpose(0, 1, 3, 4, 5, 2).reshape(c.shape[0], c.shape[1], -1, c.shape[2])
    cache_sb_t = pos_minor(cache_sb_kv)
    cache_cmp_t = pos_minor(cache_cmp_kv)
    cache_slc_t = pos_minor(cache_slc_kv)
    cache_win_t = pos_minor(cache_win_kv)
    from_pos_minor = lambda a, heads: jnp.moveaxis(
        a.reshape(a.shape[:-2] + (2, heads, HEAD_DIM, a.shape[-1])), -1, -4)

    xp = x_prompt.reshape(t_prompt, D_MODEL)
    xs = x_sample.reshape(db, D_MODEL)
    outs = {k: [] for k in ("p_sb", "p_cmp", "p_slc", "p_win", "s_sb", "s_cmp", "s_slc", "s_win", "s_gv")}
    row = lambda v: v.reshape(1, -1)

    for l in range(depth):
        w_in_l = jnp.pad(w_in[l], ((0, 0), (0, NP_IN - N_IN))).astype(BF16)
        w_out_l = w_out[l].astype(BF16)
        w_up_l = w_ffn_up[l].astype(BF16)
        w_down_l = w_ffn_down[l].astype(BF16)
        pe2, w1bd, w2rep, w2heads = _nsa_weights(cmp_pe[l], cmp_w1[l], cmp_w2[l])
        g_pre, g_post = row(norm_mix_pre[l]), row(norm_mix_post[l])
        g_grp = row(norm_group_out[l])
        gf_pre, gf_post = row(norm_ffn_pre[l]), row(norm_ffn_post[l])

        z = _project(xp, g_pre, w_in_l, tm_p)
        z3 = z.reshape(bsz, seq, NP_IN)
        o_a = _sb_prompt(z3)
        bias_rows = jnp.repeat(gmlp_b[l].T, CB, axis=1)
        o_b = _gmlp_prompt(z3, gmlp_ws[l], bias_rows)
        kvc = _compress_prompt(z3, pe2, w1bd, w2rep)
        o_cmp, sel_t = _nsa_cmp_prompt(z3, kvc, bias_cmp_p)
        slc_t = jnp.swapaxes(z3[:, :, C_SK:C_SK + 2 * KV_W], 1, 2)
        win_t = jnp.swapaxes(z3[:, :, C_WK:C_WK + 2 * KV_W], 1, 2)
        o_c = _nsa_attn_prompt(z3, slc_t, win_t, sel_t, tz, o_cmp)
        xp = _mix_out(o_a.reshape(t_prompt, D_A), o_b.reshape(t_prompt, D_B),
                      o_c.reshape(t_prompt, D_C), xp, g_grp, w_out_l, g_post, tm_p)
        xp = _ffn(xp, gf_pre, w_up_l, w_down_l, gf_post, tm_p, D_FF // 2)
        outs["p_sb"].append(z3[:, :, C_KA:C_KA + 2 * D_A].reshape(bsz, seq, 2, H_A, HEAD_DIM))
        kv_shape = (bsz, seq, 2, KV_C, HEAD_DIM)
        outs["p_cmp"].append(z3[:, :, C_CK:C_CK + 2 * KV_W].reshape(kv_shape))
        outs["p_slc"].append(from_pos_minor(slc_t, KV_C))
        n_win = min(WINDOW, seq)
        outs["p_win"].append(from_pos_minor(win_t[:, :, seq - n_win:], KV_C))

        zs = _project(xs, g_pre, w_in_l, db)
        zs3 = zs.reshape(db, 1, NP_IN)
        o_a_s = _sb_decode(zs3[:, :, C_QA:C_QA + D_A], cache_sb_t, page_table, l)
        w_row = jnp.repeat(gmlp_ws[l][:, 0, 0], CB).reshape(1, D_B)
        b_row = jnp.repeat(gmlp_b[l][:, 0], CB).reshape(1, D_B)
        o_b_s, vn_s = _gmlp_sample(zs[:, C_UB:C_UB + D_B], zs[:, C_VB:C_VB + D_B], w_row, b_row)
        qc_s = zs3[:, :, C_QC:C_QC + D_C]
        f_cmp, imp = _nsa_cmp_decode(qc_s, cache_cmp_t, page_table, l, pe2, w1bd, w2heads,
                                     bias_cmp_d, n_pad_dec)
        idx = _topk_decode(imp[:, :KV_C, :].reshape(db * KV_C, n_pad_dec), past // SLC_BLOCK, n_sb_dec)
        sel_idx = idx[:, :n_top_dec].reshape(-1)
        gate_s = zs3[:, :, C_GATE:C_GATE + LANES]
        new_slc = zs3[:, :, C_SK:C_SK + 2 * KV_W]
        new_win = zs3[:, :, C_WK:C_WK + 2 * KV_W]
        o_c_s, win_out = _nsa_attn_decode(qc_s, gate_s, new_slc, new_win, f_cmp, tsel, twin, b0,
                                          cache_slc_t, cache_win_t, page_table, sel_idx, l, n_top_dec)
        xs = _mix_out(o_a_s.reshape(db, D_A), o_b_s, o_c_s.reshape(db, D_C), xs, g_grp, w_out_l,
                      g_post, db)
        xs = _ffn(xs, gf_pre, w_up_l, w_down_l, gf_post, db, D_FF // 2)
        outs["s_sb"].append(zs[:, C_KA:C_KA + 2 * D_A].reshape(db, 1, 2, H_A, HEAD_DIM))
        outs["s_cmp"].append(zs[:, C_CK:C_CK + 2 * KV_W].reshape(db, 1, 2, KV_C, HEAD_DIM))
        outs["s_slc"].append(zs[:, C_SK:C_SK + 2 * KV_W].reshape(db, 1, 2, KV_C, HEAD_DIM))
        outs["s_win"].append(from_pos_minor(win_out, KV_C))
        outs["s_gv"].append(vn_s.reshape(db, 1, D_B))

    st = lambda k: jnp.stack(outs[k])
    return (xp.reshape(bsz, seq, D_MODEL), xs.reshape(db, 1, D_MODEL), st("p_sb"), st("p_cmp"),
            st("p_slc"), st("p_win"), st("s_sb"), st("s_cmp"), st("s_slc"), st("s_win"), st("s_gv"))
```

```python
import functools
import math

import jax
import jax.numpy as jnp
import numpy as np
from jax import lax
from jax.experimental import pallas as pl
from jax.experimental.pallas import tpu as pltpu

F32 = jnp.float32
BF16 = jnp.bfloat16

D_MODEL = 1024
HEAD_DIM = 64
D_A = 256
H_A = 4
D_B = 256
G_B = 4
CB = 64
D_C = 512
H_C = 8
KV_C = 2
R_C = 4
KV_W = 128
N_IN = 3 * D_A + 2 * D_B + D_C + 6 * KV_W + 3 * H_C
NP_IN = 2688
CHUNK = 128
CMP_LEN = 32
CMP_STRIDE = 16
SLC_BLOCK = 64
TOP_N = 16
N_LOCAL = 2
WINDOW = 512
NUM_BUCKETS = 32
REL_MAX_DIST = 2048
D_FF = 2816
EPS = 1e-6
SCALE = HEAD_DIM ** -0.5
PAGE = 128

C_QA, C_KA, C_VA, C_UB, C_VB, C_QC = 0, 256, 512, 768, 1024, 1280
C_CK, C_CV, C_SK, C_SV, C_WK, C_WV, C_GATE = 1792, 1920, 2048, 2176, 2304, 2432, 2560

LANES = 128
TQ_ATT = 128
TK_ATT = 128
TQ_NSA = 256
TK_NSA = 256
VMEM_LIMIT = 56 * 1024 * 1024
NEG = -1e30
DECAYED = -110.0
BIG_SCORE = 3e38
PAGES_PER_STEP = 8


def _cparams(sem):
    return pltpu.CompilerParams(dimension_semantics=sem, vmem_limit_bytes=VMEM_LIMIT)


def _dot(a, b):
    return jnp.dot(a, b, preferred_element_type=F32)


def _dot_nt(a, b):
    return lax.dot_general(a, b, (((1,), (1,)), ((), ())), preferred_element_type=F32)


def _rms(x, g):
    return x * lax.rsqrt(jnp.mean(x * x, axis=-1, keepdims=True) + EPS) * g


def _sigmoid(x):
    return 1.0 / (1.0 + jnp.exp(-x))


def _gelu_tanh(x):
    return 0.5 * x * (1.0 + jnp.tanh(math.sqrt(2.0 / math.pi) * (x + 0.044715 * (x * x * x))))


def _iota(shape, dim):
    return lax.broadcasted_iota(jnp.int32, shape, dim)


def _bucket_thresholds():
    n = np.arange(0, 1 << 15)
    exact = NUM_BUCKETS // 2
    nf = np.maximum(n, 1).astype(np.float32)
    big = exact + (np.log(nf / np.float32(exact)) / np.float32(math.log(REL_MAX_DIST / exact))
                   * np.float32(NUM_BUCKETS - exact)).astype(np.int32)
    bucket = np.where(n < exact, n, np.minimum(big, NUM_BUCKETS - 1))
    assert np.all(np.diff(bucket) >= 0)
    return [int(np.argmax(bucket >= k)) for k in range(1, NUM_BUCKETS)]


_THR = _bucket_thresholds()


def _bias_table_kernel(tab_ref, o_ref, *, a, rs, cs, c0):
    shape = o_ref.shape[2:]
    dist = a * pl.program_id(0) + rs * _iota(shape, 0) + cs * _iota(shape, 1) + c0
    outs = [jnp.full(shape, tab_ref[h], F32) for h in range(H_C)]
    for k in range(1, NUM_BUCKETS):
        ge = dist >= _THR[k - 1]
        for h in range(H_C):
            outs[h] = jnp.where(ge, tab_ref[k * H_C + h], outs[h])
    for h in range(H_C):
        o_ref[h, 0] = outs[h]


def _bias_table(rel_bias, steps, rows, cols, a, rs, cs, c0):
    return pl.pallas_call(
        functools.partial(_bias_table_kernel, a=a, rs=rs, cs=cs, c0=c0),
        out_shape=jax.ShapeDtypeStruct((H_C, steps, rows, cols), F32),
        grid=(steps,),
        in_specs=[pl.BlockSpec(memory_space=pltpu.SMEM)],
        out_specs=pl.BlockSpec((H_C, 1, rows, cols), lambda i: (0, i, 0, 0)),
        compiler_params=_cparams(("arbitrary",)),
        name="bias_table",
    )(rel_bias.reshape(-1))


def _proj_kernel(x_ref, g_ref, w_ref, o_ref):
    h = _rms(x_ref[...], g_ref[...])
    o_ref[...] = _dot(h.astype(BF16), w_ref[...])


def _project(x, g, w, tm):
    t = x.shape[0]
    return pl.pallas_call(
        _proj_kernel,
        out_shape=jax.ShapeDtypeStruct((t, NP_IN), F32),
        grid=(t // tm,),
        in_specs=[pl.BlockSpec((tm, D_MODEL), lambda i: (i, 0)),
                  pl.BlockSpec((1, D_MODEL), lambda i: (0, 0)),
                  pl.BlockSpec((D_MODEL, NP_IN), lambda i: (0, 0))],
        out_specs=pl.BlockSpec((tm, NP_IN), lambda i: (i, 0)),
        compiler_params=_cparams(("parallel",)),
        name="in_proj",
    )(x, g, w)


def _mixout_kernel(oa_ref, ob_ref, oc_ref, x_ref, gg_ref, w_ref, gp_ref, o_ref):
    gg = gg_ref[...]
    a = _rms(oa_ref[...], gg[:, :D_A]).astype(BF16)
    b = _rms(ob_ref[...], gg[:, D_A:D_A + D_B]).astype(BF16)
    c = _rms(oc_ref[...], gg[:, D_A + D_B:]).astype(BF16)
    y = (_dot(a, w_ref[0:D_A, :]) + _dot(b, w_ref[D_A:D_A + D_B, :])
         + _dot(c, w_ref[D_A + D_B:, :]))
    o_ref[...] = x_ref[...] + _rms(y, gp_ref[...])


def _mix_out(oa, ob, oc, x, gg, w, gp, tm):
    t = x.shape[0]
    row = lambda width: pl.BlockSpec((tm, width), lambda i: (i, 0))
    full = lambda r, c: pl.BlockSpec((r, c), lambda i: (0, 0))
    return pl.pallas_call(
        _mixout_kernel,
        out_shape=jax.ShapeDtypeStruct((t, D_MODEL), F32),
        grid=(t // tm,),
        in_specs=[row(D_A), row(D_B), row(D_C), row(D_MODEL), full(1, D_MODEL),
                  full(D_MODEL, D_MODEL), full(1, D_MODEL)],
        out_specs=row(D_MODEL),
        compiler_params=_cparams(("parallel",)),
        name="mix_out",
    )(oa, ob, oc, x, gg, w, gp)


def _ffn_kernel(x_ref, gpre_ref, wg_ref, wu_ref, wd_ref, gpost_ref, o_ref, h_ref, acc_ref):
    j = pl.program_id(1)

    @pl.when(j == 0)
    def _():
        h_ref[...] = _rms(x_ref[...], gpre_ref[...]).astype(BF16)
        acc_ref[...] = jnp.zeros_like(acc_ref)

    h = h_ref[...]
    g = _dot(h, wg_ref[...])
    u = _dot(h, wu_ref[...])
    act = (g * _sigmoid(g)) * u
    acc_ref[...] += _dot(act.astype(BF16), wd_ref[...])

    @pl.when(j == pl.num_programs(1) - 1)
    def _():
        o_ref[...] = x_ref[...] + _rms(acc_ref[...], gpost_ref[...])


def _ffn(x, gpre, w_up, w_down, gpost, tm, tf):
    t = x.shape[0]
    nf = D_FF // tf
    return pl.pallas_call(
        _ffn_kernel,
        out_shape=jax.ShapeDtypeStruct((t, D_MODEL), F32),
        grid=(t // tm, nf),
        in_specs=[pl.BlockSpec((tm, D_MODEL), lambda i, j: (i, 0)),
                  pl.BlockSpec((1, D_MODEL), lambda i, j: (0, 0)),
                  pl.BlockSpec((D_MODEL, tf), lambda i, j: (0, j)),
                  pl.BlockSpec((D_MODEL, tf), lambda i, j: (0, j + nf)),
                  pl.BlockSpec((tf, D_MODEL), lambda i, j: (j, 0)),
                  pl.BlockSpec((1, D_MODEL), lambda i, j: (0, 0))],
        out_specs=pl.BlockSpec((tm, D_MODEL), lambda i, j: (i, 0)),
        scratch_shapes=[pltpu.VMEM((tm, D_MODEL), BF16), pltpu.VMEM((tm, D_MODEL), F32)],
        compiler_params=_cparams(("parallel", "arbitrary")),
        name="ffn",
    )(x, gpre, w_up, w_up, w_down, gpost)


def _log_sig_pair(z):
    soft = jnp.log1p(jnp.exp(-jnp.abs(z)))
    return jnp.minimum(z, 0.0) - soft, jnp.minimum(-z, 0.0) - soft


def _times_01_matrix(x, mat_bf16):
    hi = x.astype(BF16)
    rem = x - hi.astype(F32)
    mid = rem.astype(BF16)
    lo = (rem - mid.astype(F32)).astype(BF16)
    return _dot(hi, mat_bf16) + _dot(mid, mat_bf16) + _dot(lo, mat_bf16)


def _suffix_sum_exclusive(x, upper_bf16):
    return _times_01_matrix(x, upper_bf16)


def _strict_upper(n):
    return jnp.where(_iota((n, n), 0) > _iota((n, n), 1), 1.0, 0.0).astype(BF16)


def _sb_prompt_kernel(q_ref, k_ref, v_ref, o_ref, run_ref, acc_ref, *, tq, tk):
    i = pl.program_id(1)
    lane_head = _iota((tq, D_A), 1) // HEAD_DIM
    qs = q_ref[0] * SCALE
    qh = [jnp.where(lane_head == h, qs, 0.0).astype(BF16) for h in range(H_A)]
    upper = _strict_upper(tk)
    qpos = i * tq + _iota((tq, tk), 0)
    run_ref[...] = jnp.zeros_like(run_ref)
    acc_ref[...] = jnp.zeros_like(acc_ref)

    def cond(c):
        j, alive = c
        return jnp.logical_and(j >= 0, alive > 0)

    def body(c):
        j, _ = c
        ks = pl.multiple_of(j * tk, tk)
        kt = k_ref[0, pl.ds(ks, tk), :].astype(BF16)
        vt = v_ref[0, pl.ds(ks, tk), :].astype(BF16)
        mask = (ks + _iota((tq, tk), 1)) < qpos
        worst = jnp.float32(-jnp.inf)
        for h in range(H_A):
            z = _dot_nt(qh[h], kt)
            lb, lk = _log_sig_pair(z)
            lk = jnp.where(mask, lk, 0.0)
            run = run_ref[h]
            after = _suffix_sum_exclusive(lk, upper) + run
            a = jnp.where(mask, jnp.exp(lb + after), 0.0)
            acc_ref[h] += _dot(a.astype(BF16), vt)
            new_run = after[:, 0:1] + lk[:, 0:1]
            run_ref[h] = new_run
            worst = jnp.maximum(worst, jnp.max(new_run))
        return j - 1, (worst > DECAYED).astype(jnp.int32)

    n_tiles = ((i + 1) * tq) // tk
    lax.while_loop(cond, body, (n_tiles - 1, jnp.int32(1)))
    out = acc_ref[0]
    for h in range(1, H_A):
        out = jnp.where(lane_head == h, acc_ref[h], out)
    o_ref[0] = out


def _sb_prompt(z3, tq=256, tk=256):
    b, s, _ = z3.shape
    return pl.pallas_call(
        functools.partial(_sb_prompt_kernel, tq=tq, tk=tk),
        out_shape=jax.ShapeDtypeStruct((b, s, D_A), F32),
        grid=(b, s // tq),
        in_specs=[pl.BlockSpec((1, tq, D_A), lambda bi, i: (bi, i, C_QA // D_A)),
                  pl.BlockSpec((1, s, D_A), lambda bi, i: (bi, 0, C_KA // D_A)),
                  pl.BlockSpec((1, s, D_A), lambda bi, i: (bi, 0, C_VA // D_A))],
        out_specs=pl.BlockSpec((1, tq, D_A), lambda bi, i: (bi, i, 0)),
        scratch_shapes=[pltpu.VMEM((H_A, tq, 1), F32), pltpu.VMEM((H_A, tq, D_A), F32)],
        compiler_params=_cparams(("parallel", "arbitrary")),
        name="sb_prompt",
    )(z3, z3, z3)


def _head_rows(row_vec, width):
    full = jnp.broadcast_to(row_vec, (8, width))
    return jnp.where(_iota((8, width), 1) // HEAD_DIM == _iota((8, width), 0), full, 0.0)


def _sb_decode_kernel(pt_ref, alive_ref, q_ref, run_in_ref, acc_in_ref, *rest, n_pages_step, n_steps):
    del pt_ref
    pages = rest[:n_pages_step]
    run_ref, acc_ref, o_ref, live_ref = rest[n_pages_step:]
    b = pl.program_id(0)
    step = pl.program_id(1)

    @pl.when(step == 0)
    def _():
        run_ref[...] = run_in_ref[...]
        acc_ref[...] = acc_in_ref[...]
        live_ref[0] = alive_ref[b]

    @pl.when(live_ref[0] > 0)
    def _():
        q8 = (_head_rows(q_ref[0], D_A) * SCALE).astype(BF16)
        upper = _strict_upper(PAGE)
        run = run_ref[0][:, 0:1]
        acc = acc_ref[0]
        for page in pages:
            kt = page[0, 0, 0:D_A, :].astype(BF16)
            vt = page[0, 0, D_A:2 * D_A, :].astype(BF16)
            z = _dot(q8, kt)
            lb, lk = _log_sig_pair(z)
            after = _suffix_sum_exclusive(lk, upper) + run
            a = jnp.exp(lb + after)
            acc = acc + _dot_nt(a.astype(BF16), vt)
            run = after[:, 0:1] + lk[:, 0:1]
        run_ref[0] = jnp.broadcast_to(run, (8, LANES))
        acc_ref[0] = acc
        head_rows = _iota((8, 1), 0) < H_A
        live_ref[0] = (jnp.max(jnp.where(head_rows, run, -jnp.inf)) > DECAYED).astype(jnp.int32)

    @pl.when(step == n_steps - 1)
    def _():
        o_ref[0] = jnp.sum(jnp.where(_own_lanes(D_A), acc_ref[0], 0.0), axis=0, keepdims=True)


def _sb_decode_phase(qa, cache_t, page_table, layer, alive, run, acc, first_back, n_pages_step,
                     n_steps):
    db, n_pages = page_table.shape

    def page_spec(k):
        def imap(b, s, pt, al):
            page = pt[b, n_pages - 1 - first_back - (s * n_pages_step + k)]
            return (layer, jnp.where(al[b] > 0, page, 0), 0, 0)
        return pl.BlockSpec((1, 1, 2 * D_A, PAGE), imap)

    per_b = lambda shape: pl.BlockSpec(shape, lambda b, s, pt, al: (b,) + (0,) * (len(shape) - 1))
    grid_spec = pltpu.PrefetchScalarGridSpec(
        num_scalar_prefetch=2,
        grid=(db, n_steps),
        in_specs=[per_b((1, 1, D_A)), per_b((1, 8, LANES)), per_b((1, 8, D_A))]
                 + [page_spec(k) for k in range(n_pages_step)],
        out_specs=(per_b((1, 8, LANES)), per_b((1, 8, D_A)), per_b((1, 1, D_A))),
        scratch_shapes=[pltpu.SMEM((1,), jnp.int32)],
    )
    return pl.pallas_call(
        functools.partial(_sb_decode_kernel, n_pages_step=n_pages_step, n_steps=n_steps),
        out_shape=(jax.ShapeDtypeStruct((db, 8, LANES), F32), jax.ShapeDtypeStruct((db, 8, D_A), F32),
                   jax.ShapeDtypeStruct((db, 1, D_A), F32)),
        grid_spec=grid_spec,
        compiler_params=_cparams(("arbitrary", "arbitrary")),
        name="sb_decode",
    )(page_table, alive, qa, run, acc, *([cache_t] * n_pages_step))


SB_FIRST_PAGES = 4


def _sb_decode(qa, cache_t, page_table, layer):
    db, n_pages = page_table.shape
    first = min(SB_FIRST_PAGES, n_pages)
    ones = jnp.ones((db,), jnp.int32)
    run0 = jnp.zeros((db, 8, LANES), F32)
    acc0 = jnp.zeros((db, 8, D_A), F32)
    run1, acc1, o1 = _sb_decode_phase(qa, cache_t, page_table, layer, ones, run0, acc0, 0, first, 1)
    rest = n_pages - first
    if rest == 0:
        return o1
    per_step = max(d for d in range(1, 13) if rest % d == 0)
    alive = (jnp.max(run1[:, :H_A, 0], axis=1) > DECAYED).astype(jnp.int32)
    return lax.cond(
        jnp.any(alive > 0),
        lambda: _sb_decode_phase(qa, cache_t, page_table, layer, alive, run1, acc1, first, per_step,
                                 rest // per_step)[2],
        lambda: o1)


def _layernorm(x):
    xc = x - jnp.mean(x, axis=-1, keepdims=True)
    return xc * lax.rsqrt(jnp.mean(xc * xc, axis=-1, keepdims=True) + EPS)


def _gmlp_prompt_kernel(u_ref, v_ref, w_ref, b_ref, o_ref, *, n_chunks):
    tril = _iota((CHUNK, CHUNK), 0) >= _iota((CHUNK, CHUNK), 1)
    ws = [jnp.where(tril, w_ref[g], 0.0).astype(BF16) for g in range(G_B)]
    group = _iota((CHUNK, D_B), 1) // CB
    for c in range(n_chunks):
        rows = slice(c * CHUNK, (c + 1) * CHUNK)
        vn = _layernorm(v_ref[0, rows, :]).astype(BF16)
        mixed = b_ref[...]
        for g in range(G_B):
            mixed = mixed + jnp.where(group == g, _dot(ws[g], vn), 0.0)
        o_ref[0, rows, :] = u_ref[0, rows, :] * mixed


def _gmlp_prompt(z3, ws, bias_rows, n_chunks=4):
    b, s, _ = z3.shape
    tr = n_chunks * CHUNK
    return pl.pallas_call(
        functools.partial(_gmlp_prompt_kernel, n_chunks=n_chunks),
        out_shape=jax.ShapeDtypeStruct((b, s, D_B), F32),
        grid=(b, s // tr),
        in_specs=[pl.BlockSpec((1, tr, D_B), lambda bi, i: (bi, i, C_UB // D_B)),
                  pl.BlockSpec((1, tr, D_B), lambda bi, i: (bi, i, C_VB // D_B)),
                  pl.BlockSpec((G_B, CHUNK, CHUNK), lambda bi, i: (0, 0, 0)),
                  pl.BlockSpec((CHUNK, D_B), lambda bi, i: (0, 0))],
        out_specs=pl.BlockSpec((1, tr, D_B), lambda bi, i: (bi, i, 0)),
        compiler_params=_cparams(("parallel", "parallel")),
        name="gmlp_prompt",
    )(z3, z3, ws, bias_rows)


def _gmlp_sample_kernel(u_ref, v_ref, w_ref, b_ref, o_ref, vn_ref):
    vn = _layernorm(v_ref[...])
    vn_ref[...] = vn
    o_ref[...] = u_ref[...] * (w_ref[...] * vn + b_ref[...])


def _gmlp_sample(u, v, w_row, b_row):
    return pl.pallas_call(
        _gmlp_sample_kernel,
        out_shape=(jax.ShapeDtypeStruct(u.shape, F32), jax.ShapeDtypeStruct(u.shape, F32)),
        name="gmlp_sample",
    )(u, v, w_row, b_row)


def _compress_hidden(load_rows, pe_ref, w1_fn, n_chunks, width):
    hid_a = jnp.zeros((n_chunks, width), F32)
    hid_b = jnp.zeros((n_chunks, width), F32)
    for j in range(CMP_STRIDE):
        x = load_rows(j)
        hid_a += _dot((x + pe_ref[j:j + 1, :]).astype(BF16), w1_fn(j))
        hid_b += _dot((x + pe_ref[CMP_STRIDE + j:CMP_STRIDE + j + 1, :]).astype(BF16),
                      w1_fn(CMP_STRIDE + j))
    return hid_a + pltpu.roll(hid_b, n_chunks - 1, 0)


def _compress_prompt_kernel(z_ref, pe_ref, w1_ref, w2_ref, o_ref, *, n_chunks):
    hid = _compress_hidden(lambda j: z_ref[0, pl.ds(j, n_chunks, stride=CMP_STRIDE), :],
                           pe_ref.at[0], lambda j: w1_ref[0, j], n_chunks, KV_W)
    act = _gelu_tanh(hid).astype(BF16)
    valid = _iota((n_chunks, 4 * HEAD_DIM), 0) < n_chunks - 1
    for g in range(KV_C):
        o_ref[0, 0, g] = jnp.where(valid, _dot(act, w2_ref[0, g]), 0.0)


def _compress_prompt(z3, pe2, w1bd, w2rep):
    b, s, _ = z3.shape
    n_chunks = s // CMP_STRIDE
    return pl.pallas_call(
        functools.partial(_compress_prompt_kernel, n_chunks=n_chunks),
        out_shape=jax.ShapeDtypeStruct((b, 2, KV_C, n_chunks, 4 * HEAD_DIM), F32),
        grid=(b, 2),
        in_specs=[pl.BlockSpec((1, s, KV_W), lambda bi, kv: (bi, 0, C_CK // KV_W + kv)),
                  pl.BlockSpec((1, CMP_LEN, KV_W), lambda bi, kv: (kv, 0, 0)),
                  pl.BlockSpec((1, CMP_LEN, KV_W, KV_W), lambda bi, kv: (kv, 0, 0, 0)),
                  pl.BlockSpec((1, KV_C, KV_W, 4 * HEAD_DIM), lambda bi, kv: (kv, 0, 0, 0))],
        out_specs=pl.BlockSpec((1, 1, KV_C, n_chunks, 4 * HEAD_DIM),
                               lambda bi, kv: (bi, kv, 0, 0, 0)),
        compiler_params=_cparams(("parallel", "parallel")),
        name="nsa_compress_prompt",
    )(z3, pe2, w1bd, w2rep)


def _select_blocks_t(score, causal, n_top):
    n_sb = score.shape[0]
    jt = _iota(score.shape, 0)
    rank = jnp.zeros(score.shape, F32)
    for jp in range(n_sb):
        row = score[jp:jp + 1, :]
        ge = jnp.where(row >= score, 1.0, 0.0)
        gt = jnp.where(row > score, 1.0, 0.0)
        rank += jnp.where(jt > jp, ge, gt)
    return jnp.where(causal, jnp.where(rank < n_top, 1.0, 0.0), 0.0)


def _nsa_cmp_prompt_kernel(q0_ref, q1_ref, kvc_ref, bias_ref, ocmp_ref, sel_ref, *, n_sb, n_cb):
    i = pl.program_id(1)
    tq = TQ_ATT
    n_pad = kvc_ref.shape[3]
    lane_head = _iota((tq, 4 * HEAD_DIM), 1) // HEAD_DIM
    n_idx = _iota((tq, n_pad), 1)
    dist = (i * tq + _iota((tq, n_pad), 0)) - (n_idx * CMP_STRIDE + (CMP_LEN - 1))
    mask = jnp.logical_and(dist >= 0, n_idx < n_cb)
    c0 = _iota((n_sb, n_pad), 1) * CMP_STRIDE
    s0 = _iota((n_sb, n_pad), 0) * SLC_BLOCK
    ov_t = jnp.where(jnp.logical_and(c0 < s0 + SLC_BLOCK, c0 + CMP_LEN > s0), 1.0, 0.0).astype(BF16)
    t_row = i * tq + _iota((n_sb, tq), 1)
    cur = t_row // SLC_BLOCK
    jt = _iota((n_sb, tq), 0)
    causal = jt <= cur
    forced = jnp.logical_or(jt == 0, jt > cur - N_LOCAL)
    sel_t = []
    for g, q_ref in enumerate((q0_ref, q1_ref)):
        qs = q_ref[0] * SCALE
        qm = jnp.concatenate([jnp.where(lane_head == r, qs, 0.0) for r in range(R_C)],
                             axis=0).astype(BF16)
        kc = kvc_ref[0, 0, g].astype(BF16)
        vc = kvc_ref[0, 1, g].astype(BF16)
        s_all = _dot_nt(qm, kc)
        o_acc = jnp.zeros((tq, 4 * HEAD_DIM), F32)
        imp_t = jnp.zeros((n_sb, tq), F32)
        for r in range(R_C):
            s = s_all[r * tq:(r + 1) * tq] + bias_ref[R_C * g + r, 0]
            sm = jnp.where(mask, s, NEG)
            m = jnp.max(sm, axis=-1, keepdims=True)
            p = jnp.where(mask, jnp.exp(sm - m), 0.0)
            p = p / jnp.maximum(jnp.sum(p, axis=-1, keepdims=True), 1e-30)
            pb = p.astype(BF16)
            o_acc += jnp.where(lane_head == r, _dot(pb, vc), 0.0)
            imp_t += _dot_nt(ov_t, pb)
        ocmp_ref[0, :, g * 4 * HEAD_DIM:(g + 1) * 4 * HEAD_DIM] = o_acc
        score = jnp.where(causal, jnp.where(forced, BIG_SCORE, imp_t), -1.0)
        sel_t.append(_select_blocks_t(score, causal, min(TOP_N, n_sb)))
    pad = jnp.zeros((SLC_BLOCK - n_sb, tq), F32)
    sel_ref[0] = jnp.concatenate([sel_t[0], pad, sel_t[1], pad] if n_sb < SLC_BLOCK else sel_t,
                                 axis=0)


def _nsa_cmp_prompt(z3, kvc, bias_cmp):
    b, s, _ = z3.shape
    n_sb = s // SLC_BLOCK
    n_cb = (s - CMP_LEN) // CMP_STRIDE + 1
    n_pad = kvc.shape[3]
    wq = 4 * HEAD_DIM
    return pl.pallas_call(
        functools.partial(_nsa_cmp_prompt_kernel, n_sb=n_sb, n_cb=n_cb),
        out_shape=(jax.ShapeDtypeStruct((b, s, D_C), F32),
                   jax.ShapeDtypeStruct((b, 2 * SLC_BLOCK, s), F32)),
        grid=(b, s // TQ_ATT),
        in_specs=[pl.BlockSpec((1, TQ_ATT, wq), lambda bi, i: (bi, i, C_QC // wq)),
                  pl.BlockSpec((1, TQ_ATT, wq), lambda bi, i: (bi, i, C_QC // wq + 1)),
                  pl.BlockSpec((1, 2, KV_C, n_pad, wq), lambda bi, i: (bi, 0, 0, 0, 0)),
                  pl.BlockSpec((H_C, 1, TQ_ATT, n_pad), lambda bi, i: (0, i, 0, 0))],
        out_specs=(pl.BlockSpec((1, TQ_ATT, D_C), lambda bi, i: (bi, i, 0)),
                   pl.BlockSpec((1, 2 * SLC_BLOCK, TQ_ATT), lambda bi, i: (bi, 0, i))),
        compiler_params=_cparams(("parallel", "parallel")),
        name="nsa_cmp_prompt",
    )(z3, z3, kvc, bias_cmp)


def _flash_step_t(args, refs, n_diag):
    kt, qt, valid_f, vt, d0 = args
    m_ref, l_ref, acc_ref, tz_ref = refs
    tk, tb = kt.shape[0], TK_ATT
    nq = valid_f.shape[1] // tb
    tabs = {e: tz_ref[jnp.clip(d0 + e, 0, n_diag - 1), 0] for e in range(-(tk // tb - 1), nq)}
    bias = jnp.concatenate(
        [jnp.concatenate([tabs[cq - ck][r] for r in range(R_C) for cq in range(nq)], axis=1)
         for ck in range(tk // tb)], axis=0)
    valid = jnp.concatenate([valid_f] * R_C, axis=1) > 0.5
    sm = jnp.where(valid, _dot(kt, qt) + bias, NEG)
    m_prev = m_ref[...]
    m_new = jnp.maximum(m_prev, jnp.max(sm, axis=0, keepdims=True))
    alpha = jnp.exp(m_prev - m_new)
    p = jnp.exp((sm - m_new).astype(BF16))
    pv = _dot(jnp.concatenate([vt, jnp.ones((16, tk), BF16)], axis=0), p)
    l_ref[...] = alpha * l_ref[...] + pv[HEAD_DIM:HEAD_DIM + 1]
    acc_ref[...] = alpha * acc_ref[...] + pv[:HEAD_DIM]
    m_ref[...] = m_new


def _nsa_attn_prompt_kernel(q_ref, sk_ref, svt_ref, wk_ref, wvt_ref, selt_ref, tz_ref, gate_ref,
                            ocmp_ref, o_ref, m_ref, l_ref, acc_ref, *, n_diag):
    g = pl.program_id(1)
    i = pl.program_id(2)
    tq, tk, tb = TQ_NSA, TK_NSA, TK_ATT
    wq = 4 * HEAD_DIM
    refs = (m_ref, l_ref, acc_ref, tz_ref)
    lane = _iota((tq, LANES), 1)
    own_half = (lane // HEAD_DIM) == g
    q = q_ref[0]
    cols = []
    for r in range(R_C):
        a = pltpu.roll(q, (wq - r * HEAD_DIM) % wq, 1)[:, :LANES]
        both = jnp.where(g == 0, a, pltpu.roll(a, HEAD_DIM, 1))
        cols.append((jnp.where(own_half, both, 0.0) * SCALE).T)
    qt = jnp.concatenate(cols, axis=1).astype(BF16)
    selt = selt_ref[0].astype(BF16)
    qpos = i * tq + _iota((tk, tq), 1)
    key_in_tile = _iota((tk, tq), 0)
    own_rows = pl.multiple_of(g * HEAD_DIM, HEAD_DIM)

    def reset():
        m_ref[...] = jnp.full(m_ref.shape, NEG, F32)
        l_ref[...] = jnp.zeros_like(l_ref)
        acc_ref[...] = jnp.zeros_like(acc_ref)

    def slc_body(jt, carry):
        ks = pl.multiple_of(jt * tk, tk)
        kt = sk_ref[0, pl.ds(ks, tk), :].astype(BF16)
        vt = svt_ref[0, jt, pl.ds(own_rows, HEAD_DIM), :].astype(BF16)
        target = (g * SLC_BLOCK + (tk // SLC_BLOCK) * jt + (_iota((tk, LANES), 0) // SLC_BLOCK))
        expand = jnp.where(_iota((tk, LANES), 1) == target, 1.0, 0.0).astype(BF16)
        chosen = _dot(expand, selt)
        valid_f = jnp.where(ks + key_in_tile <= qpos, chosen, 0.0)
        _flash_step_t((kt, qt, valid_f, vt, (i * tq - ks) // tb), refs, n_diag)
        return carry

    def win_body(jt, carry):
        ks = pl.multiple_of(jt * tk, tk)
        kt = wk_ref[0, pl.ds(ks, tk), :].astype(BF16)
        vt = wvt_ref[0, jt, pl.ds(own_rows, HEAD_DIM), :].astype(BF16)
        dist = qpos - (ks + key_in_tile)
        valid_f = jnp.where(jnp.logical_and(dist >= 0, dist <= WINDOW), 1.0, 0.0)
        _flash_step_t((kt, qt, valid_f, vt, (i * tq - ks) // tb), refs, n_diag)
        return carry

    def normalised_out():
        acc_t = acc_ref[...] / l_ref[...]
        out = jnp.zeros((tq, wq), F32)
        pad_rows = jnp.zeros((LANES - HEAD_DIM, tq), F32)
        for r in range(R_C):
            x = jnp.concatenate([acc_t[:, r * tq:(r + 1) * tq], pad_rows], axis=0).T
            x = jnp.concatenate([x, jnp.zeros((tq, wq - LANES), F32)], axis=1)
            out += x if r == 0 else pltpu.roll(x, r * HEAD_DIM, 1)
        return out

    last_tile = ((i + 1) * tq - 1) // tk
    reset()
    lax.fori_loop(0, last_tile + 1, slc_body, 0)
    o_slc = normalised_out()
    reset()
    lax.fori_loop(jnp.maximum(i * tq - WINDOW, 0) // tk, last_tile + 1, win_body, 0)
    o_win = normalised_out()

    sig = _sigmoid(gate_ref[0])
    gate_lane = R_C * g + _iota((LANES, wq), 1) // HEAD_DIM
    out = jnp.zeros((tq, wq), F32)
    for k, branch in enumerate((ocmp_ref[0], o_slc, o_win)):
        place = jnp.where(_iota((LANES, wq), 0) == k * H_C + gate_lane, 1.0, 0.0).astype(BF16)
        out += _times_01_matrix(sig, place) * branch
    o_ref[0] = out


def _value_tiles(kv_t):
    b, _, s = kv_t.shape
    v_t = kv_t[:, KV_W:, :].reshape(b, KV_W, s // TK_NSA, TK_NSA)
    return jnp.swapaxes(v_t, 1, 2)


def _nsa_attn_prompt(z3, slc_t, win_t, sel_t, tz, ocmp):
    b, s, _ = z3.shape
    wq = 4 * HEAD_DIM
    n_diag = tz.shape[0]
    n_kt = s // TK_NSA
    k_spec = lambda col: pl.BlockSpec((1, s, KV_W), lambda bi, g, i: (bi, 0, col // KV_W))
    vt_spec = pl.BlockSpec((1, n_kt, KV_W, TK_NSA), lambda bi, g, i: (bi, 0, 0, 0))
    return pl.pallas_call(
        functools.partial(_nsa_attn_prompt_kernel, n_diag=n_diag),
        out_shape=jax.ShapeDtypeStruct((b, s, D_C), F32),
        grid=(b, KV_C, s // TQ_NSA),
        in_specs=[pl.BlockSpec((1, TQ_NSA, wq), lambda bi, g, i: (bi, i, C_QC // wq + g)),
                  k_spec(C_SK), vt_spec, k_spec(C_WK), vt_spec,
                  pl.BlockSpec((1, 2 * SLC_BLOCK, TQ_NSA), lambda bi, g, i: (bi, 0, i)),
                  pl.BlockSpec((n_diag, 1, R_C, TK_ATT, TQ_ATT), lambda bi, g, i: (0, g, 0, 0, 0)),
                  pl.BlockSpec((1, TQ_NSA, LANES), lambda bi, g, i: (bi, i, C_GATE // LANES)),
                  pl.BlockSpec((1, TQ_NSA, wq), lambda bi, g, i: (bi, i, g))],
        out_specs=pl.BlockSpec((1, TQ_NSA, wq), lambda bi, g, i: (bi, i, g)),
        scratch_shapes=[pltpu.VMEM((1, R_C * TQ_NSA), F32),
                        pltpu.VMEM((1, R_C * TQ_NSA), F32),
                        pltpu.VMEM((HEAD_DIM, R_C * TQ_NSA), F32)],
        compiler_params=_cparams(("parallel", "parallel", "arbitrary")),
        name="nsa_attn_prompt",
    )(z3, z3, _value_tiles(slc_t), z3, _value_tiles(win_t), sel_t, tz, z3, ocmp)


def _own_lanes(width):
    return _iota((8, width), 1) // HEAD_DIM == _iota((8, width), 0)


def _nsa_cmp_decode_kernel(pt_ref, q_ref, pe_ref, w1_ref, w2_ref, bias_ref, *rest,
                           n_steps, n_sb):
    del pt_ref
    pages = rest[:PAGES_PER_STEP]
    f_ref, imp_ref, buf_ref = rest[PAGES_PER_STEP:]
    step = pl.program_id(1)
    for k, page in enumerate(pages):
        row0 = pl.multiple_of((step * PAGES_PER_STEP + k) * PAGE, PAGE)
        buf_ref[0, pl.ds(row0, PAGE), :] = page[0, 0, 0:KV_W, :].T
        buf_ref[1, pl.ds(row0, PAGE), :] = page[0, 0, KV_W:2 * KV_W, :].T

    @pl.when(step == n_steps - 1)
    def _():
        n_chunks = buf_ref.shape[1] // CMP_STRIDE
        n_cb = n_chunks - 1
        kv8 = []
        for kv in range(2):
            hid = _compress_hidden(
                lambda j, kv=kv: buf_ref[kv, pl.ds(j, n_chunks, stride=CMP_STRIDE), :],
                pe_ref.at[kv], lambda j, kv=kv: w1_ref[kv, j], n_chunks, KV_W)
            kv8.append(_dot(_gelu_tanh(hid).astype(BF16), w2_ref[kv]).astype(BF16))
        k8, v8 = kv8
        q8 = (_head_rows(q_ref[0], D_C) * SCALE).astype(BF16)
        s = _dot_nt(q8, k8) + bias_ref[...]
        n_idx = _iota((8, n_chunks), 1)
        mask = n_idx < n_cb
        sm = jnp.where(mask, s, NEG)
        m = jnp.max(sm, axis=-1, keepdims=True)
        p = jnp.where(mask, jnp.exp(sm - m), 0.0)
        p = p / jnp.maximum(jnp.sum(p, axis=-1, keepdims=True), 1e-30)
        pb = p.astype(BF16)
        f_ref[0] = jnp.where(_own_lanes(D_C), _dot(pb, v8), 0.0)
        n_pad = imp_ref.shape[2]
        c0 = _iota((n_chunks, n_pad), 0) * CMP_STRIDE
        j_idx = _iota((n_chunks, n_pad), 1)
        s0 = j_idx * SLC_BLOCK
        ov = jnp.logical_and(jnp.logical_and(c0 < s0 + SLC_BLOCK, c0 + CMP_LEN > s0),
                             jnp.logical_and(_iota((n_chunks, n_pad), 0) < n_cb, j_idx < n_sb))
        imp8 = _dot(pb, jnp.where(ov, 1.0, 0.0).astype(BF16))
        row = _iota((8, n_pad), 0)
        g0 = jnp.sum(jnp.where(row < R_C, imp8, 0.0), axis=0, keepdims=True)
        g1 = jnp.sum(jnp.where(row >= R_C, imp8, 0.0), axis=0, keepdims=True)
        imp_ref[0] = jnp.where(row == 0, g0, jnp.where(row == 1, g1, 0.0))


def _nsa_cmp_decode(qc, cache4, page_table, layer, pe2, w1bd, w2heads, bias_cmp, n_pad):
    db, n_pages = page_table.shape
    past = n_pages * PAGE
    n_steps = n_pages // PAGES_PER_STEP
    n_sb = -(-(past + 1) // SLC_BLOCK)
    n_chunks = past // CMP_STRIDE
    const = lambda shape: pl.BlockSpec(shape, lambda b, s, pt: (0,) * len(shape))

    def page_spec(k):
        def imap(b, s, pt):
            return (layer, pt[b, s * PAGES_PER_STEP + k], 0, 0)
        return pl.BlockSpec((1, 1, 2 * KV_W, PAGE), imap)

    grid_spec = pltpu.PrefetchScalarGridSpec(
        num_scalar_prefetch=1,
        grid=(db, n_steps),
        in_specs=[pl.BlockSpec((1, 1, D_C), lambda b, s, pt: (b, 0, 0)),
                  const((2, CMP_LEN, KV_W)), const((2, CMP_LEN, KV_W, KV_W)),
                  const((2, KV_W, D_C)), const((8, n_chunks))]
                 + [page_spec(k) for k in range(PAGES_PER_STEP)],
        out_specs=(pl.BlockSpec((1, 8, D_C), lambda b, s, pt: (b, 0, 0)),
                   pl.BlockSpec((1, 8, n_pad), lambda b, s, pt: (b, 0, 0))),
        scratch_shapes=[pltpu.VMEM((2, past, KV_W), F32)],
    )
    return pl.pallas_call(
        functools.partial(_nsa_cmp_decode_kernel, n_steps=n_steps, n_sb=n_sb),
        out_shape=(jax.ShapeDtypeStruct((db, 8, D_C), F32),
                   jax.ShapeDtypeStruct((db, 8, n_pad), F32)),
        grid_spec=grid_spec,
        compiler_params=_cparams(("parallel", "arbitrary")),
        name="nsa_cmp_decode",
    )(page_table, qc, pe2, w1bd, w2heads, bias_cmp, *([cache4] * PAGES_PER_STEP))


def _topk_decode_kernel(imp_ref, idx_ref, *, cur, n_sb, n_top):
    shape = imp_ref.shape
    lane = _iota(shape, 1)
    lane_f = lane.astype(F32)
    causal = jnp.logical_and(lane <= cur, lane < n_sb)
    forced = jnp.logical_or(lane == 0, lane > cur - N_LOCAL)
    score = jnp.where(causal, jnp.where(forced, BIG_SCORE, imp_ref[...]), -1.0)
    out_lane = _iota(idx_ref.shape, 1)
    out = jnp.full(idx_ref.shape, -1, jnp.int32)
    for k in range(n_top):
        m = jnp.max(score, axis=-1, keepdims=True)
        first = jnp.min(jnp.where(score == m, lane_f, 1e9), axis=-1, keepdims=True)
        pick = jnp.where(m > -0.5, first, -1.0).astype(jnp.int32)
        out = jnp.where(out_lane == k, pick, out)
        score = jnp.where(lane_f == first, -2.0, score)
    idx_ref[...] = out


def _topk_decode(imp_rows, cur, n_sb):
    return pl.pallas_call(
        functools.partial(_topk_decode_kernel, cur=cur, n_sb=n_sb, n_top=min(TOP_N, n_sb)),
        out_shape=jax.ShapeDtypeStruct((imp_rows.shape[0], LANES), jnp.int32),
        name="nsa_topk_decode",
    )(imp_rows)


def _softmax_with_new(s, valid, s_new, has_new):
    sm = jnp.where(valid, s, NEG)
    sn = jnp.where(has_new, s_new, NEG)
    m = jnp.maximum(jnp.max(sm, axis=-1, keepdims=True), sn)
    p = jnp.where(valid, jnp.exp(sm - m), 0.0)
    pn = jnp.where(has_new, jnp.exp(sn - m), 0.0)
    inv = 1.0 / jnp.maximum(jnp.sum(p, axis=-1, keepdims=True) + pn, 1e-30)
    return p * inv, pn * inv


def _to_f_form(o8):
    x = jnp.concatenate([o8, jnp.zeros((8, D_C - 2 * KV_W), F32)], axis=1)
    row = _iota((8, D_C), 0)
    out = jnp.zeros((8, D_C), F32)
    for h in range(H_C):
        shift = (h * HEAD_DIM - (KV_W + (h // R_C) * HEAD_DIM)) % D_C
        out = jnp.where(row == h, x if shift == 0 else pltpu.roll(x, shift, 1), out)
    return jnp.where(_own_lanes(D_C), out, 0.0)


def _nsa_attn_decode_kernel(pt_ref, idx_ref, q_ref, gate_ref, nslc_ref, nwin_ref, fcmp_ref, tsel_ref,
                            twin_ref, b0_ref, win_ref, *rest, n_top, new_blk):
    del pt_ref
    pages = rest[:KV_C * n_top]
    o_ref, nw_ref = rest[KV_C * n_top:]
    b = pl.program_id(0)
    w_len = win_ref.shape[3]
    per_page = PAGE // SLC_BLOCK

    qb = jnp.broadcast_to(q_ref[0], (8, D_C))
    row5 = _iota((8, D_C), 0)
    qr = jnp.zeros((8, D_C), F32)
    for h in range(H_C):
        shift = ((h // R_C) * HEAD_DIM - h * HEAD_DIM) % D_C
        qr = jnp.where(row5 == h, qb if shift == 0 else pltpu.roll(qb, shift, 1), qr)
    lane2 = _iota((8, 2 * KV_W), 1)
    row2 = _iota((8, 2 * KV_W), 0)
    q8k_f = jnp.where(lane2 // HEAD_DIM == row2 // R_C, qr[:, :2 * KV_W], 0.0) * SCALE
    q8k = q8k_f.astype(BF16)
    b0 = b0_ref[...]

    lane1 = _iota((8, LANES), 1)
    o_groups = []
    for g in range(KV_C):
        ids = [idx_ref[(b * KV_C + g) * n_top + k] for k in range(n_top)]
        kst = jnp.concatenate([pages[g * n_top + k][0, 0] for k in range(n_top)], axis=1).astype(BF16)
        s = _dot(q8k, kst)
        bias_parts, valid_parts = [], []
        has_new = jnp.int32(0)
        for k in range(n_top):
            blk = ids[k]
            in_cache = jnp.clip(blk, 0, new_blk - 1)
            bias_parts.append(tsel_ref[in_cache // per_page])
            ok = jnp.logical_and(blk >= 0, blk < new_blk).astype(F32)
            valid_parts.append(jnp.where(lane1 // SLC_BLOCK == in_cache % per_page, ok, 0.0))
            has_new = has_new | (blk == new_blk).astype(jnp.int32)
        s = s + jnp.concatenate(bias_parts, axis=1)
        valid = jnp.concatenate(valid_parts, axis=1) > 0.5
        new_row = nslc_ref[0]
        s_new = jnp.sum(q8k_f * new_row, axis=-1, keepdims=True) + b0
        p, pn = _softmax_with_new(s, valid, s_new, has_new > 0)
        o_groups.append(_dot_nt(p.astype(BF16), kst) + pn * new_row)
    f_slc = _to_f_form(jnp.where(row2 < R_C, o_groups[0], o_groups[1]))

    win_t = win_ref[0, 0]
    new_w = nwin_ref[0]
    s = _dot(q8k, win_t.astype(BF16)) + twin_ref[...]
    s_new = jnp.sum(q8k_f * new_w, axis=-1, keepdims=True) + b0
    p, pn = _softmax_with_new(s, s == s, s_new, True)
    f_win = _to_f_form(_dot_nt(p.astype(BF16), win_t.astype(BF16)) + pn * new_w)
    sq = (2 * KV_W, 2 * KV_W)
    new_col = jnp.sum(jnp.where(_iota(sq, 0) == _iota(sq, 1), jnp.broadcast_to(new_w, sq), 0.0),
                      axis=-1, keepdims=True)
    nw_ref[0] = jnp.where(_iota((2 * KV_W, w_len), 1) == w_len - 1, new_col,
                          pltpu.roll(win_t, w_len - 1, 1))

    sig = jnp.broadcast_to(_sigmoid(gate_ref[0]), (8, LANES))
    row1 = _iota((8, LANES), 0)
    total = jnp.zeros((8, D_C), F32)
    for k, f in enumerate((fcmp_ref[0], f_slc, f_win)):
        gk = jnp.sum(jnp.where(lane1 == k * H_C + row1, sig, 0.0), axis=-1, keepdims=True)
        total += gk * f
    o_ref[0] = jnp.sum(total, axis=0, keepdims=True)


def _nsa_attn_decode(qc, gate, new_slc, new_win, fcmp, tsel, twin, b0, cache_slc_t, cache_win_t,
                     page_table, sel_idx, layer, n_top):
    db, n_pages = page_table.shape
    new_blk = n_pages * PAGE // SLC_BLOCK
    w_len = cache_win_t.shape[3]
    per_page = PAGE // SLC_BLOCK
    const = lambda shape: pl.BlockSpec(shape, lambda b, pt, si: (0,) * len(shape))
    per_b = lambda shape: pl.BlockSpec(shape, lambda b, pt, si: (b,) + (0,) * (len(shape) - 1))

    def page_spec(g, k):
        def imap(b, pt, si):
            blk = jnp.clip(si[(b * KV_C + g) * n_top + k], 0, new_blk - 1)
            return (layer, pt[b, blk // per_page], 0, 0)
        return pl.BlockSpec((1, 1, 2 * KV_W, PAGE), imap)

    grid_spec = pltpu.PrefetchScalarGridSpec(
        num_scalar_prefetch=2,
        grid=(db,),
        in_specs=[per_b((1, 1, D_C)), per_b((1, 1, LANES)), per_b((1, 1, 2 * KV_W)),
                  per_b((1, 1, 2 * KV_W)), per_b((1, 8, D_C)),
                  const(tsel.shape), const(twin.shape), const((8, 1)),
                  pl.BlockSpec((1, 1, 2 * KV_W, w_len), lambda b, pt, si: (layer, b, 0, 0))]
                 + [page_spec(g, k) for g in range(KV_C) for k in range(n_top)],
        out_specs=(per_b((1, 1, D_C)), per_b((1, 2 * KV_W, w_len))),
    )
    return pl.pallas_call(
        functools.partial(_nsa_attn_decode_kernel, n_top=n_top, new_blk=new_blk),
        out_shape=(jax.ShapeDtypeStruct((db, 1, D_C), F32),
                   jax.ShapeDtypeStruct((db, 2 * KV_W, w_len), F32)),
        grid_spec=grid_spec,
        compiler_params=_cparams(("arbitrary",)),
        name="nsa_attn_decode",
    )(page_table, sel_idx, qc, gate, new_slc, new_win, fcmp, tsel, twin, b0, cache_win_t,
      *([cache_slc_t] * (KV_C * n_top)))


def _block_diag(mats):
    n = len(mats)
    rows = []
    for a, m in enumerate(mats):
        rows.append(jnp.concatenate([m if a == c else jnp.zeros_like(m) for c in range(n)], axis=-1))
    return jnp.concatenate(rows, axis=-2)


def _nsa_weights(cmp_pe, cmp_w1, cmp_w2):
    pe2 = jnp.concatenate([cmp_pe, cmp_pe], axis=-1)
    w1bd = jnp.stack([_block_diag([cmp_w1[kv]] * KV_C) for kv in range(2)]).astype(BF16)
    zero = jnp.zeros((HEAD_DIM, 4 * HEAD_DIM), F32)
    w2rep = []
    for kv in range(2):
        rep = jnp.concatenate([cmp_w2[kv]] * 4, axis=-1)
        w2rep.append(jnp.stack([jnp.concatenate([rep, zero], axis=0),
                                jnp.concatenate([zero, rep], axis=0)]))
    w2rep = jnp.stack(w2rep).astype(BF16)
    w2heads = jnp.concatenate([w2rep[:, 0], w2rep[:, 1]], axis=-1)
    return pe2, w1bd, w2rep, w2heads


def kernel(x_prompt, x_sample, cache_sb_kv, cache_cmp_kv, cache_slc_kv, cache_win_kv, page_table,
           rel_bias, norm_mix_pre, norm_mix_post, w_in, cmp_pe, cmp_w1, cmp_w2, gmlp_ws, gmlp_b,
           norm_group_out, w_out, norm_ffn_pre, norm_ffn_post, w_ffn_up, w_ffn_down):
    depth = w_in.shape[0]
    bsz, seq, _ = x_prompt.shape
    db = x_sample.shape[0]
    n_pages = page_table.shape[1]
    past = n_pages * PAGE
    n_phys = cache_sb_kv.shape[1]
    w_len = cache_win_kv.shape[2]
    assert x_sample.shape[1] == 1 and seq % 512 == 0 and seq // SLC_BLOCK <= SLC_BLOCK
    assert n_pages % PAGES_PER_STEP == 0 and w_len == WINDOW and past >= WINDOW

    t_prompt = bsz * seq
    tm_p = 512
    n_sb_dec = -(-(past + 1) // SLC_BLOCK)
    n_pad_dec = -(-n_sb_dec // LANES) * LANES
    n_top_dec = min(TOP_N, n_sb_dec)
    n_chunks_dec = past // CMP_STRIDE

    n_q = seq // TQ_ATT
    n_diag = min(n_q, REL_MAX_DIST // TK_ATT + 2)
    tz = _bias_table(rel_bias, n_diag, TK_ATT, TQ_ATT, TK_ATT, -1, 1, 0)
    tz = tz.reshape(KV_C, R_C, n_diag, TK_ATT, TQ_ATT).transpose(2, 0, 1, 3, 4)
    bias_cmp_p = _bias_table(rel_bias, n_q, TQ_ATT, seq // CMP_STRIDE, TQ_ATT, 1, -CMP_STRIDE,
                             -(CMP_LEN - 1))
    bias_cmp_d = _bias_table(rel_bias, 1, 8, n_chunks_dec, 0, 0, -CMP_STRIDE,
                             past - (CMP_LEN - 1))[:, 0, 0, :]
    n_tsel = -(-n_pages // 8) * 8
    tsel = _bias_table(rel_bias, 1, n_tsel, PAGE, 0, -PAGE, -1, past)[:, 0]
    tsel = jnp.swapaxes(tsel, 0, 1)
    twin = _bias_table(rel_bias, 1, 8, w_len, 0, 0, -1, w_len)[:, 0, 0, :]
    b0 = rel_bias[0].reshape(H_C, 1)

    pos_minor = lambda c: c.transpose(0, 1, 3, 4, 5, 2).reshape(c.shape[0], c.shape[1], -1, c.shape[2])
    cache_sb_t = pos_minor(cache_sb_kv)
    cache_cmp_t = pos_minor(cache_cmp_kv)
    cache_slc_t = pos_minor(cache_slc_kv)
    cache_win_t = pos_minor(cache_win_kv)
    from_pos_minor = lambda a, heads: jnp.moveaxis(
        a.reshape(a.shape[:-2] + (2, heads, HEAD_DIM, a.shape[-1])), -1, -4)

    xp = x_prompt.reshape(t_prompt, D_MODEL)
    xs = x_sample.reshape(db, D_MODEL)
    outs = {k: [] for k in ("p_sb", "p_cmp", "p_slc", "p_win", "s_sb", "s_cmp", "s_slc", "s_win", "s_gv")}
    row = lambda v: v.reshape(1, -1)

    for l in range(depth):
        w_in_l = jnp.pad(w_in[l], ((0, 0), (0, NP_IN - N_IN))).astype(BF16)
        w_out_l = w_out[l].astype(BF16)
        w_up_l = w_ffn_up[l].astype(BF16)
        w_down_l = w_ffn_down[l].astype(BF16)
        pe2, w1bd, w2rep, w2heads = _nsa_weights(cmp_pe[l], cmp_w1[l], cmp_w2[l])
        g_pre, g_post = row(norm_mix_pre[l]), row(norm_mix_post[l])
        g_grp = row(norm_group_out[l])
        gf_pre, gf_post = row(norm_ffn_pre[l]), row(norm_ffn_post[l])

        z = _project(xp, g_pre, w_in_l, tm_p)
        z3 = z.reshape(bsz, seq, NP_IN)
        o_a = _sb_prompt(z3)
        bias_rows = jnp.repeat(gmlp_b[l].T, CB, axis=1)
        o_b = _gmlp_prompt(z3, gmlp_ws[l], bias_rows)
        kvc = _compress_prompt(z3, pe2, w1bd, w2rep)
        o_cmp, sel_t = _nsa_cmp_prompt(z3, kvc, bias_cmp_p)
        slc_t = jnp.swapaxes(z3[:, :, C_SK:C_SK + 2 * KV_W], 1, 2)
        win_t = jnp.swapaxes(z3[:, :, C_WK:C_WK + 2 * KV_W], 1, 2)
        o_c = _nsa_attn_prompt(z3, slc_t, win_t, sel_t, tz, o_cmp)
        xp = _mix_out(o_a.reshape(t_prompt, D_A), o_b.reshape(t_prompt, D_B),
                      o_c.reshape(t_prompt, D_C), xp, g_grp, w_out_l, g_post, tm_p)
        xp = _ffn(xp, gf_pre, w_up_l, w_down_l, gf_post, tm_p, D_FF // 2)
        outs["p_sb"].append(z3[:, :, C_KA:C_KA + 2 * D_A].reshape(bsz, seq, 2, H_A, HEAD_DIM))
        kv_shape = (bsz, seq, 2, KV_C, HEAD_DIM)
        outs["p_cmp"].append(z3[:, :, C_CK:C_CK + 2 * KV_W].reshape(kv_shape))
        outs["p_slc"].append(from_pos_minor(slc_t, KV_C))
        n_win = min(WINDOW, seq)
        outs["p_win"].append(from_pos_minor(win_t[:, :, seq - n_win:], KV_C))

        zs = _project(xs, g_pre, w_in_l, db)
        zs3 = zs.reshape(db, 1, NP_IN)
        o_a_s = _sb_decode(zs3[:, :, C_QA:C_QA + D_A], cache_sb_t, page_table, l)
        w_row = jnp.repeat(gmlp_ws[l][:, 0, 0], CB).reshape(1, D_B)
        b_row = jnp.repeat(gmlp_b[l][:, 0], CB).reshape(1, D_B)
        o_b_s, vn_s = _gmlp_sample(zs[:, C_UB:C_UB + D_B], zs[:, C_VB:C_VB + D_B], w_row, b_row)
        qc_s = zs3[:, :, C_QC:C_QC + D_C]
        f_cmp, imp = _nsa_cmp_decode(qc_s, cache_cmp_t, page_table, l, pe2, w1bd, w2heads,
                                     bias_cmp_d, n_pad_dec)
        idx = _topk_decode(imp[:, :KV_C, :].reshape(db * KV_C, n_pad_dec), past // SLC_BLOCK, n_sb_dec)
        sel_idx = idx[:, :n_top_dec].reshape(-1)
        gate_s = zs3[:, :, C_GATE:C_GATE + LANES]
        new_slc = zs3[:, :, C_SK:C_SK + 2 * KV_W]
        new_win = zs3[:, :, C_WK:C_WK + 2 * KV_W]
        o_c_s, win_out = _nsa_attn_decode(qc_s, gate_s, new_slc, new_win, f_cmp, tsel, twin, b0,
                                          cache_slc_t, cache_win_t, page_table, sel_idx, l, n_top_dec)
        xs = _mix_out(o_a_s.reshape(db, D_A), o_b_s, o_c_s.reshape(db, D_C), xs, g_grp, w_out_l,
                      g_post, db)
        xs = _ffn(xs, gf_pre, w_up_l, w_down_l, gf_post, db, D_FF // 2)
        outs["s_sb"].append(zs[:, C_KA:C_KA + 2 * D_A].reshape(db, 1, 2, H_A, HEAD_DIM))
        outs["s_cmp"].append(zs[:, C_CK:C_CK + 2 * KV_W].reshape(db, 1, 2, KV_C, HEAD_DIM))
        outs["s_slc"].append(zs[:, C_SK:C_SK + 2 * KV_W].reshape(db, 1, 2, KV_C, HEAD_DIM))
        outs["s_win"].append(from_pos_minor(win_out, KV_C))
        outs["s_gv"].append(vn_s.reshape(db, 1, D_B))

    st = lambda k: jnp.stack(outs[k])
    return (xp.reshape(bsz, seq, D_MODEL), xs.reshape(db, 1, D_MODEL), st("p_sb"), st("p_cmp"),
            st("p_slc"), st("p_win"), st("s_sb"), st("s_cmp"), st("s_slc"), st("s_win"), st("s_gv"))
```

```python
import functools
import math

import jax
import jax.numpy as jnp
import numpy as np
from jax import lax
from jax.experimental import pallas as pl
from jax.experimental.pallas import tpu as pltpu

F32 = jnp.float32
BF16 = jnp.bfloat16

D_MODEL = 1024
HEAD_DIM = 64
D_A = 256
H_A = 4
D_B = 256
G_B = 4
CB = 64
D_C = 512
H_C = 8
KV_C = 2
R_C = 4
KV_W = 128
N_IN = 3 * D_A + 2 * D_B + D_C + 6 * KV_W + 3 * H_C
NP_IN = 2688
CHUNK = 128
CMP_LEN = 32
CMP_STRIDE = 16
SLC_BLOCK = 64
TOP_N = 16
N_LOCAL = 2
WINDOW = 512
NUM_BUCKETS = 32
REL_MAX_DIST = 2048
D_FF = 2816
EPS = 1e-6
SCALE = HEAD_DIM ** -0.5
PAGE = 128

C_QA, C_KA, C_VA, C_UB, C_VB, C_QC = 0, 256, 512, 768, 1024, 1280
C_CK, C_CV, C_SK, C_SV, C_WK, C_WV, C_GATE = 1792, 1920, 2048, 2176, 2304, 2432, 2560

LANES = 128
TQ_ATT = 128
TK_ATT = 128
TQ_NSA = 256
TK_NSA = 256
VMEM_LIMIT = 56 * 1024 * 1024
NEG = -1e30
DECAYED = -110.0
BIG_SCORE = 3e38
PAGES_PER_STEP = 8


def _cparams(sem):
    return pltpu.CompilerParams(dimension_semantics=sem, vmem_limit_bytes=VMEM_LIMIT)


def _dot(a, b):
    return jnp.dot(a, b, preferred_element_type=F32)


def _dot_nt(a, b):
    return lax.dot_general(a, b, (((1,), (1,)), ((), ())), preferred_element_type=F32)


def _rms(x, g):
    return x * lax.rsqrt(jnp.mean(x * x, axis=-1, keepdims=True) + EPS) * g


def _sigmoid(x):
    return 1.0 / (1.0 + jnp.exp(-x))


def _gelu_tanh(x):
    return 0.5 * x * (1.0 + jnp.tanh(math.sqrt(2.0 / math.pi) * (x + 0.044715 * (x * x * x))))


def _iota(shape, dim):
    return lax.broadcasted_iota(jnp.int32, shape, dim)


def _bucket_thresholds():
    n = np.arange(0, 1 << 15)
    exact = NUM_BUCKETS // 2
    nf = np.maximum(n, 1).astype(np.float32)
    big = exact + (np.log(nf / np.float32(exact)) / np.float32(math.log(REL_MAX_DIST / exact))
                   * np.float32(NUM_BUCKETS - exact)).astype(np.int32)
    bucket = np.where(n < exact, n, np.minimum(big, NUM_BUCKETS - 1))
    assert np.all(np.diff(bucket) >= 0)
    return [int(np.argmax(bucket >= k)) for k in range(1, NUM_BUCKETS)]


_THR = _bucket_thresholds()


def _bias_table_kernel(tab_ref, o_ref, *, a, rs, cs, c0):
    shape = o_ref.shape[2:]
    dist = a * pl.program_id(0) + rs * _iota(shape, 0) + cs * _iota(shape, 1) + c0
    outs = [jnp.full(shape, tab_ref[h], F32) for h in range(H_C)]
    for k in range(1, NUM_BUCKETS):
        ge = dist >= _THR[k - 1]
        for h in range(H_C):
            outs[h] = jnp.where(ge, tab_ref[k * H_C + h], outs[h])
    for h in range(H_C):
        o_ref[h, 0] = outs[h]


def _bias_table(rel_bias, steps, rows, cols, a, rs, cs, c0):
    return pl.pallas_call(
        functools.partial(_bias_table_kernel, a=a, rs=rs, cs=cs, c0=c0),
        out_shape=jax.ShapeDtypeStruct((H_C, steps, rows, cols), F32),
        grid=(steps,),
        in_specs=[pl.BlockSpec(memory_space=pltpu.SMEM)],
        out_specs=pl.BlockSpec((H_C, 1, rows, cols), lambda i: (0, i, 0, 0)),
        compiler_params=_cparams(("arbitrary",)),
        name="bias_table",
    )(rel_bias.reshape(-1))


def _proj_kernel(x_ref, g_ref, w_ref, o_ref):
    h = _rms(x_ref[...], g_ref[...])
    o_ref[...] = _dot(h.astype(BF16), w_ref[...])


def _project(x, g, w, tm):
    t = x.shape[0]
    return pl.pallas_call(
        _proj_kernel,
        out_shape=jax.ShapeDtypeStruct((t, NP_IN), F32),
        grid=(t // tm,),
        in_specs=[pl.BlockSpec((tm, D_MODEL), lambda i: (i, 0)),
                  pl.BlockSpec((1, D_MODEL), lambda i: (0, 0)),
                  pl.BlockSpec((D_MODEL, NP_IN), lambda i: (0, 0))],
        out_specs=pl.BlockSpec((tm, NP_IN), lambda i: (i, 0)),
        compiler_params=_cparams(("parallel",)),
        name="in_proj",
    )(x, g, w)


def _mixout_kernel(oa_ref, ob_ref, oc_ref, x_ref, gg_ref, w_ref, gp_ref, o_ref):
    gg = gg_ref[...]
    a = _rms(oa_ref[...], gg[:, :D_A]).astype(BF16)
    b = _rms(ob_ref[...], gg[:, D_A:D_A + D_B]).astype(BF16)
    c = _rms(oc_ref[...], gg[:, D_A + D_B:]).astype(BF16)
    y = (_dot(a, w_ref[0:D_A, :]) + _dot(b, w_ref[D_A:D_A + D_B, :])
         + _dot(c, w_ref[D_A + D_B:, :]))
    o_ref[...] = x_ref[...] + _rms(y, gp_ref[...])


def _mix_out(oa, ob, oc, x, gg, w, gp, tm):
    t = x.shape[0]
    row = lambda width: pl.BlockSpec((tm, width), lambda i: (i, 0))
    full = lambda r, c: pl.BlockSpec((r, c), lambda i: (0, 0))
    return pl.pallas_call(
        _mixout_kernel,
        out_shape=jax.ShapeDtypeStruct((t, D_MODEL), F32),
        grid=(t // tm,),
        in_specs=[row(D_A), row(D_B), row(D_C), row(D_MODEL), full(1, D_MODEL),
                  full(D_MODEL, D_MODEL), full(1, D_MODEL)],
        out_specs=row(D_MODEL),
        compiler_params=_cparams(("parallel",)),
        name="mix_out",
    )(oa, ob, oc, x, gg, w, gp)


def _ffn_kernel(x_ref, gpre_ref, wg_ref, wu_ref, wd_ref, gpost_ref, o_ref, h_ref, acc_ref):
    j = pl.program_id(1)

    @pl.when(j == 0)
    def _():
        h_ref[...] = _rms(x_ref[...], gpre_ref[...]).astype(BF16)
        acc_ref[...] = jnp.zeros_like(acc_ref)

    h = h_ref[...]
    g = _dot(h, wg_ref[...])
    u = _dot(h, wu_ref[...])
    act = (g * _sigmoid(g)) * u
    acc_ref[...] += _dot(act.astype(BF16), wd_ref[...])

    @pl.when(j == pl.num_programs(1) - 1)
    def _():
        o_ref[...] = x_ref[...] + _rms(acc_ref[...], gpost_ref[...])


def _ffn(x, gpre, w_up, w_down, gpost, tm, tf):
    t = x.shape[0]
    nf = D_FF // tf
    return pl.pallas_call(
        _ffn_kernel,
        out_shape=jax.ShapeDtypeStruct((t, D_MODEL), F32),
        grid=(t // tm, nf),
        in_specs=[pl.BlockSpec((tm, D_MODEL), lambda i, j: (i, 0)),
                  pl.BlockSpec((1, D_MODEL), lambda i, j: (0, 0)),
                  pl.BlockSpec((D_MODEL, tf), lambda i, j: (0, j)),
                  pl.BlockSpec((D_MODEL, tf), lambda i, j: (0, j + nf)),
                  pl.BlockSpec((tf, D_MODEL), lambda i, j: (j, 0)),
                  pl.BlockSpec((1, D_MODEL), lambda i, j: (0, 0))],
        out_specs=pl.BlockSpec((tm, D_MODEL), lambda i, j: (i, 0)),
        scratch_shapes=[pltpu.VMEM((tm, D_MODEL), BF16), pltpu.VMEM((tm, D_MODEL), F32)],
        compiler_params=_cparams(("parallel", "arbitrary")),
        name="ffn",
    )(x, gpre, w_up, w_up, w_down, gpost)


def _log_sig_pair(z):
    soft = jnp.log1p(jnp.exp(-jnp.abs(z)))
    return jnp.minimum(z, 0.0) - soft, jnp.minimum(-z, 0.0) - soft


def _times_01_matrix(x, mat_bf16):
    hi = x.astype(BF16)
    rem = x - hi.astype(F32)
    mid = rem.astype(BF16)
    lo = (rem - mid.astype(F32)).astype(BF16)
    return _dot(hi, mat_bf16) + _dot(mid, mat_bf16) + _dot(lo, mat_bf16)


def _suffix_sum_exclusive(x, upper_bf16):
    return _times_01_matrix(x, upper_bf16)


def _strict_upper(n):
    return jnp.where(_iota((n, n), 0) > _iota((n, n), 1), 1.0, 0.0).astype(BF16)


def _sb_prompt_kernel(q_ref, k_ref, v_ref, o_ref, run_ref, acc_ref, *, tq, tk):
    i = pl.program_id(1)
    lane_head = _iota((tq, D_A), 1) // HEAD_DIM
    qs = q_ref[0] * SCALE
    qh = [jnp.where(lane_head == h, qs, 0.0).astype(BF16) for h in range(H_A)]
    upper = _strict_upper(tk)
    run_ref[...] = jnp.zeros_like(run_ref)
    acc_ref[...] = jnp.zeros_like(acc_ref)

    def sweep_tile(j, diagonal):
        ks = pl.multiple_of(j * tk, tk)
        kt = k_ref[0, pl.ds(ks, tk), :].astype(BF16)
        vt = v_ref[0, pl.ds(ks, tk), :].astype(BF16)
        mask = _iota((tq, tk), 1) < _iota((tq, tk), 0)
        worst = jnp.float32(-jnp.inf)
        for h in range(H_A):
            z = _dot_nt(qh[h], kt)
            lb, lk = _log_sig_pair(z)
            if diagonal:
                lk = jnp.where(mask, lk, 0.0)
            after = _suffix_sum_exclusive(lk, upper) + run_ref[h]
            a = jnp.exp(lb + after)
            if diagonal:
                a = jnp.where(mask, a, 0.0)
            acc_ref[h] += _dot(a.astype(BF16), vt)
            new_run = after[:, 0:1] + lk[:, 0:1]
            run_ref[h] = new_run
            worst = jnp.maximum(worst, jnp.max(new_run))
        return (worst > DECAYED).astype(jnp.int32)

    def cond(c):
        j, alive = c
        return jnp.logical_and(j >= 0, alive > 0)

    def body(c):
        return c[0] - 1, sweep_tile(c[0], False)

    assert tq == tk
    lax.while_loop(cond, body, (i - 1, sweep_tile(i, True)))
    out = acc_ref[0]
    for h in range(1, H_A):
        out = jnp.where(lane_head == h, acc_ref[h], out)
    o_ref[0] = out


def _sb_prompt(z3, tq=256, tk=256):
    b, s, _ = z3.shape
    return pl.pallas_call(
        functools.partial(_sb_prompt_kernel, tq=tq, tk=tk),
        out_shape=jax.ShapeDtypeStruct((b, s, D_A), F32),
        grid=(b, s // tq),
        in_specs=[pl.BlockSpec((1, tq, D_A), lambda bi, i: (bi, i, C_QA // D_A)),
                  pl.BlockSpec((1, s, D_A), lambda bi, i: (bi, 0, C_KA // D_A)),
                  pl.BlockSpec((1, s, D_A), lambda bi, i: (bi, 0, C_VA // D_A))],
        out_specs=pl.BlockSpec((1, tq, D_A), lambda bi, i: (bi, i, 0)),
        scratch_shapes=[pltpu.VMEM((H_A, tq, 1), F32), pltpu.VMEM((H_A, tq, D_A), F32)],
        compiler_params=_cparams(("parallel", "arbitrary")),
        name="sb_prompt",
    )(z3, z3, z3)


def _head_rows(row_vec, width):
    full = jnp.broadcast_to(row_vec, (8, width))
    return jnp.where(_iota((8, width), 1) // HEAD_DIM == _iota((8, width), 0), full, 0.0)


def _sb_decode_kernel(pt_ref, alive_ref, q_ref, run_in_ref, acc_in_ref, *rest, n_pages_step, n_steps):
    del pt_ref
    pages = rest[:n_pages_step]
    run_ref, acc_ref, o_ref, live_ref = rest[n_pages_step:]
    b = pl.program_id(0)
    step = pl.program_id(1)

    @pl.when(step == 0)
    def _():
        run_ref[...] = run_in_ref[...]
        acc_ref[...] = acc_in_ref[...]
        live_ref[0] = alive_ref[b]

    @pl.when(live_ref[0] > 0)
    def _():
        q8 = (_head_rows(q_ref[0], D_A) * SCALE).astype(BF16)
        upper = _strict_upper(PAGE)
        run = run_ref[0][:, 0:1]
        acc = acc_ref[0]
        for page in pages:
            kt = page[0, 0, 0:D_A, :].astype(BF16)
            vt = page[0, 0, D_A:2 * D_A, :].astype(BF16)
            z = _dot(q8, kt)
            lb, lk = _log_sig_pair(z)
            after = _suffix_sum_exclusive(lk, upper) + run
            a = jnp.exp(lb + after)
            acc = acc + _dot_nt(a.astype(BF16), vt)
            run = after[:, 0:1] + lk[:, 0:1]
        run_ref[0] = jnp.broadcast_to(run, (8, LANES))
        acc_ref[0] = acc
        head_rows = _iota((8, 1), 0) < H_A
        live_ref[0] = (jnp.max(jnp.where(head_rows, run, -jnp.inf)) > DECAYED).astype(jnp.int32)

    @pl.when(step == n_steps - 1)
    def _():
        o_ref[0] = jnp.sum(jnp.where(_own_lanes(D_A), acc_ref[0], 0.0), axis=0, keepdims=True)


def _sb_decode_phase(qa, cache_t, page_table, layer, alive, run, acc, first_back, n_pages_step,
                     n_steps):
    db, n_pages = page_table.shape

    def page_spec(k):
        def imap(b, s, pt, al):
            page = pt[b, n_pages - 1 - first_back - (s * n_pages_step + k)]
            return (layer, jnp.where(al[b] > 0, page, 0), 0, 0)
        return pl.BlockSpec((1, 1, 2 * D_A, PAGE), imap)

    per_b = lambda shape: pl.BlockSpec(shape, lambda b, s, pt, al: (b,) + (0,) * (len(shape) - 1))
    grid_spec = pltpu.PrefetchScalarGridSpec(
        num_scalar_prefetch=2,
        grid=(db, n_steps),
        in_specs=[per_b((1, 1, D_A)), per_b((1, 8, LANES)), per_b((1, 8, D_A))]
                 + [page_spec(k) for k in range(n_pages_step)],
        out_specs=(per_b((1, 8, LANES)), per_b((1, 8, D_A)), per_b((1, 1, D_A))),
        scratch_shapes=[pltpu.SMEM((1,), jnp.int32)],
    )
    return pl.pallas_call(
        functools.partial(_sb_decode_kernel, n_pages_step=n_pages_step, n_steps=n_steps),
        out_shape=(jax.ShapeDtypeStruct((db, 8, LANES), F32), jax.ShapeDtypeStruct((db, 8, D_A), F32),
                   jax.ShapeDtypeStruct((db, 1, D_A), F32)),
        grid_spec=grid_spec,
        compiler_params=_cparams(("arbitrary", "arbitrary")),
        name="sb_decode",
    )(page_table, alive, qa, run, acc, *([cache_t] * n_pages_step))


SB_FIRST_PAGES = 4


def _sb_decode(qa, cache_t, page_table, layer):
    db, n_pages = page_table.shape
    first = min(SB_FIRST_PAGES, n_pages)
    ones = jnp.ones((db,), jnp.int32)
    run0 = jnp.zeros((db, 8, LANES), F32)
    acc0 = jnp.zeros((db, 8, D_A), F32)
    run1, acc1, o1 = _sb_decode_phase(qa, cache_t, page_table, layer, ones, run0, acc0, 0, first, 1)
    rest = n_pages - first
    if rest == 0:
        return o1
    per_step = max(d for d in range(1, 13) if rest % d == 0)
    alive = (jnp.max(run1[:, :H_A, 0], axis=1) > DECAYED).astype(jnp.int32)
    return lax.cond(
        jnp.any(alive > 0),
        lambda: _sb_decode_phase(qa, cache_t, page_table, layer, alive, run1, acc1, first, per_step,
                                 rest // per_step)[2],
        lambda: o1)


def _layernorm(x):
    xc = x - jnp.mean(x, axis=-1, keepdims=True)
    return xc * lax.rsqrt(jnp.mean(xc * xc, axis=-1, keepdims=True) + EPS)


def _gmlp_prompt_kernel(u_ref, v_ref, w_ref, b_ref, o_ref, *, n_chunks):
    tril = _iota((CHUNK, CHUNK), 0) >= _iota((CHUNK, CHUNK), 1)
    ws = [jnp.where(tril, w_ref[g], 0.0).astype(BF16) for g in range(G_B)]
    group = _iota((CHUNK, D_B), 1) // CB
    for c in range(n_chunks):
        rows = slice(c * CHUNK, (c + 1) * CHUNK)
        vn = _layernorm(v_ref[0, rows, :]).astype(BF16)
        mixed = b_ref[...]
        for g in range(G_B):
            mixed = mixed + jnp.where(group == g, _dot(ws[g], vn), 0.0)
        o_ref[0, rows, :] = u_ref[0, rows, :] * mixed


def _gmlp_prompt(z3, ws, bias_rows, n_chunks=4):
    b, s, _ = z3.shape
    tr = n_chunks * CHUNK
    return pl.pallas_call(
        functools.partial(_gmlp_prompt_kernel, n_chunks=n_chunks),
        out_shape=jax.ShapeDtypeStruct((b, s, D_B), F32),
        grid=(b, s // tr),
        in_specs=[pl.BlockSpec((1, tr, D_B), lambda bi, i: (bi, i, C_UB // D_B)),
                  pl.BlockSpec((1, tr, D_B), lambda bi, i: (bi, i, C_VB // D_B)),
                  pl.BlockSpec((G_B, CHUNK, CHUNK), lambda bi, i: (0, 0, 0)),
                  pl.BlockSpec((CHUNK, D_B), lambda bi, i: (0, 0))],
        out_specs=pl.BlockSpec((1, tr, D_B), lambda bi, i: (bi, i, 0)),
        compiler_params=_cparams(("parallel", "parallel")),
        name="gmlp_prompt",
    )(z3, z3, ws, bias_rows)


def _gmlp_sample_kernel(u_ref, v_ref, w_ref, b_ref, o_ref, vn_ref):
    vn = _layernorm(v_ref[...])
    vn_ref[...] = vn
    o_ref[...] = u_ref[...] * (w_ref[...] * vn + b_ref[...])


def _gmlp_sample(u, v, w_row, b_row):
    return pl.pallas_call(
        _gmlp_sample_kernel,
        out_shape=(jax.ShapeDtypeStruct(u.shape, F32), jax.ShapeDtypeStruct(u.shape, F32)),
        name="gmlp_sample",
    )(u, v, w_row, b_row)


def _compress_hidden(load_rows, pe_ref, w1_fn, n_chunks, width):
    hid_a = jnp.zeros((n_chunks, width), F32)
    hid_b = jnp.zeros((n_chunks, width), F32)
    for j in range(CMP_STRIDE):
        x = load_rows(j)
        hid_a += _dot((x + pe_ref[j:j + 1, :]).astype(BF16), w1_fn(j))
        hid_b += _dot((x + pe_ref[CMP_STRIDE + j:CMP_STRIDE + j + 1, :]).astype(BF16),
                      w1_fn(CMP_STRIDE + j))
    return hid_a + pltpu.roll(hid_b, n_chunks - 1, 0)


def _compress_prompt_kernel(z_ref, pe_ref, w1_ref, w2_ref, o_ref, *, n_chunks):
    hid = _compress_hidden(lambda j: z_ref[0, pl.ds(j, n_chunks, stride=CMP_STRIDE), :],
                           pe_ref.at[0], lambda j: w1_ref[0, j], n_chunks, KV_W)
    act = _gelu_tanh(hid).astype(BF16)
    valid = _iota((n_chunks, 4 * HEAD_DIM), 0) < n_chunks - 1
    for g in range(KV_C):
        o_ref[0, 0, g] = jnp.where(valid, _dot(act, w2_ref[0, g]), 0.0)


def _compress_prompt(z3, pe2, w1bd, w2rep):
    b, s, _ = z3.shape
    n_chunks = s // CMP_STRIDE
    return pl.pallas_call(
        functools.partial(_compress_prompt_kernel, n_chunks=n_chunks),
        out_shape=jax.ShapeDtypeStruct((b, 2, KV_C, n_chunks, 4 * HEAD_DIM), F32),
        grid=(b, 2),
        in_specs=[pl.BlockSpec((1, s, KV_W), lambda bi, kv: (bi, 0, C_CK // KV_W + kv)),
                  pl.BlockSpec((1, CMP_LEN, KV_W), lambda bi, kv: (kv, 0, 0)),
                  pl.BlockSpec((1, CMP_LEN, KV_W, KV_W), lambda bi, kv: (kv, 0, 0, 0)),
                  pl.BlockSpec((1, KV_C, KV_W, 4 * HEAD_DIM), lambda bi, kv: (kv, 0, 0, 0))],
        out_specs=pl.BlockSpec((1, 1, KV_C, n_chunks, 4 * HEAD_DIM),
                               lambda bi, kv: (bi, kv, 0, 0, 0)),
        compiler_params=_cparams(("parallel", "parallel")),
        name="nsa_compress_prompt",
    )(z3, pe2, w1bd, w2rep)


def _select_blocks_t(score, causal, n_top):
    n_sb = score.shape[0]
    jt = _iota(score.shape, 0)
    rank = jnp.zeros(score.shape, F32)
    for jp in range(n_sb):
        row = score[jp:jp + 1, :]
        ge = jnp.where(row >= score, 1.0, 0.0)
        gt = jnp.where(row > score, 1.0, 0.0)
        rank += jnp.where(jt > jp, ge, gt)
    return jnp.where(causal, jnp.where(rank < n_top, 1.0, 0.0), 0.0)


def _nsa_cmp_prompt_kernel(q0_ref, q1_ref, kvc_ref, bias_ref, ocmp_ref, sel_ref, *, n_sb, n_cb):
    i = pl.program_id(1)
    tq = TQ_ATT
    n_pad = kvc_ref.shape[3]
    lane_head = _iota((tq, 4 * HEAD_DIM), 1) // HEAD_DIM
    n_idx = _iota((tq, n_pad), 1)
    dist = (i * tq + _iota((tq, n_pad), 0)) - (n_idx * CMP_STRIDE + (CMP_LEN - 1))
    mask = jnp.logical_and(dist >= 0, n_idx < n_cb)
    c0 = _iota((n_sb, n_pad), 1) * CMP_STRIDE
    s0 = _iota((n_sb, n_pad), 0) * SLC_BLOCK
    ov_t = jnp.where(jnp.logical_and(c0 < s0 + SLC_BLOCK, c0 + CMP_LEN > s0), 1.0, 0.0).astype(BF16)
    t_row = i * tq + _iota((n_sb, tq), 1)
    cur = t_row // SLC_BLOCK
    jt = _iota((n_sb, tq), 0)
    causal = jt <= cur
    forced = jnp.logical_or(jt == 0, jt > cur - N_LOCAL)
    sel_t = []
    for g, q_ref in enumerate((q0_ref, q1_ref)):
        qs = q_ref[0] * SCALE
        qm = jnp.concatenate([jnp.where(lane_head == r, qs, 0.0) for r in range(R_C)],
                             axis=0).astype(BF16)
        kc = kvc_ref[0, 0, g].astype(BF16)
        vc = kvc_ref[0, 1, g].astype(BF16)
        s_all = _dot_nt(qm, kc)
        o_acc = jnp.zeros((tq, 4 * HEAD_DIM), F32)
        imp_t = jnp.zeros((n_sb, tq), F32)
        for r in range(R_C):
            s = s_all[r * tq:(r + 1) * tq] + bias_ref[R_C * g + r, 0]
            sm = jnp.where(mask, s, NEG)
            m = jnp.max(sm, axis=-1, keepdims=True)
            p = jnp.where(mask, jnp.exp(sm - m), 0.0)
            p = p / jnp.maximum(jnp.sum(p, axis=-1, keepdims=True), 1e-30)
            pb = p.astype(BF16)
            o_acc += jnp.where(lane_head == r, _dot(pb, vc), 0.0)
            imp_t += _dot_nt(ov_t, pb)
        ocmp_ref[0, :, g * 4 * HEAD_DIM:(g + 1) * 4 * HEAD_DIM] = o_acc
        score = jnp.where(causal, jnp.where(forced, BIG_SCORE, imp_t), -1.0)
        sel_t.append(_select_blocks_t(score, causal, min(TOP_N, n_sb)))
    pad = jnp.zeros((SLC_BLOCK - n_sb, tq), F32)
    sel_ref[0] = jnp.concatenate([sel_t[0], pad, sel_t[1], pad] if n_sb < SLC_BLOCK else sel_t,
                                 axis=0)


def _nsa_cmp_prompt(z3, kvc, bias_cmp):
    b, s, _ = z3.shape
    n_sb = s // SLC_BLOCK
    n_cb = (s - CMP_LEN) // CMP_STRIDE + 1
    n_pad = kvc.shape[3]
    wq = 4 * HEAD_DIM
    return pl.pallas_call(
        functools.partial(_nsa_cmp_prompt_kernel, n_sb=n_sb, n_cb=n_cb),
        out_shape=(jax.ShapeDtypeStruct((b, s, D_C), F32),
                   jax.ShapeDtypeStruct((b, 2 * SLC_BLOCK, s), F32)),
        grid=(b, s // TQ_ATT),
        in_specs=[pl.BlockSpec((1, TQ_ATT, wq), lambda bi, i: (bi, i, C_QC // wq)),
                  pl.BlockSpec((1, TQ_ATT, wq), lambda bi, i: (bi, i, C_QC // wq + 1)),
                  pl.BlockSpec((1, 2, KV_C, n_pad, wq), lambda bi, i: (bi, 0, 0, 0, 0)),
                  pl.BlockSpec((H_C, 1, TQ_ATT, n_pad), lambda bi, i: (0, i, 0, 0))],
        out_specs=(pl.BlockSpec((1, TQ_ATT, D_C), lambda bi, i: (bi, i, 0)),
                   pl.BlockSpec((1, 2 * SLC_BLOCK, TQ_ATT), lambda bi, i: (bi, 0, i))),
        compiler_params=_cparams(("parallel", "parallel")),
        name="nsa_cmp_prompt",
    )(z3, z3, kvc, bias_cmp)


def _score_tile(args, slot, s_ref, peak_ref, tz_ref, n_diag):
    kt, qt, valid_f, d0 = args
    tk, tb = kt.shape[0], TK_ATT
    nq = valid_f.shape[1] // tb
    tabs = {e: tz_ref[jnp.clip(d0 + e, 0, n_diag - 1), 0] for e in range(-(tk // tb - 1), nq)}
    bias = jnp.concatenate(
        [jnp.concatenate([tabs[cq - ck][r] for r in range(R_C) for cq in range(nq)], axis=1)
         for ck in range(tk // tb)], axis=0)
    valid = jnp.concatenate([valid_f] * R_C, axis=1) > 0.5
    sm = jnp.where(valid, _dot(kt, qt) + bias, NEG)
    s_ref[slot] = sm
    peak_ref[slot] = jnp.broadcast_to(jnp.max(sm, axis=0, keepdims=True), peak_ref.shape[1:])


def _absorb_tile(vt, slot, s_ref, peak_ref, m_ref, l_ref, acc_ref):
    sm = s_ref[slot]
    m_prev = m_ref[...]
    m_new = jnp.maximum(m_prev, peak_ref[slot][0:1])
    alpha = jnp.exp(m_prev - m_new)
    p = jnp.exp((sm - m_new).astype(BF16))
    pv = _dot(jnp.concatenate([vt, jnp.ones((16, vt.shape[1]), BF16)], axis=0), p)
    l_ref[...] = alpha * l_ref[...] + pv[HEAD_DIM:HEAD_DIM + 1]
    acc_ref[...] = alpha * acc_ref[...] + pv[:HEAD_DIM]
    m_ref[...] = m_new


def _nsa_attn_prompt_kernel(q_ref, sk_ref, svt_ref, wk_ref, wvt_ref, selt_ref, tz_ref, gate_ref,
                            ocmp_ref, o_ref, s_ref, peak_ref, m_ref, l_ref, acc_ref, *, n_diag):
    g = pl.program_id(1)
    i = pl.program_id(2)
    tq, tk, tb = TQ_NSA, TK_NSA, TK_ATT
    wq = 4 * HEAD_DIM
    lane = _iota((tq, LANES), 1)
    own_half = (lane // HEAD_DIM) == g
    q = q_ref[0]
    cols = []
    for r in range(R_C):
        a = pltpu.roll(q, (wq - r * HEAD_DIM) % wq, 1)[:, :LANES]
        both = jnp.where(g == 0, a, pltpu.roll(a, HEAD_DIM, 1))
        cols.append((jnp.where(own_half, both, 0.0) * SCALE).T)
    qt = jnp.concatenate(cols, axis=1).astype(BF16)
    selt = selt_ref[0].astype(BF16)
    qpos = i * tq + _iota((tk, tq), 1)
    key_in_tile = _iota((tk, tq), 0)
    own_rows = pl.multiple_of(g * HEAD_DIM, HEAD_DIM)

    def reset():
        m_ref[...] = jnp.full(m_ref.shape, NEG, F32)
        l_ref[...] = jnp.zeros_like(l_ref)
        acc_ref[...] = jnp.zeros_like(acc_ref)

    def slc_scores(jt):
        ks = pl.multiple_of(jt * tk, tk)
        kt = sk_ref[0, pl.ds(ks, tk), :].astype(BF16)
        target = (g * SLC_BLOCK + (tk // SLC_BLOCK) * jt + (_iota((tk, LANES), 0) // SLC_BLOCK))
        expand = jnp.where(_iota((tk, LANES), 1) == target, 1.0, 0.0).astype(BF16)
        chosen = _dot(expand, selt)
        valid_f = jnp.where(ks + key_in_tile <= qpos, chosen, 0.0)
        _score_tile((kt, qt, valid_f, (i * tq - ks) // tb), jt % 2, s_ref, peak_ref, tz_ref, n_diag)

    def win_scores(jt):
        ks = pl.multiple_of(jt * tk, tk)
        kt = wk_ref[0, pl.ds(ks, tk), :].astype(BF16)
        dist = qpos - (ks + key_in_tile)
        valid_f = jnp.where(jnp.logical_and(dist >= 0, dist <= WINDOW), 1.0, 0.0)
        _score_tile((kt, qt, valid_f, (i * tq - ks) // tb), jt % 2, s_ref, peak_ref, tz_ref, n_diag)

    def sweep(scores, values_ref, first, last):
        def absorb(jt):
            vt = values_ref[0, jt, pl.ds(own_rows, HEAD_DIM), :].astype(BF16)
            _absorb_tile(vt, jt % 2, s_ref, peak_ref, m_ref, l_ref, acc_ref)

        def body(jt, carry):
            absorb(jt - 1)
            scores(jt)
            return carry

        scores(first)
        lax.fori_loop(first + 1, last + 1, body, 0)
        absorb(last)

    def normalised_out():
        acc_t = acc_ref[...] / l_ref[...]
        out = jnp.zeros((tq, wq), F32)
        pad_rows = jnp.zeros((LANES - HEAD_DIM, tq), F32)
        for r in range(R_C):
            x = jnp.concatenate([acc_t[:, r * tq:(r + 1) * tq], pad_rows], axis=0).T
            x = jnp.concatenate([x, jnp.zeros((tq, wq - LANES), F32)], axis=1)
            out += x if r == 0 else pltpu.roll(x, r * HEAD_DIM, 1)
        return out

    last_tile = ((i + 1) * tq - 1) // tk
    reset()
    sweep(slc_scores, svt_ref, 0, last_tile)
    o_slc = normalised_out()
    reset()
    sweep(win_scores, wvt_ref, jnp.maximum(i * tq - WINDOW, 0) // tk, last_tile)
    o_win = normalised_out()

    sig = _sigmoid(gate_ref[0])
    gate_lane = R_C * g + _iota((LANES, wq), 1) // HEAD_DIM
    out = jnp.zeros((tq, wq), F32)
    for k, branch in enumerate((ocmp_ref[0], o_slc, o_win)):
        place = jnp.where(_iota((LANES, wq), 0) == k * H_C + gate_lane, 1.0, 0.0).astype(BF16)
        out += _times_01_matrix(sig, place) * branch
    o_ref[0] = out


def _value_tiles(kv_t):
    b, _, s = kv_t.shape
    v_t = kv_t[:, KV_W:, :].reshape(b, KV_W, s // TK_NSA, TK_NSA)
    return jnp.swapaxes(v_t, 1, 2)


def _nsa_attn_prompt(z3, slc_t, win_t, sel_t, tz, ocmp):
    b, s, _ = z3.shape
    wq = 4 * HEAD_DIM
    n_diag = tz.shape[0]
    n_kt = s // TK_NSA
    k_spec = lambda col: pl.BlockSpec((1, s, KV_W), lambda bi, g, i: (bi, 0, col // KV_W))
    vt_spec = pl.BlockSpec((1, n_kt, KV_W, TK_NSA), lambda bi, g, i: (bi, 0, 0, 0))
    return pl.pallas_call(
        functools.partial(_nsa_attn_prompt_kernel, n_diag=n_diag),
        out_shape=jax.ShapeDtypeStruct((b, s, D_C), F32),
        grid=(b, KV_C, s // TQ_NSA),
        in_specs=[pl.BlockSpec((1, TQ_NSA, wq), lambda bi, g, i: (bi, i, C_QC // wq + g)),
                  k_spec(C_SK), vt_spec, k_spec(C_WK), vt_spec,
                  pl.BlockSpec((1, 2 * SLC_BLOCK, TQ_NSA), lambda bi, g, i: (bi, 0, i)),
                  pl.BlockSpec((n_diag, 1, R_C, TK_ATT, TQ_ATT), lambda bi, g, i: (0, g, 0, 0, 0)),
                  pl.BlockSpec((1, TQ_NSA, LANES), lambda bi, g, i: (bi, i, C_GATE // LANES)),
                  pl.BlockSpec((1, TQ_NSA, wq), lambda bi, g, i: (bi, i, g))],
        out_specs=pl.BlockSpec((1, TQ_NSA, wq), lambda bi, g, i: (bi, i, g)),
        scratch_shapes=[pltpu.VMEM((2, TK_NSA, R_C * TQ_NSA), F32),
                        pltpu.VMEM((2, 8, R_C * TQ_NSA), F32),
                        pltpu.VMEM((1, R_C * TQ_NSA), F32),
                        pltpu.VMEM((1, R_C * TQ_NSA), F32),
                        pltpu.VMEM((HEAD_DIM, R_C * TQ_NSA), F32)],
        compiler_params=_cparams(("parallel", "parallel", "arbitrary")),
        name="nsa_attn_prompt",
    )(z3, z3, _value_tiles(slc_t), z3, _value_tiles(win_t), sel_t, tz, z3, ocmp)


def _own_lanes(width):
    return _iota((8, width), 1) // HEAD_DIM == _iota((8, width), 0)


def _nsa_cmp_decode_kernel(pt_ref, q_ref, pe_ref, w1_ref, w2_ref, bias_ref, *rest,
                           n_steps, n_sb):
    del pt_ref
    pages = rest[:PAGES_PER_STEP]
    f_ref, imp_ref, buf_ref = rest[PAGES_PER_STEP:]
    step = pl.program_id(1)
    rows_per_page = PAGE // CMP_STRIDE
    out_row = _iota((PAGE, PAGE), 0)
    regroup = jnp.where(_iota((PAGE, PAGE), 1)
                        == CMP_STRIDE * (out_row % rows_per_page) + out_row // rows_per_page,
                        1.0, 0.0).astype(BF16)
    for k, page in enumerate(pages):
        row0 = pl.multiple_of((step * PAGES_PER_STEP + k) * rows_per_page, rows_per_page)
        x = _dot_nt(regroup, page[0, 0].astype(BF16))
        for kv in range(2):
            for j in range(CMP_STRIDE):
                buf_ref[kv, j, pl.ds(row0, rows_per_page), :] = (
                    x[j * rows_per_page:(j + 1) * rows_per_page, kv * KV_W:(kv + 1) * KV_W])

    @pl.when(step == n_steps - 1)
    def _():
        n_chunks = buf_ref.shape[2]
        n_cb = n_chunks - 1
        kv8 = []
        for kv in range(2):
            hid = _compress_hidden(
                lambda j, kv=kv: buf_ref[kv, j],
                pe_ref.at[kv], lambda j, kv=kv: w1_ref[kv, j], n_chunks, KV_W)
            kv8.append(_dot(_gelu_tanh(hid).astype(BF16), w2_ref[kv]).astype(BF16))
        k8, v8 = kv8
        q8 = (_head_rows(q_ref[0], D_C) * SCALE).astype(BF16)
        s = _dot_nt(q8, k8) + bias_ref[...]
        n_idx = _iota((8, n_chunks), 1)
        mask = n_idx < n_cb
        sm = jnp.where(mask, s, NEG)
        m = jnp.max(sm, axis=-1, keepdims=True)
        p = jnp.where(mask, jnp.exp(sm - m), 0.0)
        p = p / jnp.maximum(jnp.sum(p, axis=-1, keepdims=True), 1e-30)
        pb = p.astype(BF16)
        f_ref[0] = jnp.where(_own_lanes(D_C), _dot(pb, v8), 0.0)
        n_pad = imp_ref.shape[2]
        c0 = _iota((n_chunks, n_pad), 0) * CMP_STRIDE
        j_idx = _iota((n_chunks, n_pad), 1)
        s0 = j_idx * SLC_BLOCK
        ov = jnp.logical_and(jnp.logical_and(c0 < s0 + SLC_BLOCK, c0 + CMP_LEN > s0),
                             jnp.logical_and(_iota((n_chunks, n_pad), 0) < n_cb, j_idx < n_sb))
        imp8 = _dot(pb, jnp.where(ov, 1.0, 0.0).astype(BF16))
        row = _iota((8, n_pad), 0)
        g0 = jnp.sum(jnp.where(row < R_C, imp8, 0.0), axis=0, keepdims=True)
        g1 = jnp.sum(jnp.where(row >= R_C, imp8, 0.0), axis=0, keepdims=True)
        imp_ref[0] = jnp.where(row == 0, g0, jnp.where(row == 1, g1, 0.0))


def _nsa_cmp_decode(qc, cache4, page_table, layer, pe2, w1bd, w2heads, bias_cmp, n_pad):
    db, n_pages = page_table.shape
    past = n_pages * PAGE
    n_steps = n_pages // PAGES_PER_STEP
    n_sb = -(-(past + 1) // SLC_BLOCK)
    n_chunks = past // CMP_STRIDE
    const = lambda shape: pl.BlockSpec(shape, lambda b, s, pt: (0,) * len(shape))

    def page_spec(k):
        def imap(b, s, pt):
            return (layer, pt[b, s * PAGES_PER_STEP + k], 0, 0)
        return pl.BlockSpec((1, 1, 2 * KV_W, PAGE), imap)

    grid_spec = pltpu.PrefetchScalarGridSpec(
        num_scalar_prefetch=1,
        grid=(db, n_steps),
        in_specs=[pl.BlockSpec((1, 1, D_C), lambda b, s, pt: (b, 0, 0)),
                  const((2, CMP_LEN, KV_W)), const((2, CMP_LEN, KV_W, KV_W)),
                  const((2, KV_W, D_C)), const((8, n_chunks))]
                 + [page_spec(k) for k in range(PAGES_PER_STEP)],
        out_specs=(pl.BlockSpec((1, 8, D_C), lambda b, s, pt: (b, 0, 0)),
                   pl.BlockSpec((1, 8, n_pad), lambda b, s, pt: (b, 0, 0))),
        scratch_shapes=[pltpu.VMEM((2, CMP_STRIDE, n_chunks, KV_W), F32)],
    )
    return pl.pallas_call(
        functools.partial(_nsa_cmp_decode_kernel, n_steps=n_steps, n_sb=n_sb),
        out_shape=(jax.ShapeDtypeStruct((db, 8, D_C), F32),
                   jax.ShapeDtypeStruct((db, 8, n_pad), F32)),
        grid_spec=grid_spec,
        compiler_params=_cparams(("parallel", "arbitrary")),
        name="nsa_cmp_decode",
    )(page_table, qc, pe2, w1bd, w2heads, bias_cmp, *([cache4] * PAGES_PER_STEP))


def _topk_decode_kernel(imp_ref, idx_ref, *, cur, n_sb, n_top):
    shape = imp_ref.shape
    lane = _iota(shape, 1)
    lane_f = lane.astype(F32)
    causal = jnp.logical_and(lane <= cur, lane < n_sb)
    forced = jnp.logical_or(lane == 0, lane > cur - N_LOCAL)
    score = jnp.where(causal, jnp.where(forced, BIG_SCORE, imp_ref[...]), -1.0)
    out_lane = _iota(idx_ref.shape, 1)
    out = jnp.full(idx_ref.shape, -1, jnp.int32)
    for k in range(n_top):
        m = jnp.max(score, axis=-1, keepdims=True)
        first = jnp.min(jnp.where(score == m, lane_f, 1e9), axis=-1, keepdims=True)
        pick = jnp.where(m > -0.5, first, -1.0).astype(jnp.int32)
        out = jnp.where(out_lane == k, pick, out)
        score = jnp.where(lane_f == first, -2.0, score)
    idx_ref[...] = out


def _topk_decode(imp_rows, cur, n_sb):
    return pl.pallas_call(
        functools.partial(_topk_decode_kernel, cur=cur, n_sb=n_sb, n_top=min(TOP_N, n_sb)),
        out_shape=jax.ShapeDtypeStruct((imp_rows.shape[0], LANES), jnp.int32),
        name="nsa_topk_decode",
    )(imp_rows)


def _softmax_with_new(s, valid, s_new, has_new):
    sm = jnp.where(valid, s, NEG)
    sn = jnp.where(has_new, s_new, NEG)
    m = jnp.maximum(jnp.max(sm, axis=-1, keepdims=True), sn)
    p = jnp.where(valid, jnp.exp(sm - m), 0.0)
    pn = jnp.where(has_new, jnp.exp(sn - m), 0.0)
    inv = 1.0 / jnp.maximum(jnp.sum(p, axis=-1, keepdims=True) + pn, 1e-30)
    return p * inv, pn * inv


def _to_f_form(o8):
    x = jnp.concatenate([o8, jnp.zeros((8, D_C - 2 * KV_W), F32)], axis=1)
    row = _iota((8, D_C), 0)
    out = jnp.zeros((8, D_C), F32)
    for h in range(H_C):
        shift = (h * HEAD_DIM - (KV_W + (h // R_C) * HEAD_DIM)) % D_C
        out = jnp.where(row == h, x if shift == 0 else pltpu.roll(x, shift, 1), out)
    return jnp.where(_own_lanes(D_C), out, 0.0)


def _nsa_attn_decode_kernel(pt_ref, idx_ref, q_ref, gate_ref, nslc_ref, nwin_ref, fcmp_ref, tsel_ref,
                            twin_ref, b0_ref, win_ref, *rest, n_top, new_blk):
    del pt_ref
    pages = rest[:KV_C * n_top]
    o_ref, nw_ref = rest[KV_C * n_top:]
    b = pl.program_id(0)
    w_len = win_ref.shape[3]
    per_page = PAGE // SLC_BLOCK

    qb = jnp.broadcast_to(q_ref[0], (8, D_C))
    row5 = _iota((8, D_C), 0)
    qr = jnp.zeros((8, D_C), F32)
    for h in range(H_C):
        shift = ((h // R_C) * HEAD_DIM - h * HEAD_DIM) % D_C
        qr = jnp.where(row5 == h, qb if shift == 0 else pltpu.roll(qb, shift, 1), qr)
    lane2 = _iota((8, 2 * KV_W), 1)
    row2 = _iota((8, 2 * KV_W), 0)
    q8k_f = jnp.where(lane2 // HEAD_DIM == row2 // R_C, qr[:, :2 * KV_W], 0.0) * SCALE
    q8k = q8k_f.astype(BF16)
    b0 = b0_ref[...]

    lane1 = _iota((8, LANES), 1)
    o_groups = []
    for g in range(KV_C):
        ids = [idx_ref[(b * KV_C + g) * n_top + k] for k in range(n_top)]
        kst = jnp.concatenate([pages[g * n_top + k][0, 0] for k in range(n_top)], axis=1).astype(BF16)
        s = _dot(q8k, kst)
        bias_parts, valid_parts = [], []
        has_new = jnp.int32(0)
        for k in range(n_top):
            blk = ids[k]
            in_cache = jnp.clip(blk, 0, new_blk - 1)
            bias_parts.append(tsel_ref[in_cache // per_page])
            ok = jnp.logical_and(blk >= 0, blk < new_blk).astype(F32)
            valid_parts.append(jnp.where(lane1 // SLC_BLOCK == in_cache % per_page, ok, 0.0))
            has_new = has_new | (blk == new_blk).astype(jnp.int32)
        s = s + jnp.concatenate(bias_parts, axis=1)
        valid = jnp.concatenate(valid_parts, axis=1) > 0.5
        new_row = nslc_ref[0]
        s_new = jnp.sum(q8k_f * new_row, axis=-1, keepdims=True) + b0
        p, pn = _softmax_with_new(s, valid, s_new, has_new > 0)
        o_groups.append(_dot_nt(p.astype(BF16), kst) + pn * new_row)
    f_slc = _to_f_form(jnp.where(row2 < R_C, o_groups[0], o_groups[1]))

    win_t = win_ref[0, 0]
    new_w = nwin_ref[0]
    s = _dot(q8k, win_t.astype(BF16)) + twin_ref[...]
    s_new = jnp.sum(q8k_f * new_w, axis=-1, keepdims=True) + b0
    p, pn = _softmax_with_new(s, s == s, s_new, True)
    f_win = _to_f_form(_dot_nt(p.astype(BF16), win_t.astype(BF16)) + pn * new_w)
    sq = (2 * KV_W, 2 * KV_W)
    new_col = jnp.sum(jnp.where(_iota(sq, 0) == _iota(sq, 1), jnp.broadcast_to(new_w, sq), 0.0),
                      axis=-1, keepdims=True)
    nw_ref[0] = jnp.where(_iota((2 * KV_W, w_len), 1) == w_len - 1, new_col,
                          pltpu.roll(win_t, w_len - 1, 1))

    sig = jnp.broadcast_to(_sigmoid(gate_ref[0]), (8, LANES))
    row1 = _iota((8, LANES), 0)
    total = jnp.zeros((8, D_C), F32)
    for k, f in enumerate((fcmp_ref[0], f_slc, f_win)):
        gk = jnp.sum(jnp.where(lane1 == k * H_C + row1, sig, 0.0), axis=-1, keepdims=True)
        total += gk * f
    o_ref[0] = jnp.sum(total, axis=0, keepdims=True)


def _nsa_attn_decode(qc, gate, new_slc, new_win, fcmp, tsel, twin, b0, cache_slc_t, cache_win_t,
                     page_table, sel_idx, layer, n_top):
    db, n_pages = page_table.shape
    new_blk = n_pages * PAGE // SLC_BLOCK
    w_len = cache_win_t.shape[3]
    per_page = PAGE // SLC_BLOCK
    const = lambda shape: pl.BlockSpec(shape, lambda b, pt, si: (0,) * len(shape))
    per_b = lambda shape: pl.BlockSpec(shape, lambda b, pt, si: (b,) + (0,) * (len(shape) - 1))

    def page_spec(g, k):
        def imap(b, pt, si):
            blk = jnp.clip(si[(b * KV_C + g) * n_top + k], 0, new_blk - 1)
            return (layer, pt[b, blk // per_page], 0, 0)
        return pl.BlockSpec((1, 1, 2 * KV_W, PAGE), imap)

    grid_spec = pltpu.PrefetchScalarGridSpec(
        num_scalar_prefetch=2,
        grid=(db,),
        in_specs=[per_b((1, 1, D_C)), per_b((1, 1, LANES)), per_b((1, 1, 2 * KV_W)),
                  per_b((1, 1, 2 * KV_W)), per_b((1, 8, D_C)),
                  const(tsel.shape), const(twin.shape), const((8, 1)),
                  pl.BlockSpec((1, 1, 2 * KV_W, w_len), lambda b, pt, si: (layer, b, 0, 0))]
                 + [page_spec(g, k) for g in range(KV_C) for k in range(n_top)],
        out_specs=(per_b((1, 1, D_C)), per_b((1, 2 * KV_W, w_len))),
    )
    return pl.pallas_call(
        functools.partial(_nsa_attn_decode_kernel, n_top=n_top, new_blk=new_blk),
        out_shape=(jax.ShapeDtypeStruct((db, 1, D_C), F32),
                   jax.ShapeDtypeStruct((db, 2 * KV_W, w_len), F32)),
        grid_spec=grid_spec,
        compiler_params=_cparams(("arbitrary",)),
        name="nsa_attn_decode",
    )(page_table, sel_idx, qc, gate, new_slc, new_win, fcmp, tsel, twin, b0, cache_win_t,
      *([cache_slc_t] * (KV_C * n_top)))


def _block_diag(mats):
    n = len(mats)
    rows = []
    for a, m in enumerate(mats):
        rows.append(jnp.concatenate([m if a == c else jnp.zeros_like(m) for c in range(n)], axis=-1))
    return jnp.concatenate(rows, axis=-2)


def _nsa_weights(cmp_pe, cmp_w1, cmp_w2):
    pe2 = jnp.concatenate([cmp_pe, cmp_pe], axis=-1)
    w1bd = jnp.stack([_block_diag([cmp_w1[kv]] * KV_C) for kv in range(2)]).astype(BF16)
    zero = jnp.zeros((HEAD_DIM, 4 * HEAD_DIM), F32)
    w2rep = []
    for kv in range(2):
        rep = jnp.concatenate([cmp_w2[kv]] * 4, axis=-1)
        w2rep.append(jnp.stack([jnp.concatenate([rep, zero], axis=0),
                                jnp.concatenate([zero, rep], axis=0)]))
    w2rep = jnp.stack(w2rep).astype(BF16)
    w2heads = jnp.concatenate([w2rep[:, 0], w2rep[:, 1]], axis=-1)
    return pe2, w1bd, w2rep, w2heads


def kernel(x_prompt, x_sample, cache_sb_kv, cache_cmp_kv, cache_slc_kv, cache_win_kv, page_table,
           rel_bias, norm_mix_pre, norm_mix_post, w_in, cmp_pe, cmp_w1, cmp_w2, gmlp_ws, gmlp_b,
           norm_group_out, w_out, norm_ffn_pre, norm_ffn_post, w_ffn_up, w_ffn_down):
    depth = w_in.shape[0]
    bsz, seq, _ = x_prompt.shape
    db = x_sample.shape[0]
    n_pages = page_table.shape[1]
    past = n_pages * PAGE
    n_phys = cache_sb_kv.shape[1]
    w_len = cache_win_kv.shape[2]
    assert x_sample.shape[1] == 1 and seq % 512 == 0 and seq // SLC_BLOCK <= SLC_BLOCK
    assert n_pages % PAGES_PER_STEP == 0 and w_len == WINDOW and past >= WINDOW

    t_prompt = bsz * seq
    tm_p = 512
    n_sb_dec = -(-(past + 1) // SLC_BLOCK)
    n_pad_dec = -(-n_sb_dec // LANES) * LANES
    n_top_dec = min(TOP_N, n_sb_dec)
    n_chunks_dec = past // CMP_STRIDE

    n_q = seq // TQ_ATT
    n_diag = min(n_q, REL_MAX_DIST // TK_ATT + 2)
    tz = _bias_table(rel_bias, n_diag, TK_ATT, TQ_ATT, TK_ATT, -1, 1, 0)
    tz = tz.reshape(KV_C, R_C, n_diag, TK_ATT, TQ_ATT).transpose(2, 0, 1, 3, 4)
    bias_cmp_p = _bias_table(rel_bias, n_q, TQ_ATT, seq // CMP_STRIDE, TQ_ATT, 1, -CMP_STRIDE,
                             -(CMP_LEN - 1))
    bias_cmp_d = _bias_table(rel_bias, 1, 8, n_chunks_dec, 0, 0, -CMP_STRIDE,
                             past - (CMP_LEN - 1))[:, 0, 0, :]
    n_tsel = -(-n_pages // 8) * 8
    tsel = _bias_table(rel_bias, 1, n_tsel, PAGE, 0, -PAGE, -1, past)[:, 0]
    tsel = jnp.swapaxes(tsel, 0, 1)
    twin = _bias_table(rel_bias, 1, 8, w_len, 0, 0, -1, w_len)[:, 0, 0, :]
    b0 = rel_bias[0].reshape(H_C, 1)

    pos_minor = lambda c: c.transpose(0, 1, 3, 4, 5, 2).reshape(c.shape[0], c.shape[1], -1, c.shape[2])
    cache_sb_t = pos_minor(cache_sb_kv)
    cache_cmp_t = pos_minor(cache_cmp_kv)
    cache_slc_t = pos_minor(cache_slc_kv)
    cache_win_t = pos_minor(cache_win_kv)
    from_pos_minor = lambda a, heads: jnp.moveaxis(
        a.reshape(a.shape[:-2] + (2, heads, HEAD_DIM, a.shape[-1])), -1, -4)

    xp = x_prompt.reshape(t_prompt, D_MODEL)
    xs = x_sample.reshape(db, D_MODEL)
    outs = {k: [] for k in ("p_sb", "p_cmp", "p_slc", "p_win", "s_sb", "s_cmp", "s_slc", "s_win", "s_gv")}
    row = lambda v: v.reshape(1, -1)

    for l in range(depth):
        w_in_l = jnp.pad(w_in[l], ((0, 0), (0, NP_IN - N_IN))).astype(BF16)
        w_out_l = w_out[l].astype(BF16)
        w_up_l = w_ffn_up[l].astype(BF16)
        w_down_l = w_ffn_down[l].astype(BF16)
        pe2, w1bd, w2rep, w2heads = _nsa_weights(cmp_pe[l], cmp_w1[l], cmp_w2[l])
        g_pre, g_post = row(norm_mix_pre[l]), row(norm_mix_post[l])
        g_grp = row(norm_group_out[l])
        gf_pre, gf_post = row(norm_ffn_pre[l]), row(norm_ffn_post[l])

        z = _project(xp, g_pre, w_in_l, tm_p)
        z3 = z.reshape(bsz, seq, NP_IN)
        o_a = _sb_prompt(z3)
        bias_rows = jnp.repeat(gmlp_b[l].T, CB, axis=1)
        o_b = _gmlp_prompt(z3, gmlp_ws[l], bias_rows)
        kvc = _compress_prompt(z3, pe2, w1bd, w2rep)
        o_cmp, sel_t = _nsa_cmp_prompt(z3, kvc, bias_cmp_p)
        slc_t = jnp.swapaxes(z3[:, :, C_SK:C_SK + 2 * KV_W], 1, 2)
        win_t = jnp.swapaxes(z3[:, :, C_WK:C_WK + 2 * KV_W], 1, 2)
        o_c = _nsa_attn_prompt(z3, slc_t, win_t, sel_t, tz, o_cmp)
        xp = _mix_out(o_a.reshape(t_prompt, D_A), o_b.reshape(t_prompt, D_B),
                      o_c.reshape(t_prompt, D_C), xp, g_grp, w_out_l, g_post, tm_p)
        xp = _ffn(xp, gf_pre, w_up_l, w_down_l, gf_post, tm_p, D_FF // 2)
        outs["p_sb"].append(z3[:, :, C_KA:C_KA + 2 * D_A].reshape(bsz, seq, 2, H_A, HEAD_DIM))
        kv_shape = (bsz, seq, 2, KV_C, HEAD_DIM)
        outs["p_cmp"].append(z3[:, :, C_CK:C_CK + 2 * KV_W].reshape(kv_shape))
        outs["p_slc"].append(from_pos_minor(slc_t, KV_C))
        n_win = min(WINDOW, seq)
        outs["p_win"].append(from_pos_minor(win_t[:, :, seq - n_win:], KV_C))

        zs = _project(xs, g_pre, w_in_l, db)
        zs3 = zs.reshape(db, 1, NP_IN)
        o_a_s = _sb_decode(zs3[:, :, C_QA:C_QA + D_A], cache_sb_t, page_table, l)
        w_row = jnp.repeat(gmlp_ws[l][:, 0, 0], CB).reshape(1, D_B)
        b_row = jnp.repeat(gmlp_b[l][:, 0], CB).reshape(1, D_B)
        o_b_s, vn_s = _gmlp_sample(zs[:, C_UB:C_UB + D_B], zs[:, C_VB:C_VB + D_B], w_row, b_row)
        qc_s = zs3[:, :, C_QC:C_QC + D_C]
        f_cmp, imp = _nsa_cmp_decode(qc_s, cache_cmp_t, page_table, l, pe2, w1bd, w2heads,
                                     bias_cmp_d, n_pad_dec)
        idx = _topk_decode(imp[:, :KV_C, :].reshape(db * KV_C, n_pad_dec), past // SLC_BLOCK, n_sb_dec)
        sel_idx = idx[:, :n_top_dec].reshape(-1)
        gate_s = zs3[:, :, C_GATE:C_GATE + LANES]
        new_slc = zs3[:, :, C_SK:C_SK + 2 * KV_W]
        new_win = zs3[:, :, C_WK:C_WK + 2 * KV_W]
        o_c_s, win_out = _nsa_attn_decode(qc_s, gate_s, new_slc, new_win, f_cmp, tsel, twin, b0,
                                          cache_slc_t, cache_win_t, page_table, sel_idx, l, n_top_dec)
        xs = _mix_out(o_a_s.reshape(db, D_A), o_b_s, o_c_s.reshape(db, D_C), xs, g_grp, w_out_l,
                      g_post, db)
        xs = _ffn(xs, gf_pre, w_up_l, w_down_l, gf_post, db, D_FF // 2)
        outs["s_sb"].append(zs[:, C_KA:C_KA + 2 * D_A].reshape(db, 1, 2, H_A, HEAD_DIM))
        outs["s_cmp"].append(zs[:, C_CK:C_CK + 2 * KV_W].reshape(db, 1, 2, KV_C, HEAD_DIM))
        outs["s_slc"].append(zs[:, C_SK:C_SK + 2 * KV_W].reshape(db, 1, 2, KV_C, HEAD_DIM))
        outs["s_win"].append(from_pos_minor(win_out, KV_C))
        outs["s_gv"].append(vn_s.reshape(db, 1, D_B))

    st = lambda k: jnp.stack(outs[k])
    return (xp.reshape(bsz, seq, D_MODEL), xs.reshape(db, 1, D_MODEL), st("p_sb"), st("p_cmp"),
            st("p_slc"), st("p_win"), st("s_sb"), st("s_cmp"), st("s_slc"), st("s_win"), st("s_gv"))
```

```python
import functools
import math

import jax
import jax.numpy as jnp
import numpy as np
from jax import lax
from jax.experimental import pallas as pl
from jax.experimental.pallas import tpu as pltpu

F32 = jnp.float32
BF16 = jnp.bfloat16

D_MODEL = 1024
HEAD_DIM = 64
D_A = 256
H_A = 4
D_B = 256
G_B = 4
CB = 64
D_C = 512
H_C = 8
KV_C = 2
R_C = 4
KV_W = 128
N_IN = 3 * D_A + 2 * D_B + D_C + 6 * KV_W + 3 * H_C
NP_IN = 2688
CHUNK = 128
CMP_LEN = 32
CMP_STRIDE = 16
SLC_BLOCK = 64
TOP_N = 16
N_LOCAL = 2
WINDOW = 512
NUM_BUCKETS = 32
REL_MAX_DIST = 2048
D_FF = 2816
EPS = 1e-6
SCALE = HEAD_DIM ** -0.5
PAGE = 128

C_QA, C_KA, C_VA, C_UB, C_VB, C_QC = 0, 256, 512, 768, 1024, 1280
C_CK, C_CV, C_SK, C_SV, C_WK, C_WV, C_GATE = 1792, 1920, 2048, 2176, 2304, 2432, 2560

LANES = 128
TQ_ATT = 128
TK_ATT = 128
TQ_NSA = 256
TK_NSA = 256
VMEM_LIMIT = 56 * 1024 * 1024
NEG = -1e30
DECAYED = -110.0
BIG_SCORE = 3e38
PAGES_PER_STEP = 8


def _cparams(sem):
    return pltpu.CompilerParams(dimension_semantics=sem, vmem_limit_bytes=VMEM_LIMIT)


def _dot(a, b):
    return jnp.dot(a, b, preferred_element_type=F32)


def _dot_nt(a, b):
    return lax.dot_general(a, b, (((1,), (1,)), ((), ())), preferred_element_type=F32)


def _rms(x, g):
    return x * lax.rsqrt(jnp.mean(x * x, axis=-1, keepdims=True) + EPS) * g


def _sigmoid(x):
    return 1.0 / (1.0 + jnp.exp(-x))


def _gelu_tanh(x):
    return 0.5 * x * (1.0 + jnp.tanh(math.sqrt(2.0 / math.pi) * (x + 0.044715 * (x * x * x))))


def _iota(shape, dim):
    return lax.broadcasted_iota(jnp.int32, shape, dim)


def _bucket_thresholds():
    n = np.arange(0, 1 << 15)
    exact = NUM_BUCKETS // 2
    nf = np.maximum(n, 1).astype(np.float32)
    big = exact + (np.log(nf / np.float32(exact)) / np.float32(math.log(REL_MAX_DIST / exact))
                   * np.float32(NUM_BUCKETS - exact)).astype(np.int32)
    bucket = np.where(n < exact, n, np.minimum(big, NUM_BUCKETS - 1))
    assert np.all(np.diff(bucket) >= 0)
    return [int(np.argmax(bucket >= k)) for k in range(1, NUM_BUCKETS)]


_THR = _bucket_thresholds()


def _bias_table_kernel(tab_ref, o_ref, *, a, rs, cs, c0):
    shape = o_ref.shape[2:]
    dist = a * pl.program_id(0) + rs * _iota(shape, 0) + cs * _iota(shape, 1) + c0
    outs = [jnp.full(shape, tab_ref[h], F32) for h in range(H_C)]
    for k in range(1, NUM_BUCKETS):
        ge = dist >= _THR[k - 1]
        for h in range(H_C):
            outs[h] = jnp.where(ge, tab_ref[k * H_C + h], outs[h])
    for h in range(H_C):
        o_ref[h, 0] = outs[h]


def _bias_table(rel_bias, steps, rows, cols, a, rs, cs, c0):
    return pl.pallas_call(
        functools.partial(_bias_table_kernel, a=a, rs=rs, cs=cs, c0=c0),
        out_shape=jax.ShapeDtypeStruct((H_C, steps, rows, cols), F32),
        grid=(steps,),
        in_specs=[pl.BlockSpec(memory_space=pltpu.SMEM)],
        out_specs=pl.BlockSpec((H_C, 1, rows, cols), lambda i: (0, i, 0, 0)),
        compiler_params=_cparams(("arbitrary",)),
        name="bias_table",
    )(rel_bias.reshape(-1))


def _proj_kernel(x_ref, g_ref, w_ref, o_ref):
    h = _rms(x_ref[...], g_ref[...])
    o_ref[...] = _dot(h.astype(BF16), w_ref[...])


def _project(x, g, w, tm):
    t = x.shape[0]
    return pl.pallas_call(
        _proj_kernel,
        out_shape=jax.ShapeDtypeStruct((t, NP_IN), F32),
        grid=(t // tm,),
        in_specs=[pl.BlockSpec((tm, D_MODEL), lambda i: (i, 0)),
                  pl.BlockSpec((1, D_MODEL), lambda i: (0, 0)),
                  pl.BlockSpec((D_MODEL, NP_IN), lambda i: (0, 0))],
        out_specs=pl.BlockSpec((tm, NP_IN), lambda i: (i, 0)),
        compiler_params=_cparams(("parallel",)),
        name="in_proj",
    )(x, g, w)


def _mixout_kernel(oa_ref, ob_ref, oc_ref, x_ref, gg_ref, w_ref, gp_ref, o_ref):
    gg = gg_ref[...]
    a = _rms(oa_ref[...], gg[:, :D_A]).astype(BF16)
    b = _rms(ob_ref[...], gg[:, D_A:D_A + D_B]).astype(BF16)
    c = _rms(oc_ref[...], gg[:, D_A + D_B:]).astype(BF16)
    y = (_dot(a, w_ref[0:D_A, :]) + _dot(b, w_ref[D_A:D_A + D_B, :])
         + _dot(c, w_ref[D_A + D_B:, :]))
    o_ref[...] = x_ref[...] + _rms(y, gp_ref[...])


def _mix_out(oa, ob, oc, x, gg, w, gp, tm):
    t = x.shape[0]
    row = lambda width: pl.BlockSpec((tm, width), lambda i: (i, 0))
    full = lambda r, c: pl.BlockSpec((r, c), lambda i: (0, 0))
    return pl.pallas_call(
        _mixout_kernel,
        out_shape=jax.ShapeDtypeStruct((t, D_MODEL), F32),
        grid=(t // tm,),
        in_specs=[row(D_A), row(D_B), row(D_C), row(D_MODEL), full(1, D_MODEL),
                  full(D_MODEL, D_MODEL), full(1, D_MODEL)],
        out_specs=row(D_MODEL),
        compiler_params=_cparams(("parallel",)),
        name="mix_out",
    )(oa, ob, oc, x, gg, w, gp)


def _ffn_kernel(x_ref, gpre_ref, wg_ref, wu_ref, wd_ref, gpost_ref, o_ref, h_ref, acc_ref):
    j = pl.program_id(1)

    @pl.when(j == 0)
    def _():
        h_ref[...] = _rms(x_ref[...], gpre_ref[...]).astype(BF16)
        acc_ref[...] = jnp.zeros_like(acc_ref)

    h = h_ref[...]
    g = _dot(h, wg_ref[...])
    u = _dot(h, wu_ref[...])
    act = (g * _sigmoid(g)) * u
    acc_ref[...] += _dot(act.astype(BF16), wd_ref[...])

    @pl.when(j == pl.num_programs(1) - 1)
    def _():
        o_ref[...] = x_ref[...] + _rms(acc_ref[...], gpost_ref[...])


def _ffn(x, gpre, w_up, w_down, gpost, tm, tf):
    t = x.shape[0]
    nf = D_FF // tf
    return pl.pallas_call(
        _ffn_kernel,
        out_shape=jax.ShapeDtypeStruct((t, D_MODEL), F32),
        grid=(t // tm, nf),
        in_specs=[pl.BlockSpec((tm, D_MODEL), lambda i, j: (i, 0)),
                  pl.BlockSpec((1, D_MODEL), lambda i, j: (0, 0)),
                  pl.BlockSpec((D_MODEL, tf), lambda i, j: (0, j)),
                  pl.BlockSpec((D_MODEL, tf), lambda i, j: (0, j + nf)),
                  pl.BlockSpec((tf, D_MODEL), lambda i, j: (j, 0)),
                  pl.BlockSpec((1, D_MODEL), lambda i, j: (0, 0))],
        out_specs=pl.BlockSpec((tm, D_MODEL), lambda i, j: (i, 0)),
        scratch_shapes=[pltpu.VMEM((tm, D_MODEL), BF16), pltpu.VMEM((tm, D_MODEL), F32)],
        compiler_params=_cparams(("parallel", "arbitrary")),
        name="ffn",
    )(x, gpre, w_up, w_up, w_down, gpost)


def _log_sig_pair(z):
    soft = jnp.log(1.0 + jnp.exp(-jnp.abs(z)))
    log_beta = jnp.minimum(z, 0.0) - soft
    return log_beta, log_beta - z


def _times_01_matrix(x, mat_bf16, pieces=3):
    out = None
    rem = x
    for k in range(pieces):
        part = rem.astype(BF16)
        term = _dot(part, mat_bf16)
        out = term if out is None else out + term
        if k + 1 < pieces:
            rem = rem - part.astype(F32)
    return out


def _suffix_sum_exclusive(x, upper_bf16):
    return _times_01_matrix(x, upper_bf16, pieces=2)


def _strict_upper(n):
    return jnp.where(_iota((n, n), 0) > _iota((n, n), 1), 1.0, 0.0).astype(BF16)


def _sb_prompt_kernel(q_ref, k_ref, v_ref, o_ref, run_ref, acc_ref, *, tq, tk):
    i = pl.program_id(1)
    lane_head = _iota((tq, D_A), 1) // HEAD_DIM
    qs = q_ref[0] * SCALE
    qh = [jnp.where(lane_head == h, qs, 0.0).astype(BF16) for h in range(H_A)]
    upper = _strict_upper(tk)
    run_ref[...] = jnp.zeros_like(run_ref)
    acc_ref[...] = jnp.zeros_like(acc_ref)

    def sweep_tile(j, diagonal):
        ks = pl.multiple_of(j * tk, tk)
        kt = k_ref[0, pl.ds(ks, tk), :].astype(BF16)
        vt = v_ref[0, pl.ds(ks, tk), :].astype(BF16)
        mask = _iota((tq, tk), 1) < _iota((tq, tk), 0)
        heads = range(H_A)
        z = [_dot_nt(qh[h], kt) for h in heads]
        pairs = [_log_sig_pair(z[h]) for h in heads]
        lb = [pairs[h][0] for h in heads]
        lk = [jnp.where(mask, pairs[h][1], 0.0) if diagonal else pairs[h][1] for h in heads]
        after = [_suffix_sum_exclusive(lk[h], upper) + run_ref[h] for h in heads]
        a = [jnp.exp(lb[h] + after[h]) for h in heads]
        if diagonal:
            a = [jnp.where(mask, a[h], 0.0) for h in heads]
        pv = [_dot(a[h].astype(BF16), vt) for h in heads]
        worst = jnp.float32(-jnp.inf)
        for h in heads:
            acc_ref[h] += pv[h]
            new_run = after[h][:, 0:1] + lk[h][:, 0:1]
            run_ref[h] = new_run
            worst = jnp.maximum(worst, jnp.max(new_run))
        return (worst > DECAYED).astype(jnp.int32)

    def cond(c):
        j, alive = c
        return jnp.logical_and(j >= 0, alive > 0)

    def body(c):
        return c[0] - 1, sweep_tile(c[0], False)

    assert tq == tk
    lax.while_loop(cond, body, (i - 1, sweep_tile(i, True)))
    out = acc_ref[0]
    for h in range(1, H_A):
        out = jnp.where(lane_head == h, acc_ref[h], out)
    o_ref[0] = out


def _sb_prompt(z3, tq=256, tk=256):
    b, s, _ = z3.shape
    return pl.pallas_call(
        functools.partial(_sb_prompt_kernel, tq=tq, tk=tk),
        out_shape=jax.ShapeDtypeStruct((b, s, D_A), F32),
        grid=(b, s // tq),
        in_specs=[pl.BlockSpec((1, tq, D_A), lambda bi, i: (bi, i, C_QA // D_A)),
                  pl.BlockSpec((1, s, D_A), lambda bi, i: (bi, 0, C_KA // D_A)),
                  pl.BlockSpec((1, s, D_A), lambda bi, i: (bi, 0, C_VA // D_A))],
        out_specs=pl.BlockSpec((1, tq, D_A), lambda bi, i: (bi, i, 0)),
        scratch_shapes=[pltpu.VMEM((H_A, tq, 1), F32), pltpu.VMEM((H_A, tq, D_A), F32)],
        compiler_params=_cparams(("parallel", "arbitrary")),
        name="sb_prompt",
    )(z3, z3, z3)


def _head_rows(row_vec, width):
    full = jnp.broadcast_to(row_vec, (8, width))
    return jnp.where(_iota((8, width), 1) // HEAD_DIM == _iota((8, width), 0), full, 0.0)


def _sb_decode_kernel(pt_ref, alive_ref, q_ref, run_in_ref, acc_in_ref, *rest, n_pages_step, n_steps):
    del pt_ref
    pages = rest[:n_pages_step]
    run_ref, acc_ref, o_ref, live_ref = rest[n_pages_step:]
    b = pl.program_id(0)
    step = pl.program_id(1)

    @pl.when(step == 0)
    def _():
        run_ref[...] = run_in_ref[...]
        acc_ref[...] = acc_in_ref[...]
        live_ref[0] = alive_ref[b]

    @pl.when(live_ref[0] > 0)
    def _():
        q8 = (_head_rows(q_ref[0], D_A) * SCALE).astype(BF16)
        upper = _strict_upper(PAGE)
        run = run_ref[0][:, 0:1]
        acc = acc_ref[0]
        for page in pages:
            kt = page[0, 0, 0:D_A, :].astype(BF16)
            vt = page[0, 0, D_A:2 * D_A, :].astype(BF16)
            z = _dot(q8, kt)
            lb, lk = _log_sig_pair(z)
            after = _suffix_sum_exclusive(lk, upper) + run
            a = jnp.exp(lb + after)
            acc = acc + _dot_nt(a.astype(BF16), vt)
            run = after[:, 0:1] + lk[:, 0:1]
        run_ref[0] = jnp.broadcast_to(run, (8, LANES))
        acc_ref[0] = acc
        head_rows = _iota((8, 1), 0) < H_A
        live_ref[0] = (jnp.max(jnp.where(head_rows, run, -jnp.inf)) > DECAYED).astype(jnp.int32)

    @pl.when(step == n_steps - 1)
    def _():
        o_ref[0] = jnp.sum(jnp.where(_own_lanes(D_A), acc_ref[0], 0.0), axis=0, keepdims=True)


def _sb_decode_phase(qa, cache_t, page_table, layer, alive, run, acc, first_back, n_pages_step,
                     n_steps):
    db, n_pages = page_table.shape

    def page_spec(k):
        def imap(b, s, pt, al):
            page = pt[b, n_pages - 1 - first_back - (s * n_pages_step + k)]
            return (layer, jnp.where(al[b] > 0, page, 0), 0, 0)
        return pl.BlockSpec((1, 1, 2 * D_A, PAGE), imap)

    per_b = lambda shape: pl.BlockSpec(shape, lambda b, s, pt, al: (b,) + (0,) * (len(shape) - 1))
    grid_spec = pltpu.PrefetchScalarGridSpec(
        num_scalar_prefetch=2,
        grid=(db, n_steps),
        in_specs=[per_b((1, 1, D_A)), per_b((1, 8, LANES)), per_b((1, 8, D_A))]
                 + [page_spec(k) for k in range(n_pages_step)],
        out_specs=(per_b((1, 8, LANES)), per_b((1, 8, D_A)), per_b((1, 1, D_A))),
        scratch_shapes=[pltpu.SMEM((1,), jnp.int32)],
    )
    return pl.pallas_call(
        functools.partial(_sb_decode_kernel, n_pages_step=n_pages_step, n_steps=n_steps),
        out_shape=(jax.ShapeDtypeStruct((db, 8, LANES), F32), jax.ShapeDtypeStruct((db, 8, D_A), F32),
                   jax.ShapeDtypeStruct((db, 1, D_A), F32)),
        grid_spec=grid_spec,
        compiler_params=_cparams(("arbitrary", "arbitrary")),
        name="sb_decode",
    )(page_table, alive, qa, run, acc, *([cache_t] * n_pages_step))


SB_FIRST_PAGES = 4


def _sb_decode(qa, cache_t, page_table, layer):
    db, n_pages = page_table.shape
    first = min(SB_FIRST_PAGES, n_pages)
    ones = jnp.ones((db,), jnp.int32)
    run0 = jnp.zeros((db, 8, LANES), F32)
    acc0 = jnp.zeros((db, 8, D_A), F32)
    run1, acc1, o1 = _sb_decode_phase(qa, cache_t, page_table, layer, ones, run0, acc0, 0, first, 1)
    rest = n_pages - first
    if rest == 0:
        return o1
    per_step = max(d for d in range(1, 13) if rest % d == 0)
    alive = (jnp.max(run1[:, :H_A, 0], axis=1) > DECAYED).astype(jnp.int32)
    return lax.cond(
        jnp.any(alive > 0),
        lambda: _sb_decode_phase(qa, cache_t, page_table, layer, alive, run1, acc1, first, per_step,
                                 rest // per_step)[2],
        lambda: o1)


def _layernorm(x):
    xc = x - jnp.mean(x, axis=-1, keepdims=True)
    return xc * lax.rsqrt(jnp.mean(xc * xc, axis=-1, keepdims=True) + EPS)


def _gmlp_prompt_kernel(u_ref, v_ref, w_ref, b_ref, o_ref, *, n_chunks):
    tril = _iota((CHUNK, CHUNK), 0) >= _iota((CHUNK, CHUNK), 1)
    ws = [jnp.where(tril, w_ref[g], 0.0).astype(BF16) for g in range(G_B)]
    group = _iota((CHUNK, D_B), 1) // CB
    for c in range(n_chunks):
        rows = slice(c * CHUNK, (c + 1) * CHUNK)
        vn = _layernorm(v_ref[0, rows, :]).astype(BF16)
        mixed = b_ref[...]
        for g in range(G_B):
            mixed = mixed + jnp.where(group == g, _dot(ws[g], vn), 0.0)
        o_ref[0, rows, :] = u_ref[0, rows, :] * mixed


def _gmlp_prompt(z3, ws, bias_rows, n_chunks=4):
    b, s, _ = z3.shape
    tr = n_chunks * CHUNK
    return pl.pallas_call(
        functools.partial(_gmlp_prompt_kernel, n_chunks=n_chunks),
        out_shape=jax.ShapeDtypeStruct((b, s, D_B), F32),
        grid=(b, s // tr),
        in_specs=[pl.BlockSpec((1, tr, D_B), lambda bi, i: (bi, i, C_UB // D_B)),
                  pl.BlockSpec((1, tr, D_B), lambda bi, i: (bi, i, C_VB // D_B)),
                  pl.BlockSpec((G_B, CHUNK, CHUNK), lambda bi, i: (0, 0, 0)),
                  pl.BlockSpec((CHUNK, D_B), lambda bi, i: (0, 0))],
        out_specs=pl.BlockSpec((1, tr, D_B), lambda bi, i: (bi, i, 0)),
        compiler_params=_cparams(("parallel", "parallel")),
        name="gmlp_prompt",
    )(z3, z3, ws, bias_rows)


def _gmlp_sample_kernel(u_ref, v_ref, w_ref, b_ref, o_ref, vn_ref):
    vn = _layernorm(v_ref[...])
    vn_ref[...] = vn
    o_ref[...] = u_ref[...] * (w_ref[...] * vn + b_ref[...])


def _gmlp_sample(u, v, w_row, b_row):
    return pl.pallas_call(
        _gmlp_sample_kernel,
        out_shape=(jax.ShapeDtypeStruct(u.shape, F32), jax.ShapeDtypeStruct(u.shape, F32)),
        name="gmlp_sample",
    )(u, v, w_row, b_row)


def _compress_hidden(load_rows, pe_ref, w1_fn, n_chunks, width):
    hid_a = jnp.zeros((n_chunks, width), F32)
    hid_b = jnp.zeros((n_chunks, width), F32)
    for j in range(CMP_STRIDE):
        x = load_rows(j)
        hid_a += _dot((x + pe_ref[j:j + 1, :]).astype(BF16), w1_fn(j))
        hid_b += _dot((x + pe_ref[CMP_STRIDE + j:CMP_STRIDE + j + 1, :]).astype(BF16),
                      w1_fn(CMP_STRIDE + j))
    return hid_a + pltpu.roll(hid_b, n_chunks - 1, 0)


def _compress_prompt_kernel(z_ref, pe_ref, w1_ref, w2_ref, o_ref, *, n_chunks):
    hid = _compress_hidden(lambda j: z_ref[0, pl.ds(j, n_chunks, stride=CMP_STRIDE), :],
                           pe_ref.at[0], lambda j: w1_ref[0, j], n_chunks, KV_W)
    act = _gelu_tanh(hid).astype(BF16)
    valid = _iota((n_chunks, 4 * HEAD_DIM), 0) < n_chunks - 1
    for g in range(KV_C):
        o_ref[0, 0, g] = jnp.where(valid, _dot(act, w2_ref[0, g]), 0.0)


def _compress_prompt(z3, pe2, w1bd, w2rep):
    b, s, _ = z3.shape
    n_chunks = s // CMP_STRIDE
    return pl.pallas_call(
        functools.partial(_compress_prompt_kernel, n_chunks=n_chunks),
        out_shape=jax.ShapeDtypeStruct((b, 2, KV_C, n_chunks, 4 * HEAD_DIM), F32),
        grid=(b, 2),
        in_specs=[pl.BlockSpec((1, s, KV_W), lambda bi, kv: (bi, 0, C_CK // KV_W + kv)),
                  pl.BlockSpec((1, CMP_LEN, KV_W), lambda bi, kv: (kv, 0, 0)),
                  pl.BlockSpec((1, CMP_LEN, KV_W, KV_W), lambda bi, kv: (kv, 0, 0, 0)),
                  pl.BlockSpec((1, KV_C, KV_W, 4 * HEAD_DIM), lambda bi, kv: (kv, 0, 0, 0))],
        out_specs=pl.BlockSpec((1, 1, KV_C, n_chunks, 4 * HEAD_DIM),
                               lambda bi, kv: (bi, kv, 0, 0, 0)),
        compiler_params=_cparams(("parallel", "parallel")),
        name="nsa_compress_prompt",
    )(z3, pe2, w1bd, w2rep)


def _select_blocks_t(score, causal, n_top):
    n_sb = score.shape[0]
    jt = _iota(score.shape, 0)
    rank = jnp.zeros(score.shape, F32)
    for jp in range(n_sb):
        row = score[jp:jp + 1, :]
        ge = jnp.where(row >= score, 1.0, 0.0)
        gt = jnp.where(row > score, 1.0, 0.0)
        rank += jnp.where(jt > jp, ge, gt)
    return jnp.where(causal, jnp.where(rank < n_top, 1.0, 0.0), 0.0)


def _nsa_cmp_prompt_kernel(q0_ref, q1_ref, kvc_ref, bias_ref, ocmp_ref, sel_ref, *, n_sb, n_cb):
    i = pl.program_id(1)
    tq = TQ_ATT
    n_pad = kvc_ref.shape[3]
    lane_head = _iota((tq, 4 * HEAD_DIM), 1) // HEAD_DIM
    n_idx = _iota((tq, n_pad), 1)
    dist = (i * tq + _iota((tq, n_pad), 0)) - (n_idx * CMP_STRIDE + (CMP_LEN - 1))
    mask = jnp.logical_and(dist >= 0, n_idx < n_cb)
    c0 = _iota((n_sb, n_pad), 1) * CMP_STRIDE
    s0 = _iota((n_sb, n_pad), 0) * SLC_BLOCK
    ov_t = jnp.where(jnp.logical_and(c0 < s0 + SLC_BLOCK, c0 + CMP_LEN > s0), 1.0, 0.0).astype(BF16)
    t_row = i * tq + _iota((n_sb, tq), 1)
    cur = t_row // SLC_BLOCK
    jt = _iota((n_sb, tq), 0)
    causal = jt <= cur
    forced = jnp.logical_or(jt == 0, jt > cur - N_LOCAL)
    s_all = []
    for g, q_ref in enumerate((q0_ref, q1_ref)):
        qs = q_ref[0] * SCALE
        qm = jnp.concatenate([jnp.where(lane_head == r, qs, 0.0) for r in range(R_C)],
                             axis=0).astype(BF16)
        s_all.append(_dot_nt(qm, kvc_ref[0, 0, g].astype(BF16)))
    probs = []
    for h in range(H_C):
        g, r = h // R_C, h % R_C
        sm = jnp.where(mask, s_all[g][r * tq:(r + 1) * tq] + bias_ref[h, 0], NEG)
        m = jnp.max(sm, axis=-1, keepdims=True)
        p = jnp.where(mask, jnp.exp(sm - m), 0.0)
        p = p / jnp.maximum(jnp.sum(p, axis=-1, keepdims=True), 1e-30)
        probs.append(p.astype(BF16))
    sel_t = []
    for g in range(KV_C):
        vc = kvc_ref[0, 1, g].astype(BF16)
        o_acc = jnp.zeros((tq, 4 * HEAD_DIM), F32)
        imp_t = jnp.zeros((n_sb, tq), F32)
        for r in range(R_C):
            pb = probs[R_C * g + r]
            o_acc += jnp.where(lane_head == r, _dot(pb, vc), 0.0)
            imp_t += _dot_nt(ov_t, pb)
        ocmp_ref[0, :, g * 4 * HEAD_DIM:(g + 1) * 4 * HEAD_DIM] = o_acc
        score = jnp.where(causal, jnp.where(forced, BIG_SCORE, imp_t), -1.0)
        sel_t.append(_select_blocks_t(score, causal, min(TOP_N, n_sb)))
    pad = jnp.zeros((SLC_BLOCK - n_sb, tq), F32)
    sel_ref[0] = jnp.concatenate([sel_t[0], pad, sel_t[1], pad] if n_sb < SLC_BLOCK else sel_t,
                                 axis=0)


def _nsa_cmp_prompt(z3, kvc, bias_cmp):
    b, s, _ = z3.shape
    n_sb = s // SLC_BLOCK
    n_cb = (s - CMP_LEN) // CMP_STRIDE + 1
    n_pad = kvc.shape[3]
    wq = 4 * HEAD_DIM
    return pl.pallas_call(
        functools.partial(_nsa_cmp_prompt_kernel, n_sb=n_sb, n_cb=n_cb),
        out_shape=(jax.ShapeDtypeStruct((b, s, D_C), F32),
                   jax.ShapeDtypeStruct((b, 2 * SLC_BLOCK, s), F32)),
        grid=(b, s // TQ_ATT),
        in_specs=[pl.BlockSpec((1, TQ_ATT, wq), lambda bi, i: (bi, i, C_QC // wq)),
                  pl.BlockSpec((1, TQ_ATT, wq), lambda bi, i: (bi, i, C_QC // wq + 1)),
                  pl.BlockSpec((1, 2, KV_C, n_pad, wq), lambda bi, i: (bi, 0, 0, 0, 0)),
                  pl.BlockSpec((H_C, 1, TQ_ATT, n_pad), lambda bi, i: (0, i, 0, 0))],
        out_specs=(pl.BlockSpec((1, TQ_ATT, D_C), lambda bi, i: (bi, i, 0)),
                   pl.BlockSpec((1, 2 * SLC_BLOCK, TQ_ATT), lambda bi, i: (bi, 0, i))),
        compiler_params=_cparams(("parallel", "parallel")),
        name="nsa_cmp_prompt",
    )(z3, z3, kvc, bias_cmp)


def _score_tile(args, slot, s_ref, peak_ref, tz_ref, n_diag):
    raw, valid_f, d0 = args
    tk, tb = raw.shape[0], TK_ATT
    nq = valid_f.shape[1] // tb
    tabs = {e: tz_ref[jnp.clip(d0 + e, 0, n_diag - 1), 0] for e in range(-(tk // tb - 1), nq)}
    bias = jnp.concatenate(
        [jnp.concatenate([tabs[cq - ck][r] for r in range(R_C) for cq in range(nq)], axis=1)
         for ck in range(tk // tb)], axis=0)
    valid = jnp.concatenate([valid_f] * R_C, axis=1) > 0.5
    sm = jnp.where(valid, raw + bias, NEG)
    s_ref[slot] = sm
    peak_ref[slot] = jnp.broadcast_to(jnp.max(sm, axis=0, keepdims=True), peak_ref.shape[1:])


def _absorb_tile(vt, slot, s_ref, peak_ref, m_ref, l_ref, acc_ref):
    sm = s_ref[slot]
    m_prev = m_ref[...]
    m_new = jnp.maximum(m_prev, peak_ref[slot][0:1])
    alpha = jnp.exp(m_prev - m_new)
    p = jnp.exp((sm - m_new).astype(BF16))
    pv = _dot(jnp.concatenate([vt, jnp.ones((16, vt.shape[1]), BF16)], axis=0), p)
    l_ref[...] = alpha * l_ref[...] + pv[HEAD_DIM:HEAD_DIM + 1]
    acc_ref[...] = alpha * acc_ref[...] + pv[:HEAD_DIM]
    m_ref[...] = m_new


def _nsa_attn_prompt_kernel(q_ref, sk_ref, svt_ref, wk_ref, wvt_ref, selt_ref, tz_ref, gate_ref,
                            ocmp_ref, o_ref, s_ref, peak_ref, m_ref, l_ref, acc_ref, *, n_diag):
    g = pl.program_id(1)
    i = pl.program_id(2)
    tq, tk, tb = TQ_NSA, TK_NSA, TK_ATT
    wq = 4 * HEAD_DIM
    lane = _iota((tq, LANES), 1)
    own_half = (lane // HEAD_DIM) == g
    q = q_ref[0]
    cols = []
    for r in range(R_C):
        a = pltpu.roll(q, (wq - r * HEAD_DIM) % wq, 1)[:, :LANES]
        both = jnp.where(g == 0, a, pltpu.roll(a, HEAD_DIM, 1))
        cols.append((jnp.where(own_half, both, 0.0) * SCALE).T)
    qt = jnp.concatenate(cols, axis=1).astype(BF16)
    selt = selt_ref[0].astype(BF16)
    qpos = i * tq + _iota((tk, tq), 1)
    key_in_tile = _iota((tk, tq), 0)
    own_rows = pl.multiple_of(g * HEAD_DIM, HEAD_DIM)

    def reset():
        m_ref[...] = jnp.full(m_ref.shape, NEG, F32)
        l_ref[...] = jnp.zeros_like(l_ref)
        acc_ref[...] = jnp.zeros_like(acc_ref)

    def slc_products(jt):
        ks = pl.multiple_of(jt * tk, tk)
        kt = sk_ref[0, pl.ds(ks, tk), :].astype(BF16)
        target = (g * SLC_BLOCK + (tk // SLC_BLOCK) * jt + (_iota((tk, LANES), 0) // SLC_BLOCK))
        expand = jnp.where(_iota((tk, LANES), 1) == target, 1.0, 0.0).astype(BF16)
        chosen = _dot(expand, selt)
        valid_f = jnp.where(ks + key_in_tile <= qpos, chosen, 0.0)
        return _dot(kt, qt), valid_f, (i * tq - ks) // tb

    def win_products(jt):
        ks = pl.multiple_of(jt * tk, tk)
        kt = wk_ref[0, pl.ds(ks, tk), :].astype(BF16)
        dist = qpos - (ks + key_in_tile)
        valid_f = jnp.where(jnp.logical_and(dist >= 0, dist <= WINDOW), 1.0, 0.0)
        return _dot(kt, qt), valid_f, (i * tq - ks) // tb

    def sweep(products, values_ref, first, last):
        def absorb(jt):
            vt = values_ref[0, jt, pl.ds(own_rows, HEAD_DIM), :].astype(BF16)
            _absorb_tile(vt, jt % 2, s_ref, peak_ref, m_ref, l_ref, acc_ref)

        def body(jt, carry):
            started = products(jt)
            absorb(jt - 1)
            _score_tile(started, jt % 2, s_ref, peak_ref, tz_ref, n_diag)
            return carry

        _score_tile(products(first), first % 2, s_ref, peak_ref, tz_ref, n_diag)
        lax.fori_loop(first + 1, last + 1, body, 0)
        absorb(last)

    def normalised_out():
        acc_t = acc_ref[...] / l_ref[...]
        out = jnp.zeros((tq, wq), F32)
        pad_rows = jnp.zeros((LANES - HEAD_DIM, tq), F32)
        for r in range(R_C):
            x = jnp.concatenate([acc_t[:, r * tq:(r + 1) * tq], pad_rows], axis=0).T
            x = jnp.concatenate([x, jnp.zeros((tq, wq - LANES), F32)], axis=1)
            out += x if r == 0 else pltpu.roll(x, r * HEAD_DIM, 1)
        return out

    last_tile = ((i + 1) * tq - 1) // tk
    reset()
    sweep(slc_products, svt_ref, 0, last_tile)
    o_slc = normalised_out()
    reset()
    sweep(win_products, wvt_ref, jnp.maximum(i * tq - WINDOW, 0) // tk, last_tile)
    o_win = normalised_out()

    sig = _sigmoid(gate_ref[0])
    gate_lane = R_C * g + _iota((LANES, wq), 1) // HEAD_DIM
    out = jnp.zeros((tq, wq), F32)
    for k, branch in enumerate((ocmp_ref[0], o_slc, o_win)):
        place = jnp.where(_iota((LANES, wq), 0) == k * H_C + gate_lane, 1.0, 0.0).astype(BF16)
        out += _times_01_matrix(sig, place) * branch
    o_ref[0] = out


def _value_tiles(kv_t):
    b, _, s = kv_t.shape
    v_t = kv_t[:, KV_W:, :].reshape(b, KV_W, s // TK_NSA, TK_NSA)
    return jnp.swapaxes(v_t, 1, 2)


def _nsa_attn_prompt(z3, slc_t, win_t, sel_t, tz, ocmp):
    b, s, _ = z3.shape
    wq = 4 * HEAD_DIM
    n_diag = tz.shape[0]
    n_kt = s // TK_NSA
    k_spec = lambda col: pl.BlockSpec((1, s, KV_W), lambda bi, g, i: (bi, 0, col // KV_W))
    vt_spec = pl.BlockSpec((1, n_kt, KV_W, TK_NSA), lambda bi, g, i: (bi, 0, 0, 0))
    return pl.pallas_call(
        functools.partial(_nsa_attn_prompt_kernel, n_diag=n_diag),
        out_shape=jax.ShapeDtypeStruct((b, s, D_C), F32),
        grid=(b, KV_C, s // TQ_NSA),
        in_specs=[pl.BlockSpec((1, TQ_NSA, wq), lambda bi, g, i: (bi, i, C_QC // wq + g)),
                  k_spec(C_SK), vt_spec, k_spec(C_WK), vt_spec,
                  pl.BlockSpec((1, 2 * SLC_BLOCK, TQ_NSA), lambda bi, g, i: (bi, 0, i)),
                  pl.BlockSpec((n_diag, 1, R_C, TK_ATT, TQ_ATT), lambda bi, g, i: (0, g, 0, 0, 0)),
                  pl.BlockSpec((1, TQ_NSA, LANES), lambda bi, g, i: (bi, i, C_GATE // LANES)),
                  pl.BlockSpec((1, TQ_NSA, wq), lambda bi, g, i: (bi, i, g))],
        out_specs=pl.BlockSpec((1, TQ_NSA, wq), lambda bi, g, i: (bi, i, g)),
        scratch_shapes=[pltpu.VMEM((2, TK_NSA, R_C * TQ_NSA), F32),
                        pltpu.VMEM((2, 8, R_C * TQ_NSA), F32),
                        pltpu.VMEM((1, R_C * TQ_NSA), F32),
                        pltpu.VMEM((1, R_C * TQ_NSA), F32),
                        pltpu.VMEM((HEAD_DIM, R_C * TQ_NSA), F32)],
        compiler_params=_cparams(("parallel", "parallel", "arbitrary")),
        name="nsa_attn_prompt",
    )(z3, z3, _value_tiles(slc_t), z3, _value_tiles(win_t), sel_t, tz, z3, ocmp)


def _own_lanes(width):
    return _iota((8, width), 1) // HEAD_DIM == _iota((8, width), 0)


def _nsa_cmp_decode_kernel(pt_ref, q_ref, pe_ref, w1_ref, w2_ref, bias_ref, *rest,
                           n_steps, n_sb):
    del pt_ref
    pages = rest[:PAGES_PER_STEP]
    f_ref, imp_ref, buf_ref = rest[PAGES_PER_STEP:]
    step = pl.program_id(1)
    rows_per_page = PAGE // CMP_STRIDE
    out_row = _iota((PAGE, PAGE), 0)
    regroup = jnp.where(_iota((PAGE, PAGE), 1)
                        == CMP_STRIDE * (out_row % rows_per_page) + out_row // rows_per_page,
                        1.0, 0.0).astype(BF16)
    for k, page in enumerate(pages):
        row0 = pl.multiple_of((step * PAGES_PER_STEP + k) * rows_per_page, rows_per_page)
        x = _dot_nt(regroup, page[0, 0].astype(BF16))
        for kv in range(2):
            for j in range(CMP_STRIDE):
                buf_ref[kv, j, pl.ds(row0, rows_per_page), :] = (
                    x[j * rows_per_page:(j + 1) * rows_per_page, kv * KV_W:(kv + 1) * KV_W])

    @pl.when(step == n_steps - 1)
    def _():
        n_chunks = buf_ref.shape[2]
        n_cb = n_chunks - 1
        kv8 = []
        for kv in range(2):
            hid = _compress_hidden(
                lambda j, kv=kv: buf_ref[kv, j],
                pe_ref.at[kv], lambda j, kv=kv: w1_ref[kv, j], n_chunks, KV_W)
            kv8.append(_dot(_gelu_tanh(hid).astype(BF16), w2_ref[kv]).astype(BF16))
        k8, v8 = kv8
        q8 = (_head_rows(q_ref[0], D_C) * SCALE).astype(BF16)
        s = _dot_nt(q8, k8) + bias_ref[...]
        n_idx = _iota((8, n_chunks), 1)
        mask = n_idx < n_cb
        sm = jnp.where(mask, s, NEG)
        m = jnp.max(sm, axis=-1, keepdims=True)
        p = jnp.where(mask, jnp.exp(sm - m), 0.0)
        p = p / jnp.maximum(jnp.sum(p, axis=-1, keepdims=True), 1e-30)
        pb = p.astype(BF16)
        f_ref[0] = jnp.where(_own_lanes(D_C), _dot(pb, v8), 0.0)
        n_pad = imp_ref.shape[2]
        c0 = _iota((n_chunks, n_pad), 0) * CMP_STRIDE
        j_idx = _iota((n_chunks, n_pad), 1)
        s0 = j_idx * SLC_BLOCK
        ov = jnp.logical_and(jnp.logical_and(c0 < s0 + SLC_BLOCK, c0 + CMP_LEN > s0),
                             jnp.logical_and(_iota((n_chunks, n_pad), 0) < n_cb, j_idx < n_sb))
        imp8 = _dot(pb, jnp.where(ov, 1.0, 0.0).astype(BF16))
        row = _iota((8, n_pad), 0)
        g0 = jnp.sum(jnp.where(row < R_C, imp8, 0.0), axis=0, keepdims=True)
        g1 = jnp.sum(jnp.where(row >= R_C, imp8, 0.0), axis=0, keepdims=True)
        imp_ref[0] = jnp.where(row == 0, g0, jnp.where(row == 1, g1, 0.0))


def _nsa_cmp_decode(qc, cache4, page_table, layer, pe2, w1bd, w2heads, bias_cmp, n_pad):
    db, n_pages = page_table.shape
    past = n_pages * PAGE
    n_steps = n_pages // PAGES_PER_STEP
    n_sb = -(-(past + 1) // SLC_BLOCK)
    n_chunks = past // CMP_STRIDE
    const = lambda shape: pl.BlockSpec(shape, lambda b, s, pt: (0,) * len(shape))

    def page_spec(k):
        def imap(b, s, pt):
            return (layer, pt[b, s * PAGES_PER_STEP + k], 0, 0)
        return pl.BlockSpec((1, 1, 2 * KV_W, PAGE), imap)

    grid_spec = pltpu.PrefetchScalarGridSpec(
        num_scalar_prefetch=1,
        grid=(db, n_steps),
        in_specs=[pl.BlockSpec((1, 1, D_C), lambda b, s, pt: (b, 0, 0)),
                  const((2, CMP_LEN, KV_W)), const((2, CMP_LEN, KV_W, KV_W)),
                  const((2, KV_W, D_C)), const((8, n_chunks))]
                 + [page_spec(k) for k in range(PAGES_PER_STEP)],
        out_specs=(pl.BlockSpec((1, 8, D_C), lambda b, s, pt: (b, 0, 0)),
                   pl.BlockSpec((1, 8, n_pad), lambda b, s, pt: (b, 0, 0))),
        scratch_shapes=[pltpu.VMEM((2, CMP_STRIDE, n_chunks, KV_W), F32)],
    )
    return pl.pallas_call(
        functools.partial(_nsa_cmp_decode_kernel, n_steps=n_steps, n_sb=n_sb),
        out_shape=(jax.ShapeDtypeStruct((db, 8, D_C), F32),
                   jax.ShapeDtypeStruct((db, 8, n_pad), F32)),
        grid_spec=grid_spec,
        compiler_params=_cparams(("parallel", "arbitrary")),
        name="nsa_cmp_decode",
    )(page_table, qc, pe2, w1bd, w2heads, bias_cmp, *([cache4] * PAGES_PER_STEP))


def _topk_decode_kernel(imp_ref, idx_ref, *, cur, n_sb, n_top):
    shape = imp_ref.shape
    lane = _iota(shape, 1)
    lane_f = lane.astype(F32)
    causal = jnp.logical_and(lane <= cur, lane < n_sb)
    forced = jnp.logical_or(lane == 0, lane > cur - N_LOCAL)
    score = jnp.where(causal, jnp.where(forced, BIG_SCORE, imp_ref[...]), -1.0)
    out_lane = _iota(idx_ref.shape, 1)
    out = jnp.full(idx_ref.shape, -1, jnp.int32)
    for k in range(n_top):
        m = jnp.max(score, axis=-1, keepdims=True)
        first = jnp.min(jnp.where(score == m, lane_f, 1e9), axis=-1, keepdims=True)
        pick = jnp.where(m > -0.5, first, -1.0).astype(jnp.int32)
        out = jnp.where(out_lane == k, pick, out)
        score = jnp.where(lane_f == first, -2.0, score)
    idx_ref[...] = out


def _topk_decode(imp_rows, cur, n_sb):
    return pl.pallas_call(
        functools.partial(_topk_decode_kernel, cur=cur, n_sb=n_sb, n_top=min(TOP_N, n_sb)),
        out_shape=jax.ShapeDtypeStruct((imp_rows.shape[0], LANES), jnp.int32),
        name="nsa_topk_decode",
    )(imp_rows)


def _softmax_with_new(s, valid, s_new, has_new):
    sm = jnp.where(valid, s, NEG)
    sn = jnp.where(has_new, s_new, NEG)
    m = jnp.maximum(jnp.max(sm, axis=-1, keepdims=True), sn)
    p = jnp.where(valid, jnp.exp(sm - m), 0.0)
    pn = jnp.where(has_new, jnp.exp(sn - m), 0.0)
    inv = 1.0 / jnp.maximum(jnp.sum(p, axis=-1, keepdims=True) + pn, 1e-30)
    return p * inv, pn * inv


def _to_f_form(o8):
    x = jnp.concatenate([o8, jnp.zeros((8, D_C - 2 * KV_W), F32)], axis=1)
    row = _iota((8, D_C), 0)
    out = jnp.zeros((8, D_C), F32)
    for h in range(H_C):
        shift = (h * HEAD_DIM - (KV_W + (h // R_C) * HEAD_DIM)) % D_C
        out = jnp.where(row == h, x if shift == 0 else pltpu.roll(x, shift, 1), out)
    return jnp.where(_own_lanes(D_C), out, 0.0)


def _nsa_attn_decode_kernel(pt_ref, idx_ref, q_ref, gate_ref, nslc_ref, nwin_ref, fcmp_ref, tsel_ref,
                            twin_ref, b0_ref, win_ref, *rest, n_top, new_blk):
    del pt_ref
    pages = rest[:KV_C * n_top]
    o_ref, nw_ref = rest[KV_C * n_top:]
    b = pl.program_id(0)
    w_len = win_ref.shape[3]
    per_page = PAGE // SLC_BLOCK

    qb = jnp.broadcast_to(q_ref[0], (8, D_C))
    row5 = _iota((8, D_C), 0)
    qr = jnp.zeros((8, D_C), F32)
    for h in range(H_C):
        shift = ((h // R_C) * HEAD_DIM - h * HEAD_DIM) % D_C
        qr = jnp.where(row5 == h, qb if shift == 0 else pltpu.roll(qb, shift, 1), qr)
    lane2 = _iota((8, 2 * KV_W), 1)
    row2 = _iota((8, 2 * KV_W), 0)
    q8k_f = jnp.where(lane2 // HEAD_DIM == row2 // R_C, qr[:, :2 * KV_W], 0.0) * SCALE
    q8k = q8k_f.astype(BF16)
    b0 = b0_ref[...]

    lane1 = _iota((8, LANES), 1)
    o_groups = []
    for g in range(KV_C):
        ids = [idx_ref[(b * KV_C + g) * n_top + k] for k in range(n_top)]
        kst = jnp.concatenate([pages[g * n_top + k][0, 0] for k in range(n_top)], axis=1).astype(BF16)
        s = _dot(q8k, kst)
        bias_parts, valid_parts = [], []
        has_new = jnp.int32(0)
        for k in range(n_top):
            blk = ids[k]
            in_cache = jnp.clip(blk, 0, new_blk - 1)
            bias_parts.append(tsel_ref[in_cache // per_page])
            ok = jnp.logical_and(blk >= 0, blk < new_blk).astype(F32)
            valid_parts.append(jnp.where(lane1 // SLC_BLOCK == in_cache % per_page, ok, 0.0))
            has_new = has_new | (blk == new_blk).astype(jnp.int32)
        s = s + jnp.concatenate(bias_parts, axis=1)
        valid = jnp.concatenate(valid_parts, axis=1) > 0.5
        new_row = nslc_ref[0]
        s_new = jnp.sum(q8k_f * new_row, axis=-1, keepdims=True) + b0
        p, pn = _softmax_with_new(s, valid, s_new, has_new > 0)
        o_groups.append(_dot_nt(p.astype(BF16), kst) + pn * new_row)
    f_slc = _to_f_form(jnp.where(row2 < R_C, o_groups[0], o_groups[1]))

    win_t = win_ref[0, 0]
    new_w = nwin_ref[0]
    s = _dot(q8k, win_t.astype(BF16)) + twin_ref[...]
    s_new = jnp.sum(q8k_f * new_w, axis=-1, keepdims=True) + b0
    p, pn = _softmax_with_new(s, s == s, s_new, True)
    f_win = _to_f_form(_dot_nt(p.astype(BF16), win_t.astype(BF16)) + pn * new_w)
    sq = (2 * KV_W, 2 * KV_W)
    new_col = jnp.sum(jnp.where(_iota(sq, 0) == _iota(sq, 1), jnp.broadcast_to(new_w, sq), 0.0),
                      axis=-1, keepdims=True)
    nw_ref[0] = jnp.where(_iota((2 * KV_W, w_len), 1) == w_len - 1, new_col,
                          pltpu.roll(win_t, w_len - 1, 1))

    sig = jnp.broadcast_to(_sigmoid(gate_ref[0]), (8, LANES))
    row1 = _iota((8, LANES), 0)
    total = jnp.zeros((8, D_C), F32)
    for k, f in enumerate((fcmp_ref[0], f_slc, f_win)):
        gk = jnp.sum(jnp.where(lane1 == k * H_C + row1, sig, 0.0), axis=-1, keepdims=True)
        total += gk * f
    o_ref[0] = jnp.sum(total, axis=0, keepdims=True)


def _nsa_attn_decode(qc, gate, new_slc, new_win, fcmp, tsel, twin, b0, cache_slc_t, cache_win_t,
                     page_table, sel_idx, layer, n_top):
    db, n_pages = page_table.shape
    new_blk = n_pages * PAGE // SLC_BLOCK
    w_len = cache_win_t.shape[3]
    per_page = PAGE // SLC_BLOCK
    const = lambda shape: pl.BlockSpec(shape, lambda b, pt, si: (0,) * len(shape))
    per_b = lambda shape: pl.BlockSpec(shape, lambda b, pt, si: (b,) + (0,) * (len(shape) - 1))

    def page_spec(g, k):
        def imap(b, pt, si):
            blk = jnp.clip(si[(b * KV_C + g) * n_top + k], 0, new_blk - 1)
            return (layer, pt[b, blk // per_page], 0, 0)
        return pl.BlockSpec((1, 1, 2 * KV_W, PAGE), imap)

    grid_spec = pltpu.PrefetchScalarGridSpec(
        num_scalar_prefetch=2,
        grid=(db,),
        in_specs=[per_b((1, 1, D_C)), per_b((1, 1, LANES)), per_b((1, 1, 2 * KV_W)),
                  per_b((1, 1, 2 * KV_W)), per_b((1, 8, D_C)),
                  const(tsel.shape), const(twin.shape), const((8, 1)),
                  pl.BlockSpec((1, 1, 2 * KV_W, w_len), lambda b, pt, si: (layer, b, 0, 0))]
                 + [page_spec(g, k) for g in range(KV_C) for k in range(n_top)],
        out_specs=(per_b((1, 1, D_C)), per_b((1, 2 * KV_W, w_len))),
    )
    return pl.pallas_call(
        functools.partial(_nsa_attn_decode_kernel, n_top=n_top, new_blk=new_blk),
        out_shape=(jax.ShapeDtypeStruct((db, 1, D_C), F32),
                   jax.ShapeDtypeStruct((db, 2 * KV_W, w_len), F32)),
        grid_spec=grid_spec,
        compiler_params=_cparams(("arbitrary",)),
        name="nsa_attn_decode",
    )(page_table, sel_idx, qc, gate, new_slc, new_win, fcmp, tsel, twin, b0, cache_win_t,
      *([cache_slc_t] * (KV_C * n_top)))


def _block_diag(mats):
    n = len(mats)
    rows = []
    for a, m in enumerate(mats):
        rows.append(jnp.concatenate([m if a == c else jnp.zeros_like(m) for c in range(n)], axis=-1))
    return jnp.concatenate(rows, axis=-2)


def _nsa_weights(cmp_pe, cmp_w1, cmp_w2):
    pe2 = jnp.concatenate([cmp_pe, cmp_pe], axis=-1)
    w1bd = jnp.stack([_block_diag([cmp_w1[kv]] * KV_C) for kv in range(2)]).astype(BF16)
    zero = jnp.zeros((HEAD_DIM, 4 * HEAD_DIM), F32)
    w2rep = []
    for kv in range(2):
        rep = jnp.concatenate([cmp_w2[kv]] * 4, axis=-1)
        w2rep.append(jnp.stack([jnp.concatenate([rep, zero], axis=0),
                                jnp.concatenate([zero, rep], axis=0)]))
    w2rep = jnp.stack(w2rep).astype(BF16)
    w2heads = jnp.concatenate([w2rep[:, 0], w2rep[:, 1]], axis=-1)
    return pe2, w1bd, w2rep, w2heads


def kernel(x_prompt, x_sample, cache_sb_kv, cache_cmp_kv, cache_slc_kv, cache_win_kv, page_table,
           rel_bias, norm_mix_pre, norm_mix_post, w_in, cmp_pe, cmp_w1, cmp_w2, gmlp_ws, gmlp_b,
           norm_group_out, w_out, norm_ffn_pre, norm_ffn_post, w_ffn_up, w_ffn_down):
    depth = w_in.shape[0]
    bsz, seq, _ = x_prompt.shape
    db = x_sample.shape[0]
    n_pages = page_table.shape[1]
    past = n_pages * PAGE
    n_phys = cache_sb_kv.shape[1]
    w_len = cache_win_kv.shape[2]
    assert x_sample.shape[1] == 1 and seq % 512 == 0 and seq // SLC_BLOCK <= SLC_BLOCK
    assert n_pages % PAGES_PER_STEP == 0 and w_len == WINDOW and past >= WINDOW

    t_prompt = bsz * seq
    tm_p = 512
    n_sb_dec = -(-(past + 1) // SLC_BLOCK)
    n_pad_dec = -(-n_sb_dec // LANES) * LANES
    n_top_dec = min(TOP_N, n_sb_dec)
    n_chunks_dec = past // CMP_STRIDE

    n_q = seq // TQ_ATT
    n_diag = min(n_q, REL_MAX_DIST // TK_ATT + 2)
    tz = _bias_table(rel_bias, n_diag, TK_ATT, TQ_ATT, TK_ATT, -1, 1, 0)
    tz = tz.reshape(KV_C, R_C, n_diag, TK_ATT, TQ_ATT).transpose(2, 0, 1, 3, 4)
    bias_cmp_p = _bias_table(rel_bias, n_q, TQ_ATT, seq // CMP_STRIDE, TQ_ATT, 1, -CMP_STRIDE,
                             -(CMP_LEN - 1))
    bias_cmp_d = _bias_table(rel_bias, 1, 8, n_chunks_dec, 0, 0, -CMP_STRIDE,
                             past - (CMP_LEN - 1))[:, 0, 0, :]
    n_tsel = -(-n_pages // 8) * 8
    tsel = _bias_table(rel_bias, 1, n_tsel, PAGE, 0, -PAGE, -1, past)[:, 0]
    tsel = jnp.swapaxes(tsel, 0, 1)
    twin = _bias_table(rel_bias, 1, 8, w_len, 0, 0, -1, w_len)[:, 0, 0, :]
    b0 = rel_bias[0].reshape(H_C, 1)

    pos_minor = lambda c: c.transpose(0, 1, 3, 4, 5, 2).reshape(c.shape[0], c.shape[1], -1, c.shape[2])
    cache_sb_t = pos_minor(cache_sb_kv)
    cache_cmp_t = pos_minor(cache_cmp_kv)
    cache_slc_t = pos_minor(cache_slc_kv)
    cache_win_t = pos_minor(cache_win_kv)
    from_pos_minor = lambda a, heads: jnp.moveaxis(
        a.reshape(a.shape[:-2] + (2, heads, HEAD_DIM, a.shape[-1])), -1, -4)

    xp = x_prompt.reshape(t_prompt, D_MODEL)
    xs = x_sample.reshape(db, D_MODEL)
    outs = {k: [] for k in ("p_sb", "p_cmp", "p_slc", "p_win", "s_sb", "s_cmp", "s_slc", "s_win", "s_gv")}
    row = lambda v: v.reshape(1, -1)

    for l in range(depth):
        w_in_l = jnp.pad(w_in[l], ((0, 0), (0, NP_IN - N_IN))).astype(BF16)
        w_out_l = w_out[l].astype(BF16)
        w_up_l = w_ffn_up[l].astype(BF16)
        w_down_l = w_ffn_down[l].astype(BF16)
        pe2, w1bd, w2rep, w2heads = _nsa_weights(cmp_pe[l], cmp_w1[l], cmp_w2[l])
        g_pre, g_post = row(norm_mix_pre[l]), row(norm_mix_post[l])
        g_grp = row(norm_group_out[l])
        gf_pre, gf_post = row(norm_ffn_pre[l]), row(norm_ffn_post[l])

        z = _project(xp, g_pre, w_in_l, tm_p)
        z3 = z.reshape(bsz, seq, NP_IN)
        o_a = _sb_prompt(z3)
        bias_rows = jnp.repeat(gmlp_b[l].T, CB, axis=1)
        o_b = _gmlp_prompt(z3, gmlp_ws[l], bias_rows)
        kvc = _compress_prompt(z3, pe2, w1bd, w2rep)
        o_cmp, sel_t = _nsa_cmp_prompt(z3, kvc, bias_cmp_p)
        slc_t = jnp.swapaxes(z3[:, :, C_SK:C_SK + 2 * KV_W], 1, 2)
        win_t = jnp.swapaxes(z3[:, :, C_WK:C_WK + 2 * KV_W], 1, 2)
        o_c = _nsa_attn_prompt(z3, slc_t, win_t, sel_t, tz, o_cmp)
        xp = _mix_out(o_a.reshape(t_prompt, D_A), o_b.reshape(t_prompt, D_B),
                      o_c.reshape(t_prompt, D_C), xp, g_grp, w_out_l, g_post, tm_p)
        xp = _ffn(xp, gf_pre, w_up_l, w_down_l, gf_post, tm_p, D_FF // 2)
        outs["p_sb"].append(z3[:, :, C_KA:C_KA + 2 * D_A].reshape(bsz, seq, 2, H_A, HEAD_DIM))
        kv_shape = (bsz, seq, 2, KV_C, HEAD_DIM)
        outs["p_cmp"].append(z3[:, :, C_CK:C_CK + 2 * KV_W].reshape(kv_shape))
        outs["p_slc"].append(from_pos_minor(slc_t, KV_C))
        n_win = min(WINDOW, seq)
        outs["p_win"].append(from_pos_minor(win_t[:, :, seq - n_win:], KV_C))

        zs = _project(xs, g_pre, w_in_l, db)
        zs3 = zs.reshape(db, 1, NP_IN)
        o_a_s = _sb_decode(zs3[:, :, C_QA:C_QA + D_A], cache_sb_t, page_table, l)
        w_row = jnp.repeat(gmlp_ws[l][:, 0, 0], CB).reshape(1, D_B)
        b_row = jnp.repeat(gmlp_b[l][:, 0], CB).reshape(1, D_B)
        o_b_s, vn_s = _gmlp_sample(zs[:, C_UB:C_UB + D_B], zs[:, C_VB:C_VB + D_B], w_row, b_row)
        qc_s = zs3[:, :, C_QC:C_QC + D_C]
        f_cmp, imp = _nsa_cmp_decode(qc_s, cache_cmp_t, page_table, l, pe2, w1bd, w2heads,
                                     bias_cmp_d, n_pad_dec)
        idx = _topk_decode(imp[:, :KV_C, :].reshape(db * KV_C, n_pad_dec), past // SLC_BLOCK, n_sb_dec)
        sel_idx = idx[:, :n_top_dec].reshape(-1)
        gate_s = zs3[:, :, C_GATE:C_GATE + LANES]
        new_slc = zs3[:, :, C_SK:C_SK + 2 * KV_W]
        new_win = zs3[:, :, C_WK:C_WK + 2 * KV_W]
        o_c_s, win_out = _nsa_attn_decode(qc_s, gate_s, new_slc, new_win, f_cmp, tsel, twin, b0,
                                          cache_slc_t, cache_win_t, page_table, sel_idx, l, n_top_dec)
        xs = _mix_out(o_a_s.reshape(db, D_A), o_b_s, o_c_s.reshape(db, D_C), xs, g_grp, w_out_l,
                      g_post, db)
        xs = _ffn(xs, gf_pre, w_up_l, w_down_l, gf_post, db, D_FF // 2)
        outs["s_sb"].append(zs[:, C_KA:C_KA + 2 * D_A].reshape(db, 1, 2, H_A, HEAD_DIM))
        outs["s_cmp"].append(zs[:, C_CK:C_CK + 2 * KV_W].reshape(db, 1, 2, KV_C, HEAD_DIM))
        outs["s_slc"].append(zs[:, C_SK:C_SK + 2 * KV_W].reshape(db, 1, 2, KV_C, HEAD_DIM))
        outs["s_win"].append(from_pos_minor(win_out, KV_C))
        outs["s_gv"].append(vn_s.reshape(db, 1, D_B))

    st = lambda k: jnp.stack(outs[k])
    return (xp.reshape(bsz, seq, D_MODEL), xs.reshape(db, 1, D_MODEL), st("p_sb"), st("p_cmp"),
            st("p_slc"), st("p_win"), st("s_sb"), st("s_cmp"), st("s_slc"), st("s_win"), st("s_gv"))
```

```python
import functools
import math

import jax
import jax.numpy as jnp
import numpy as np
from jax import lax
from jax.experimental import pallas as pl
from jax.experimental.pallas import tpu as pltpu

F32 = jnp.float32
BF16 = jnp.bfloat16

D_MODEL = 1024
HEAD_DIM = 64
D_A = 256
H_A = 4
D_B = 256
G_B = 4
CB = 64
D_C = 512
H_C = 8
KV_C = 2
R_C = 4
KV_W = 128
N_IN = 3 * D_A + 2 * D_B + D_C + 6 * KV_W + 3 * H_C
NP_IN = 2688
CHUNK = 128
CMP_LEN = 32
CMP_STRIDE = 16
SLC_BLOCK = 64
TOP_N = 16
N_LOCAL = 2
WINDOW = 512
NUM_BUCKETS = 32
REL_MAX_DIST = 2048
D_FF = 2816
EPS = 1e-6
SCALE = HEAD_DIM ** -0.5
PAGE = 128

C_QA, C_KA, C_VA, C_UB, C_VB, C_QC = 0, 256, 512, 768, 1024, 1280
C_CK, C_CV, C_SK, C_SV, C_WK, C_WV, C_GATE = 1792, 1920, 2048, 2176, 2304, 2432, 2560

LANES = 128
TQ_ATT = 128
TK_ATT = 128
TQ_NSA = 256
TK_NSA = 512
VMEM_LIMIT = 56 * 1024 * 1024
NEG = -1e30
DECAYED = -110.0
BIG_SCORE = 3e38
PAGES_PER_STEP = 8


def _cparams(sem):
    return pltpu.CompilerParams(dimension_semantics=sem, vmem_limit_bytes=VMEM_LIMIT)


def _dot(a, b):
    return jnp.dot(a, b, preferred_element_type=F32)


def _dot_nt(a, b):
    return lax.dot_general(a, b, (((1,), (1,)), ((), ())), preferred_element_type=F32)


def _rms(x, g):
    return x * lax.rsqrt(jnp.mean(x * x, axis=-1, keepdims=True) + EPS) * g


def _sigmoid(x):
    return 1.0 / (1.0 + jnp.exp(-x))


def _gelu_tanh(x):
    return 0.5 * x * (1.0 + jnp.tanh(math.sqrt(2.0 / math.pi) * (x + 0.044715 * (x * x * x))))


def _iota(shape, dim):
    return lax.broadcasted_iota(jnp.int32, shape, dim)


def _bucket_thresholds():
    n = np.arange(0, 1 << 15)
    exact = NUM_BUCKETS // 2
    nf = np.maximum(n, 1).astype(np.float32)
    big = exact + (np.log(nf / np.float32(exact)) / np.float32(math.log(REL_MAX_DIST / exact))
                   * np.float32(NUM_BUCKETS - exact)).astype(np.int32)
    bucket = np.where(n < exact, n, np.minimum(big, NUM_BUCKETS - 1))
    assert np.all(np.diff(bucket) >= 0)
    return [int(np.argmax(bucket >= k)) for k in range(1, NUM_BUCKETS)]


_THR = _bucket_thresholds()


def _bias_table_kernel(tab_ref, o_ref, *, a, rs, cs, c0):
    shape = o_ref.shape[2:]
    dist = a * pl.program_id(0) + rs * _iota(shape, 0) + cs * _iota(shape, 1) + c0
    outs = [jnp.full(shape, tab_ref[h], F32) for h in range(H_C)]
    for k in range(1, NUM_BUCKETS):
        ge = dist >= _THR[k - 1]
        for h in range(H_C):
            outs[h] = jnp.where(ge, tab_ref[k * H_C + h], outs[h])
    for h in range(H_C):
        o_ref[h, 0] = outs[h]


def _bias_table(rel_bias, steps, rows, cols, a, rs, cs, c0):
    return pl.pallas_call(
        functools.partial(_bias_table_kernel, a=a, rs=rs, cs=cs, c0=c0),
        out_shape=jax.ShapeDtypeStruct((H_C, steps, rows, cols), F32),
        grid=(steps,),
        in_specs=[pl.BlockSpec(memory_space=pltpu.SMEM)],
        out_specs=pl.BlockSpec((H_C, 1, rows, cols), lambda i: (0, i, 0, 0)),
        compiler_params=_cparams(("arbitrary",)),
        name="bias_table",
    )(rel_bias.reshape(-1))


def _proj_kernel(x_ref, g_ref, w_ref, o_ref):
    h = _rms(x_ref[...], g_ref[...])
    o_ref[...] = _dot(h.astype(BF16), w_ref[...])


def _project(x, g, w, tm):
    t = x.shape[0]
    return pl.pallas_call(
        _proj_kernel,
        out_shape=jax.ShapeDtypeStruct((t, NP_IN), F32),
        grid=(t // tm,),
        in_specs=[pl.BlockSpec((tm, D_MODEL), lambda i: (i, 0)),
                  pl.BlockSpec((1, D_MODEL), lambda i: (0, 0)),
                  pl.BlockSpec((D_MODEL, NP_IN), lambda i: (0, 0))],
        out_specs=pl.BlockSpec((tm, NP_IN), lambda i: (i, 0)),
        compiler_params=_cparams(("parallel",)),
        name="in_proj",
    )(x, g, w)


def _mixout_kernel(oa_ref, ob_ref, oc_ref, x_ref, gg_ref, w_ref, gp_ref, o_ref):
    gg = gg_ref[...]
    a = _rms(oa_ref[...], gg[:, :D_A]).astype(BF16)
    b = _rms(ob_ref[...], gg[:, D_A:D_A + D_B]).astype(BF16)
    c = _rms(oc_ref[...], gg[:, D_A + D_B:]).astype(BF16)
    y = (_dot(a, w_ref[0:D_A, :]) + _dot(b, w_ref[D_A:D_A + D_B, :])
         + _dot(c, w_ref[D_A + D_B:, :]))
    o_ref[...] = x_ref[...] + _rms(y, gp_ref[...])


def _mix_out(oa, ob, oc, x, gg, w, gp, tm):
    t = x.shape[0]
    row = lambda width: pl.BlockSpec((tm, width), lambda i: (i, 0))
    full = lambda r, c: pl.BlockSpec((r, c), lambda i: (0, 0))
    return pl.pallas_call(
        _mixout_kernel,
        out_shape=jax.ShapeDtypeStruct((t, D_MODEL), F32),
        grid=(t // tm,),
        in_specs=[row(D_A), row(D_B), row(D_C), row(D_MODEL), full(1, D_MODEL),
                  full(D_MODEL, D_MODEL), full(1, D_MODEL)],
        out_specs=row(D_MODEL),
        compiler_params=_cparams(("parallel",)),
        name="mix_out",
    )(oa, ob, oc, x, gg, w, gp)


def _ffn_kernel(x_ref, gpre_ref, wg_ref, wu_ref, wd_ref, gpost_ref, o_ref, h_ref, acc_ref):
    j = pl.program_id(1)

    @pl.when(j == 0)
    def _():
        h_ref[...] = _rms(x_ref[...], gpre_ref[...]).astype(BF16)
        acc_ref[...] = jnp.zeros_like(acc_ref)

    h = h_ref[...]
    g = _dot(h, wg_ref[...])
    u = _dot(h, wu_ref[...])
    act = (g * _sigmoid(g)) * u
    acc_ref[...] += _dot(act.astype(BF16), wd_ref[...])

    @pl.when(j == pl.num_programs(1) - 1)
    def _():
        o_ref[...] = x_ref[...] + _rms(acc_ref[...], gpost_ref[...])


def _ffn(x, gpre, w_up, w_down, gpost, tm, tf):
    t = x.shape[0]
    nf = D_FF // tf
    return pl.pallas_call(
        _ffn_kernel,
        out_shape=jax.ShapeDtypeStruct((t, D_MODEL), F32),
        grid=(t // tm, nf),
        in_specs=[pl.BlockSpec((tm, D_MODEL), lambda i, j: (i, 0)),
                  pl.BlockSpec((1, D_MODEL), lambda i, j: (0, 0)),
                  pl.BlockSpec((D_MODEL, tf), lambda i, j: (0, j)),
                  pl.BlockSpec((D_MODEL, tf), lambda i, j: (0, j + nf)),
                  pl.BlockSpec((tf, D_MODEL), lambda i, j: (j, 0)),
                  pl.BlockSpec((1, D_MODEL), lambda i, j: (0, 0))],
        out_specs=pl.BlockSpec((tm, D_MODEL), lambda i, j: (i, 0)),
        scratch_shapes=[pltpu.VMEM((tm, D_MODEL), BF16), pltpu.VMEM((tm, D_MODEL), F32)],
        compiler_params=_cparams(("parallel", "arbitrary")),
        name="ffn",
    )(x, gpre, w_up, w_up, w_down, gpost)


def _log_sig_pair(z):
    soft = jnp.log(1.0 + jnp.exp(-jnp.abs(z)))
    log_beta = jnp.minimum(z, 0.0) - soft
    return log_beta, log_beta - z


def _times_01_matrix(x, mat_bf16, pieces=3):
    out = None
    rem = x
    for k in range(pieces):
        part = rem.astype(BF16)
        term = _dot(part, mat_bf16)
        out = term if out is None else out + term
        if k + 1 < pieces:
            rem = rem - part.astype(F32)
    return out


def _suffix_sum_exclusive(x, upper_bf16):
    return _times_01_matrix(x, upper_bf16, pieces=2)


def _strict_upper(n):
    return jnp.where(_iota((n, n), 0) > _iota((n, n), 1), 1.0, 0.0).astype(BF16)


def _sb_prompt_kernel(q_ref, k_ref, v_ref, o_ref, run_ref, acc_ref, *, tq, tk):
    i = pl.program_id(1)
    lane_head = _iota((tq, D_A), 1) // HEAD_DIM
    qs = q_ref[0] * SCALE
    qh = [jnp.where(lane_head == h, qs, 0.0).astype(BF16) for h in range(H_A)]
    upper = _strict_upper(tk)
    run_ref[...] = jnp.zeros_like(run_ref)
    acc_ref[...] = jnp.zeros_like(acc_ref)

    def sweep_tile(j, diagonal):
        ks = pl.multiple_of(j * tk, tk)
        kt = k_ref[0, pl.ds(ks, tk), :].astype(BF16)
        vt = v_ref[0, pl.ds(ks, tk), :].astype(BF16)
        mask = _iota((tq, tk), 1) < _iota((tq, tk), 0)
        heads = range(H_A)
        z = [_dot_nt(qh[h], kt) for h in heads]
        pairs = [_log_sig_pair(z[h]) for h in heads]
        lb = [pairs[h][0] for h in heads]
        lk = [jnp.where(mask, pairs[h][1], 0.0) if diagonal else pairs[h][1] for h in heads]
        after = [_suffix_sum_exclusive(lk[h], upper) + run_ref[h] for h in heads]
        a = [jnp.exp(lb[h] + after[h]) for h in heads]
        if diagonal:
            a = [jnp.where(mask, a[h], 0.0) for h in heads]
        pv = [_dot(a[h].astype(BF16), vt) for h in heads]
        worst = jnp.float32(-jnp.inf)
        for h in heads:
            acc_ref[h] += pv[h]
            new_run = after[h][:, 0:1] + lk[h][:, 0:1]
            run_ref[h] = new_run
            worst = jnp.maximum(worst, jnp.max(new_run))
        return (worst > DECAYED).astype(jnp.int32)

    def cond(c):
        j, alive = c
        return jnp.logical_and(j >= 0, alive > 0)

    def body(c):
        return c[0] - 1, sweep_tile(c[0], False)

    assert tq == tk
    lax.while_loop(cond, body, (i - 1, sweep_tile(i, True)))
    out = acc_ref[0]
    for h in range(1, H_A):
        out = jnp.where(lane_head == h, acc_ref[h], out)
    o_ref[0] = out


def _sb_prompt(z3, tq=256, tk=256):
    b, s, _ = z3.shape
    return pl.pallas_call(
        functools.partial(_sb_prompt_kernel, tq=tq, tk=tk),
        out_shape=jax.ShapeDtypeStruct((b, s, D_A), F32),
        grid=(b, s // tq),
        in_specs=[pl.BlockSpec((1, tq, D_A), lambda bi, i: (bi, i, C_QA // D_A)),
                  pl.BlockSpec((1, s, D_A), lambda bi, i: (bi, 0, C_KA // D_A)),
                  pl.BlockSpec((1, s, D_A), lambda bi, i: (bi, 0, C_VA // D_A))],
        out_specs=pl.BlockSpec((1, tq, D_A), lambda bi, i: (bi, i, 0)),
        scratch_shapes=[pltpu.VMEM((H_A, tq, 1), F32), pltpu.VMEM((H_A, tq, D_A), F32)],
        compiler_params=_cparams(("parallel", "arbitrary")),
        name="sb_prompt",
    )(z3, z3, z3)


def _head_rows(row_vec, width):
    full = jnp.broadcast_to(row_vec, (8, width))
    return jnp.where(_iota((8, width), 1) // HEAD_DIM == _iota((8, width), 0), full, 0.0)


def _sb_decode_kernel(pt_ref, alive_ref, q_ref, run_in_ref, acc_in_ref, *rest, n_pages_step, n_steps):
    del pt_ref
    pages = rest[:n_pages_step]
    run_ref, acc_ref, o_ref, live_ref = rest[n_pages_step:]
    b = pl.program_id(0)
    step = pl.program_id(1)

    @pl.when(step == 0)
    def _():
        run_ref[...] = run_in_ref[...]
        acc_ref[...] = acc_in_ref[...]
        live_ref[0] = alive_ref[b]

    @pl.when(live_ref[0] > 0)
    def _():
        q8 = (_head_rows(q_ref[0], D_A) * SCALE).astype(BF16)
        upper = _strict_upper(PAGE)
        run = run_ref[0][:, 0:1]
        acc = acc_ref[0]
        for page in pages:
            kt = page[0, 0, 0:D_A, :].astype(BF16)
            vt = page[0, 0, D_A:2 * D_A, :].astype(BF16)
            z = _dot(q8, kt)
            lb, lk = _log_sig_pair(z)
            after = _suffix_sum_exclusive(lk, upper) + run
            a = jnp.exp(lb + after)
            acc = acc + _dot_nt(a.astype(BF16), vt)
            run = after[:, 0:1] + lk[:, 0:1]
        run_ref[0] = jnp.broadcast_to(run, (8, LANES))
        acc_ref[0] = acc
        head_rows = _iota((8, 1), 0) < H_A
        live_ref[0] = (jnp.max(jnp.where(head_rows, run, -jnp.inf)) > DECAYED).astype(jnp.int32)

    @pl.when(step == n_steps - 1)
    def _():
        o_ref[0] = jnp.sum(jnp.where(_own_lanes(D_A), acc_ref[0], 0.0), axis=0, keepdims=True)


def _sb_decode_phase(qa, cache_t, page_table, layer, alive, run, acc, first_back, n_pages_step,
                     n_steps):
    db, n_pages = page_table.shape

    def page_spec(k):
        def imap(b, s, pt, al):
            page = pt[b, n_pages - 1 - first_back - (s * n_pages_step + k)]
            return (layer, jnp.where(al[b] > 0, page, 0), 0, 0)
        return pl.BlockSpec((1, 1, 2 * D_A, PAGE), imap)

    per_b = lambda shape: pl.BlockSpec(shape, lambda b, s, pt, al: (b,) + (0,) * (len(shape) - 1))
    grid_spec = pltpu.PrefetchScalarGridSpec(
        num_scalar_prefetch=2,
        grid=(db, n_steps),
        in_specs=[per_b((1, 1, D_A)), per_b((1, 8, LANES)), per_b((1, 8, D_A))]
                 + [page_spec(k) for k in range(n_pages_step)],
        out_specs=(per_b((1, 8, LANES)), per_b((1, 8, D_A)), per_b((1, 1, D_A))),
        scratch_shapes=[pltpu.SMEM((1,), jnp.int32)],
    )
    return pl.pallas_call(
        functools.partial(_sb_decode_kernel, n_pages_step=n_pages_step, n_steps=n_steps),
        out_shape=(jax.ShapeDtypeStruct((db, 8, LANES), F32), jax.ShapeDtypeStruct((db, 8, D_A), F32),
                   jax.ShapeDtypeStruct((db, 1, D_A), F32)),
        grid_spec=grid_spec,
        compiler_params=_cparams(("arbitrary", "arbitrary")),
        name="sb_decode",
    )(page_table, alive, qa, run, acc, *([cache_t] * n_pages_step))


SB_FIRST_PAGES = 4


def _sb_decode(qa, cache_t, page_table, layer):
    db, n_pages = page_table.shape
    first = min(SB_FIRST_PAGES, n_pages)
    ones = jnp.ones((db,), jnp.int32)
    run0 = jnp.zeros((db, 8, LANES), F32)
    acc0 = jnp.zeros((db, 8, D_A), F32)
    run1, acc1, o1 = _sb_decode_phase(qa, cache_t, page_table, layer, ones, run0, acc0, 0, first, 1)
    rest = n_pages - first
    if rest == 0:
        return o1
    per_step = max(d for d in range(1, 13) if rest % d == 0)
    alive = (jnp.max(run1[:, :H_A, 0], axis=1) > DECAYED).astype(jnp.int32)
    return lax.cond(
        jnp.any(alive > 0),
        lambda: _sb_decode_phase(qa, cache_t, page_table, layer, alive, run1, acc1, first, per_step,
                                 rest // per_step)[2],
        lambda: o1)


def _layernorm(x):
    xc = x - jnp.mean(x, axis=-1, keepdims=True)
    return xc * lax.rsqrt(jnp.mean(xc * xc, axis=-1, keepdims=True) + EPS)


def _gmlp_prompt_kernel(u_ref, v_ref, w_ref, b_ref, o_ref, *, n_chunks):
    tril = _iota((CHUNK, CHUNK), 0) >= _iota((CHUNK, CHUNK), 1)
    ws = [jnp.where(tril, w_ref[g], 0.0).astype(BF16) for g in range(G_B)]
    group = _iota((CHUNK, D_B), 1) // CB
    for c in range(n_chunks):
        rows = slice(c * CHUNK, (c + 1) * CHUNK)
        vn = _layernorm(v_ref[0, rows, :]).astype(BF16)
        mixed = b_ref[...]
        for g in range(G_B):
            mixed = mixed + jnp.where(group == g, _dot(ws[g], vn), 0.0)
        o_ref[0, rows, :] = u_ref[0, rows, :] * mixed


def _gmlp_prompt(z3, ws, bias_rows, n_chunks=4):
    b, s, _ = z3.shape
    tr = n_chunks * CHUNK
    return pl.pallas_call(
        functools.partial(_gmlp_prompt_kernel, n_chunks=n_chunks),
        out_shape=jax.ShapeDtypeStruct((b, s, D_B), F32),
        grid=(b, s // tr),
        in_specs=[pl.BlockSpec((1, tr, D_B), lambda bi, i: (bi, i, C_UB // D_B)),
                  pl.BlockSpec((1, tr, D_B), lambda bi, i: (bi, i, C_VB // D_B)),
                  pl.BlockSpec((G_B, CHUNK, CHUNK), lambda bi, i: (0, 0, 0)),
                  pl.BlockSpec((CHUNK, D_B), lambda bi, i: (0, 0))],
        out_specs=pl.BlockSpec((1, tr, D_B), lambda bi, i: (bi, i, 0)),
        compiler_params=_cparams(("parallel", "parallel")),
        name="gmlp_prompt",
    )(z3, z3, ws, bias_rows)


def _gmlp_sample_kernel(u_ref, v_ref, w_ref, b_ref, o_ref, vn_ref):
    vn = _layernorm(v_ref[...])
    vn_ref[...] = vn
    o_ref[...] = u_ref[...] * (w_ref[...] * vn + b_ref[...])


def _gmlp_sample(u, v, w_row, b_row):
    return pl.pallas_call(
        _gmlp_sample_kernel,
        out_shape=(jax.ShapeDtypeStruct(u.shape, F32), jax.ShapeDtypeStruct(u.shape, F32)),
        name="gmlp_sample",
    )(u, v, w_row, b_row)


def _compress_hidden(load_rows, pe_ref, w1_fn, n_chunks, width):
    hid_a = jnp.zeros((n_chunks, width), F32)
    hid_b = jnp.zeros((n_chunks, width), F32)
    for j in range(CMP_STRIDE):
        x = load_rows(j)
        hid_a += _dot((x + pe_ref[j:j + 1, :]).astype(BF16), w1_fn(j))
        hid_b += _dot((x + pe_ref[CMP_STRIDE + j:CMP_STRIDE + j + 1, :]).astype(BF16),
                      w1_fn(CMP_STRIDE + j))
    return hid_a + pltpu.roll(hid_b, n_chunks - 1, 0)


def _compress_prompt_kernel(z_ref, pe_ref, w1_ref, w2_ref, o_ref, *, n_chunks):
    hid = _compress_hidden(lambda j: z_ref[0, pl.ds(j, n_chunks, stride=CMP_STRIDE), :],
                           pe_ref.at[0], lambda j: w1_ref[0, j], n_chunks, KV_W)
    act = _gelu_tanh(hid).astype(BF16)
    valid = _iota((n_chunks, 4 * HEAD_DIM), 0) < n_chunks - 1
    for g in range(KV_C):
        o_ref[0, 0, g] = jnp.where(valid, _dot(act, w2_ref[0, g]), 0.0)


def _compress_prompt(z3, pe2, w1bd, w2rep):
    b, s, _ = z3.shape
    n_chunks = s // CMP_STRIDE
    return pl.pallas_call(
        functools.partial(_compress_prompt_kernel, n_chunks=n_chunks),
        out_shape=jax.ShapeDtypeStruct((b, 2, KV_C, n_chunks, 4 * HEAD_DIM), F32),
        grid=(b, 2),
        in_specs=[pl.BlockSpec((1, s, KV_W), lambda bi, kv: (bi, 0, C_CK // KV_W + kv)),
                  pl.BlockSpec((1, CMP_LEN, KV_W), lambda bi, kv: (kv, 0, 0)),
                  pl.BlockSpec((1, CMP_LEN, KV_W, KV_W), lambda bi, kv: (kv, 0, 0, 0)),
                  pl.BlockSpec((1, KV_C, KV_W, 4 * HEAD_DIM), lambda bi, kv: (kv, 0, 0, 0))],
        out_specs=pl.BlockSpec((1, 1, KV_C, n_chunks, 4 * HEAD_DIM),
                               lambda bi, kv: (bi, kv, 0, 0, 0)),
        compiler_params=_cparams(("parallel", "parallel")),
        name="nsa_compress_prompt",
    )(z3, pe2, w1bd, w2rep)


def _select_blocks_t(score, causal, n_top):
    n_sb = score.shape[0]
    jt = _iota(score.shape, 0)
    rank = jnp.zeros(score.shape, F32)
    for jp in range(n_sb):
        row = score[jp:jp + 1, :]
        ge = jnp.where(row >= score, 1.0, 0.0)
        gt = jnp.where(row > score, 1.0, 0.0)
        rank += jnp.where(jt > jp, ge, gt)
    return jnp.where(causal, jnp.where(rank < n_top, 1.0, 0.0), 0.0)


def _nsa_cmp_prompt_kernel(q0_ref, q1_ref, kvc_ref, bias_ref, ocmp_ref, sel_ref, *, n_sb, n_cb):
    i = pl.program_id(1)
    tq = TQ_ATT
    n_pad = kvc_ref.shape[3]
    lane_head = _iota((tq, 4 * HEAD_DIM), 1) // HEAD_DIM
    n_idx = _iota((tq, n_pad), 1)
    dist = (i * tq + _iota((tq, n_pad), 0)) - (n_idx * CMP_STRIDE + (CMP_LEN - 1))
    mask = jnp.logical_and(dist >= 0, n_idx < n_cb)
    c0 = _iota((n_sb, n_pad), 1) * CMP_STRIDE
    s0 = _iota((n_sb, n_pad), 0) * SLC_BLOCK
    ov_t = jnp.where(jnp.logical_and(c0 < s0 + SLC_BLOCK, c0 + CMP_LEN > s0), 1.0, 0.0).astype(BF16)
    t_row = i * tq + _iota((n_sb, tq), 1)
    cur = t_row // SLC_BLOCK
    jt = _iota((n_sb, tq), 0)
    causal = jt <= cur
    forced = jnp.logical_or(jt == 0, jt > cur - N_LOCAL)
    s_all = []
    for g, q_ref in enumerate((q0_ref, q1_ref)):
        qs = q_ref[0] * SCALE
        qm = jnp.concatenate([jnp.where(lane_head == r, qs, 0.0) for r in range(R_C)],
                             axis=0).astype(BF16)
        s_all.append(_dot_nt(qm, kvc_ref[0, 0, g].astype(BF16)))
    probs = []
    for h in range(H_C):
        g, r = h // R_C, h % R_C
        sm = jnp.where(mask, s_all[g][r * tq:(r + 1) * tq] + bias_ref[h, 0], NEG)
        m = jnp.max(sm, axis=-1, keepdims=True)
        p = jnp.where(mask, jnp.exp(sm - m), 0.0)
        p = p / jnp.maximum(jnp.sum(p, axis=-1, keepdims=True), 1e-30)
        probs.append(p.astype(BF16))
    sel_t = []
    for g in range(KV_C):
        vc = kvc_ref[0, 1, g].astype(BF16)
        o_acc = jnp.zeros((tq, 4 * HEAD_DIM), F32)
        imp_t = jnp.zeros((n_sb, tq), F32)
        for r in range(R_C):
            pb = probs[R_C * g + r]
            o_acc += jnp.where(lane_head == r, _dot(pb, vc), 0.0)
            imp_t += _dot_nt(ov_t, pb)
        ocmp_ref[0, :, g * 4 * HEAD_DIM:(g + 1) * 4 * HEAD_DIM] = o_acc
        score = jnp.where(causal, jnp.where(forced, BIG_SCORE, imp_t), -1.0)
        sel_t.append(_select_blocks_t(score, causal, min(TOP_N, n_sb)))
    pad = jnp.zeros((SLC_BLOCK - n_sb, tq), F32)
    sel_ref[0] = jnp.concatenate([sel_t[0], pad, sel_t[1], pad] if n_sb < SLC_BLOCK else sel_t,
                                 axis=0)


def _nsa_cmp_prompt(z3, kvc, bias_cmp):
    b, s, _ = z3.shape
    n_sb = s // SLC_BLOCK
    n_cb = (s - CMP_LEN) // CMP_STRIDE + 1
    n_pad = kvc.shape[3]
    wq = 4 * HEAD_DIM
    return pl.pallas_call(
        functools.partial(_nsa_cmp_prompt_kernel, n_sb=n_sb, n_cb=n_cb),
        out_shape=(jax.ShapeDtypeStruct((b, s, D_C), F32),
                   jax.ShapeDtypeStruct((b, 2 * SLC_BLOCK, s), F32)),
        grid=(b, s // TQ_ATT),
        in_specs=[pl.BlockSpec((1, TQ_ATT, wq), lambda bi, i: (bi, i, C_QC // wq)),
                  pl.BlockSpec((1, TQ_ATT, wq), lambda bi, i: (bi, i, C_QC // wq + 1)),
                  pl.BlockSpec((1, 2, KV_C, n_pad, wq), lambda bi, i: (bi, 0, 0, 0, 0)),
                  pl.BlockSpec((H_C, 1, TQ_ATT, n_pad), lambda bi, i: (0, i, 0, 0))],
        out_specs=(pl.BlockSpec((1, TQ_ATT, D_C), lambda bi, i: (bi, i, 0)),
                   pl.BlockSpec((1, 2 * SLC_BLOCK, TQ_ATT), lambda bi, i: (bi, 0, i))),
        compiler_params=_cparams(("parallel", "parallel")),
        name="nsa_cmp_prompt",
    )(z3, z3, kvc, bias_cmp)


def _score_tile(args, slot, s_ref, peak_ref, tz_ref, n_diag):
    raw, valid_f, d0 = args
    tk, tb = raw.shape[0], TK_ATT
    nq = valid_f.shape[1] // tb
    tabs = {e: tz_ref[jnp.clip(d0 + e, 0, n_diag - 1), 0] for e in range(-(tk // tb - 1), nq)}
    bias = jnp.concatenate(
        [jnp.concatenate([tabs[cq - ck][r] for r in range(R_C) for cq in range(nq)], axis=1)
         for ck in range(tk // tb)], axis=0)
    valid = jnp.concatenate([valid_f] * R_C, axis=1) > 0.5
    sm = jnp.where(valid, raw + bias, NEG)
    s_ref[slot] = sm
    peak_ref[slot] = jnp.broadcast_to(jnp.max(sm, axis=0, keepdims=True), peak_ref.shape[1:])


def _absorb_tile(vt, slot, s_ref, peak_ref, m_ref, l_ref, acc_ref):
    sm = s_ref[slot]
    m_prev = m_ref[...]
    m_new = jnp.maximum(m_prev, peak_ref[slot][0:1])
    alpha = jnp.exp(m_prev - m_new)
    p = jnp.exp((sm - m_new).astype(BF16))
    pv = _dot(jnp.concatenate([vt, jnp.ones((16, vt.shape[1]), BF16)], axis=0), p)
    l_ref[...] = alpha * l_ref[...] + pv[HEAD_DIM:HEAD_DIM + 1]
    acc_ref[...] = alpha * acc_ref[...] + pv[:HEAD_DIM]
    m_ref[...] = m_new


def _nsa_attn_prompt_kernel(q_ref, sk_ref, svt_ref, wk_ref, wvt_ref, selt_ref, tz_ref, gate_ref,
                            ocmp_ref, o_ref, s_ref, peak_ref, m_ref, l_ref, acc_ref, *, n_diag):
    g = pl.program_id(1)
    i = pl.program_id(2)
    tq, tk, tb = TQ_NSA, TK_NSA, TK_ATT
    wq = 4 * HEAD_DIM
    lane = _iota((tq, LANES), 1)
    own_half = (lane // HEAD_DIM) == g
    q = q_ref[0]
    cols = []
    for r in range(R_C):
        a = pltpu.roll(q, (wq - r * HEAD_DIM) % wq, 1)[:, :LANES]
        both = jnp.where(g == 0, a, pltpu.roll(a, HEAD_DIM, 1))
        cols.append((jnp.where(own_half, both, 0.0) * SCALE).T)
    qt = jnp.concatenate(cols, axis=1).astype(BF16)
    selt = selt_ref[0].astype(BF16)
    qpos = i * tq + _iota((tk, tq), 1)
    key_in_tile = _iota((tk, tq), 0)
    own_rows = pl.multiple_of(g * HEAD_DIM, HEAD_DIM)

    def reset():
        m_ref[...] = jnp.full(m_ref.shape, NEG, F32)
        l_ref[...] = jnp.zeros_like(l_ref)
        acc_ref[...] = jnp.zeros_like(acc_ref)

    def slc_products(jt):
        ks = pl.multiple_of(jt * tk, tk)
        kt = sk_ref[0, pl.ds(ks, tk), :].astype(BF16)
        target = (g * SLC_BLOCK + (tk // SLC_BLOCK) * jt + (_iota((tk, LANES), 0) // SLC_BLOCK))
        expand = jnp.where(_iota((tk, LANES), 1) == target, 1.0, 0.0).astype(BF16)
        chosen = _dot(expand, selt)
        valid_f = jnp.where(ks + key_in_tile <= qpos, chosen, 0.0)
        return _dot(kt, qt), valid_f, (i * tq - ks) // tb

    def win_products(jt):
        ks = pl.multiple_of(jt * tk, tk)
        kt = wk_ref[0, pl.ds(ks, tk), :].astype(BF16)
        dist = qpos - (ks + key_in_tile)
        valid_f = jnp.where(jnp.logical_and(dist >= 0, dist <= WINDOW), 1.0, 0.0)
        return _dot(kt, qt), valid_f, (i * tq - ks) // tb

    def sweep(products, values_ref, first, last):
        def absorb(jt):
            vt = values_ref[0, jt, pl.ds(own_rows, HEAD_DIM), :].astype(BF16)
            _absorb_tile(vt, jt % 2, s_ref, peak_ref, m_ref, l_ref, acc_ref)

        def body(jt, carry):
            started = products(jt)
            absorb(jt - 1)
            _score_tile(started, jt % 2, s_ref, peak_ref, tz_ref, n_diag)
            return carry

        _score_tile(products(first), first % 2, s_ref, peak_ref, tz_ref, n_diag)
        lax.fori_loop(first + 1, last + 1, body, 0)
        absorb(last)

    def normalised_out():
        acc_t = acc_ref[...] / l_ref[...]
        out = jnp.zeros((tq, wq), F32)
        pad_rows = jnp.zeros((LANES - HEAD_DIM, tq), F32)
        for r in range(R_C):
            x = jnp.concatenate([acc_t[:, r * tq:(r + 1) * tq], pad_rows], axis=0).T
            x = jnp.concatenate([x, jnp.zeros((tq, wq - LANES), F32)], axis=1)
            out += x if r == 0 else pltpu.roll(x, r * HEAD_DIM, 1)
        return out

    last_tile = ((i + 1) * tq - 1) // tk
    reset()
    sweep(slc_products, svt_ref, 0, last_tile)
    o_slc = normalised_out()
    reset()
    sweep(win_products, wvt_ref, jnp.maximum(i * tq - WINDOW, 0) // tk, last_tile)
    o_win = normalised_out()

    sig = _sigmoid(gate_ref[0])
    gate_lane = R_C * g + _iota((LANES, wq), 1) // HEAD_DIM
    out = jnp.zeros((tq, wq), F32)
    for k, branch in enumerate((ocmp_ref[0], o_slc, o_win)):
        place = jnp.where(_iota((LANES, wq), 0) == k * H_C + gate_lane, 1.0, 0.0).astype(BF16)
        out += _times_01_matrix(sig, place) * branch
    o_ref[0] = out


def _value_tiles(kv_t):
    b, _, s = kv_t.shape
    v_t = kv_t[:, KV_W:, :].reshape(b, KV_W, s // TK_NSA, TK_NSA)
    return jnp.swapaxes(v_t, 1, 2)


def _nsa_attn_prompt(z3, slc_t, win_t, sel_t, tz, ocmp):
    b, s, _ = z3.shape
    wq = 4 * HEAD_DIM
    n_diag = tz.shape[0]
    n_kt = s // TK_NSA
    k_spec = lambda col: pl.BlockSpec((1, s, KV_W), lambda bi, g, i: (bi, 0, col // KV_W))
    vt_spec = pl.BlockSpec((1, n_kt, KV_W, TK_NSA), lambda bi, g, i: (bi, 0, 0, 0))
    return pl.pallas_call(
        functools.partial(_nsa_attn_prompt_kernel, n_diag=n_diag),
        out_shape=jax.ShapeDtypeStruct((b, s, D_C), F32),
        grid=(b, KV_C, s // TQ_NSA),
        in_specs=[pl.BlockSpec((1, TQ_NSA, wq), lambda bi, g, i: (bi, i, C_QC // wq + g)),
                  k_spec(C_SK), vt_spec, k_spec(C_WK), vt_spec,
                  pl.BlockSpec((1, 2 * SLC_BLOCK, TQ_NSA), lambda bi, g, i: (bi, 0, i)),
                  pl.BlockSpec((n_diag, 1, R_C, TK_ATT, TQ_ATT), lambda bi, g, i: (0, g, 0, 0, 0)),
                  pl.BlockSpec((1, TQ_NSA, LANES), lambda bi, g, i: (bi, i, C_GATE // LANES)),
                  pl.BlockSpec((1, TQ_NSA, wq), lambda bi, g, i: (bi, i, g))],
        out_specs=pl.BlockSpec((1, TQ_NSA, wq), lambda bi, g, i: (bi, i, g)),
        scratch_shapes=[pltpu.VMEM((2, TK_NSA, R_C * TQ_NSA), F32),
                        pltpu.VMEM((2, 8, R_C * TQ_NSA), F32),
                        pltpu.VMEM((1, R_C * TQ_NSA), F32),
                        pltpu.VMEM((1, R_C * TQ_NSA), F32),
                        pltpu.VMEM((HEAD_DIM, R_C * TQ_NSA), F32)],
        compiler_params=_cparams(("parallel", "parallel", "arbitrary")),
        name="nsa_attn_prompt",
    )(z3, z3, _value_tiles(slc_t), z3, _value_tiles(win_t), sel_t, tz, z3, ocmp)


def _own_lanes(width):
    return _iota((8, width), 1) // HEAD_DIM == _iota((8, width), 0)


def _nsa_cmp_decode_kernel(pt_ref, q_ref, pe_ref, w1_ref, w2_ref, bias_ref, *rest,
                           n_steps, n_sb):
    del pt_ref
    pages = rest[:PAGES_PER_STEP]
    f_ref, imp_ref, buf_ref = rest[PAGES_PER_STEP:]
    step = pl.program_id(1)
    rows_per_page = PAGE // CMP_STRIDE
    out_row = _iota((PAGE, PAGE), 0)
    regroup = jnp.where(_iota((PAGE, PAGE), 1)
                        == CMP_STRIDE * (out_row % rows_per_page) + out_row // rows_per_page,
                        1.0, 0.0).astype(BF16)
    for k, page in enumerate(pages):
        row0 = pl.multiple_of((step * PAGES_PER_STEP + k) * rows_per_page, rows_per_page)
        x = _dot_nt(regroup, page[0, 0].astype(BF16))
        for kv in range(2):
            for j in range(CMP_STRIDE):
                buf_ref[kv, j, pl.ds(row0, rows_per_page), :] = (
                    x[j * rows_per_page:(j + 1) * rows_per_page, kv * KV_W:(kv + 1) * KV_W])

    @pl.when(step == n_steps - 1)
    def _():
        n_chunks = buf_ref.shape[2]
        n_cb = n_chunks - 1
        kv8 = []
        for kv in range(2):
            hid = _compress_hidden(
                lambda j, kv=kv: buf_ref[kv, j],
                pe_ref.at[kv], lambda j, kv=kv: w1_ref[kv, j], n_chunks, KV_W)
            kv8.append(_dot(_gelu_tanh(hid).astype(BF16), w2_ref[kv]).astype(BF16))
        k8, v8 = kv8
        q8 = (_head_rows(q_ref[0], D_C) * SCALE).astype(BF16)
        s = _dot_nt(q8, k8) + bias_ref[...]
        n_idx = _iota((8, n_chunks), 1)
        mask = n_idx < n_cb
        sm = jnp.where(mask, s, NEG)
        m = jnp.max(sm, axis=-1, keepdims=True)
        p = jnp.where(mask, jnp.exp(sm - m), 0.0)
        p = p / jnp.maximum(jnp.sum(p, axis=-1, keepdims=True), 1e-30)
        pb = p.astype(BF16)
        f_ref[0] = jnp.where(_own_lanes(D_C), _dot(pb, v8), 0.0)
        n_pad = imp_ref.shape[2]
        c0 = _iota((n_chunks, n_pad), 0) * CMP_STRIDE
        j_idx = _iota((n_chunks, n_pad), 1)
        s0 = j_idx * SLC_BLOCK
        ov = jnp.logical_and(jnp.logical_and(c0 < s0 + SLC_BLOCK, c0 + CMP_LEN > s0),
                             jnp.logical_and(_iota((n_chunks, n_pad), 0) < n_cb, j_idx < n_sb))
        imp8 = _dot(pb, jnp.where(ov, 1.0, 0.0).astype(BF16))
        row = _iota((8, n_pad), 0)
        g0 = jnp.sum(jnp.where(row < R_C, imp8, 0.0), axis=0, keepdims=True)
        g1 = jnp.sum(jnp.where(row >= R_C, imp8, 0.0), axis=0, keepdims=True)
        imp_ref[0] = jnp.where(row == 0, g0, jnp.where(row == 1, g1, 0.0))


def _nsa_cmp_decode(qc, cache4, page_table, layer, pe2, w1bd, w2heads, bias_cmp, n_pad):
    db, n_pages = page_table.shape
    past = n_pages * PAGE
    n_steps = n_pages // PAGES_PER_STEP
    n_sb = -(-(past + 1) // SLC_BLOCK)
    n_chunks = past // CMP_STRIDE
    const = lambda shape: pl.BlockSpec(shape, lambda b, s, pt: (0,) * len(shape))

    def page_spec(k):
        def imap(b, s, pt):
            return (layer, pt[b, s * PAGES_PER_STEP + k], 0, 0)
        return pl.BlockSpec((1, 1, 2 * KV_W, PAGE), imap)

    grid_spec = pltpu.PrefetchScalarGridSpec(
        num_scalar_prefetch=1,
        grid=(db, n_steps),
        in_specs=[pl.BlockSpec((1, 1, D_C), lambda b, s, pt: (b, 0, 0)),
                  const((2, CMP_LEN, KV_W)), const((2, CMP_LEN, KV_W, KV_W)),
                  const((2, KV_W, D_C)), const((8, n_chunks))]
                 + [page_spec(k) for k in range(PAGES_PER_STEP)],
        out_specs=(pl.BlockSpec((1, 8, D_C), lambda b, s, pt: (b, 0, 0)),
                   pl.BlockSpec((1, 8, n_pad), lambda b, s, pt: (b, 0, 0))),
        scratch_shapes=[pltpu.VMEM((2, CMP_STRIDE, n_chunks, KV_W), F32)],
    )
    return pl.pallas_call(
        functools.partial(_nsa_cmp_decode_kernel, n_steps=n_steps, n_sb=n_sb),
        out_shape=(jax.ShapeDtypeStruct((db, 8, D_C), F32),
                   jax.ShapeDtypeStruct((db, 8, n_pad), F32)),
        grid_spec=grid_spec,
        compiler_params=_cparams(("parallel", "arbitrary")),
        name="nsa_cmp_decode",
    )(page_table, qc, pe2, w1bd, w2heads, bias_cmp, *([cache4] * PAGES_PER_STEP))


def _topk_decode_kernel(imp_ref, idx_ref, *, cur, n_sb, n_top):
    shape = imp_ref.shape
    lane = _iota(shape, 1)
    lane_f = lane.astype(F32)
    causal = jnp.logical_and(lane <= cur, lane < n_sb)
    forced = jnp.logical_or(lane == 0, lane > cur - N_LOCAL)
    score = jnp.where(causal, jnp.where(forced, BIG_SCORE, imp_ref[...]), -1.0)
    out_lane = _iota(idx_ref.shape, 1)
    out = jnp.full(idx_ref.shape, -1, jnp.int32)
    for k in range(n_top):
        m = jnp.max(score, axis=-1, keepdims=True)
        first = jnp.min(jnp.where(score == m, lane_f, 1e9), axis=-1, keepdims=True)
        pick = jnp.where(m > -0.5, first, -1.0).astype(jnp.int32)
        out = jnp.where(out_lane == k, pick, out)
        score = jnp.where(lane_f == first, -2.0, score)
    idx_ref[...] = out


def _topk_decode(imp_rows, cur, n_sb):
    return pl.pallas_call(
        functools.partial(_topk_decode_kernel, cur=cur, n_sb=n_sb, n_top=min(TOP_N, n_sb)),
        out_shape=jax.ShapeDtypeStruct((imp_rows.shape[0], LANES), jnp.int32),
        name="nsa_topk_decode",
    )(imp_rows)


def _softmax_with_new(s, valid, s_new, has_new):
    sm = jnp.where(valid, s, NEG)
    sn = jnp.where(has_new, s_new, NEG)
    m = jnp.maximum(jnp.max(sm, axis=-1, keepdims=True), sn)
    p = jnp.where(valid, jnp.exp(sm - m), 0.0)
    pn = jnp.where(has_new, jnp.exp(sn - m), 0.0)
    inv = 1.0 / jnp.maximum(jnp.sum(p, axis=-1, keepdims=True) + pn, 1e-30)
    return p * inv, pn * inv


def _to_f_form(o8):
    x = jnp.concatenate([o8, jnp.zeros((8, D_C - 2 * KV_W), F32)], axis=1)
    row = _iota((8, D_C), 0)
    out = jnp.zeros((8, D_C), F32)
    for h in range(H_C):
        shift = (h * HEAD_DIM - (KV_W + (h // R_C) * HEAD_DIM)) % D_C
        out = jnp.where(row == h, x if shift == 0 else pltpu.roll(x, shift, 1), out)
    return jnp.where(_own_lanes(D_C), out, 0.0)


def _nsa_attn_decode_kernel(pt_ref, idx_ref, q_ref, gate_ref, nslc_ref, nwin_ref, fcmp_ref, tsel_ref,
                            twin_ref, b0_ref, win_ref, *rest, n_top, new_blk):
    del pt_ref
    pages = rest[:KV_C * n_top]
    o_ref, nw_ref = rest[KV_C * n_top:]
    b = pl.program_id(0)
    w_len = win_ref.shape[3]
    per_page = PAGE // SLC_BLOCK

    qb = jnp.broadcast_to(q_ref[0], (8, D_C))
    row5 = _iota((8, D_C), 0)
    qr = jnp.zeros((8, D_C), F32)
    for h in range(H_C):
        shift = ((h // R_C) * HEAD_DIM - h * HEAD_DIM) % D_C
        qr = jnp.where(row5 == h, qb if shift == 0 else pltpu.roll(qb, shift, 1), qr)
    lane2 = _iota((8, 2 * KV_W), 1)
    row2 = _iota((8, 2 * KV_W), 0)
    q8k_f = jnp.where(lane2 // HEAD_DIM == row2 // R_C, qr[:, :2 * KV_W], 0.0) * SCALE
    q8k = q8k_f.astype(BF16)
    b0 = b0_ref[...]

    lane1 = _iota((8, LANES), 1)
    o_groups = []
    for g in range(KV_C):
        ids = [idx_ref[(b * KV_C + g) * n_top + k] for k in range(n_top)]
        kst = jnp.concatenate([pages[g * n_top + k][0, 0] for k in range(n_top)], axis=1).astype(BF16)
        s = _dot(q8k, kst)
        bias_parts, valid_parts = [], []
        has_new = jnp.int32(0)
        for k in range(n_top):
            blk = ids[k]
            in_cache = jnp.clip(blk, 0, new_blk - 1)
            bias_parts.append(tsel_ref[in_cache // per_page])
            ok = jnp.logical_and(blk >= 0, blk < new_blk).astype(F32)
            valid_parts.append(jnp.where(lane1 // SLC_BLOCK == in_cache % per_page, ok, 0.0))
            has_new = has_new | (blk == new_blk).astype(jnp.int32)
        s = s + jnp.concatenate(bias_parts, axis=1)
        valid = jnp.concatenate(valid_parts, axis=1) > 0.5
        new_row = nslc_ref[0]
        s_new = jnp.sum(q8k_f * new_row, axis=-1, keepdims=True) + b0
        p, pn = _softmax_with_new(s, valid, s_new, has_new > 0)
        o_groups.append(_dot_nt(p.astype(BF16), kst) + pn * new_row)
    f_slc = _to_f_form(jnp.where(row2 < R_C, o_groups[0], o_groups[1]))

    win_t = win_ref[0, 0]
    new_w = nwin_ref[0]
    s = _dot(q8k, win_t.astype(BF16)) + twin_ref[...]
    s_new = jnp.sum(q8k_f * new_w, axis=-1, keepdims=True) + b0
    p, pn = _softmax_with_new(s, s == s, s_new, True)
    f_win = _to_f_form(_dot_nt(p.astype(BF16), win_t.astype(BF16)) + pn * new_w)
    sq = (2 * KV_W, 2 * KV_W)
    new_col = jnp.sum(jnp.where(_iota(sq, 0) == _iota(sq, 1), jnp.broadcast_to(new_w, sq), 0.0),
                      axis=-1, keepdims=True)
    nw_ref[0] = jnp.where(_iota((2 * KV_W, w_len), 1) == w_len - 1, new_col,
                          pltpu.roll(win_t, w_len - 1, 1))

    sig = jnp.broadcast_to(_sigmoid(gate_ref[0]), (8, LANES))
    row1 = _iota((8, LANES), 0)
    total = jnp.zeros((8, D_C), F32)
    for k, f in enumerate((fcmp_ref[0], f_slc, f_win)):
        gk = jnp.sum(jnp.where(lane1 == k * H_C + row1, sig, 0.0), axis=-1, keepdims=True)
        total += gk * f
    o_ref[0] = jnp.sum(total, axis=0, keepdims=True)


def _nsa_attn_decode(qc, gate, new_slc, new_win, fcmp, tsel, twin, b0, cache_slc_t, cache_win_t,
                     page_table, sel_idx, layer, n_top):
    db, n_pages = page_table.shape
    new_blk = n_pages * PAGE // SLC_BLOCK
    w_len = cache_win_t.shape[3]
    per_page = PAGE // SLC_BLOCK
    const = lambda shape: pl.BlockSpec(shape, lambda b, pt, si: (0,) * len(shape))
    per_b = lambda shape: pl.BlockSpec(shape, lambda b, pt, si: (b,) + (0,) * (len(shape) - 1))

    def page_spec(g, k):
        def imap(b, pt, si):
            blk = jnp.clip(si[(b * KV_C + g) * n_top + k], 0, new_blk - 1)
            return (layer, pt[b, blk // per_page], 0, 0)
        return pl.BlockSpec((1, 1, 2 * KV_W, PAGE), imap)

    grid_spec = pltpu.PrefetchScalarGridSpec(
        num_scalar_prefetch=2,
        grid=(db,),
        in_specs=[per_b((1, 1, D_C)), per_b((1, 1, LANES)), per_b((1, 1, 2 * KV_W)),
                  per_b((1, 1, 2 * KV_W)), per_b((1, 8, D_C)),
                  const(tsel.shape), const(twin.shape), const((8, 1)),
                  pl.BlockSpec((1, 1, 2 * KV_W, w_len), lambda b, pt, si: (layer, b, 0, 0))]
                 + [page_spec(g, k) for g in range(KV_C) for k in range(n_top)],
        out_specs=(per_b((1, 1, D_C)), per_b((1, 2 * KV_W, w_len))),
    )
    return pl.pallas_call(
        functools.partial(_nsa_attn_decode_kernel, n_top=n_top, new_blk=new_blk),
        out_shape=(jax.ShapeDtypeStruct((db, 1, D_C), F32),
                   jax.ShapeDtypeStruct((db, 2 * KV_W, w_len), F32)),
        grid_spec=grid_spec,
        compiler_params=_cparams(("arbitrary",)),
        name="nsa_attn_decode",
    )(page_table, sel_idx, qc, gate, new_slc, new_win, fcmp, tsel, twin, b0, cache_win_t,
      *([cache_slc_t] * (KV_C * n_top)))


def _block_diag(mats):
    n = len(mats)
    rows = []
    for a, m in enumerate(mats):
        rows.append(jnp.concatenate([m if a == c else jnp.zeros_like(m) for c in range(n)], axis=-1))
    return jnp.concatenate(rows, axis=-2)


def _nsa_weights(cmp_pe, cmp_w1, cmp_w2):
    pe2 = jnp.concatenate([cmp_pe, cmp_pe], axis=-1)
    w1bd = jnp.stack([_block_diag([cmp_w1[kv]] * KV_C) for kv in range(2)]).astype(BF16)
    zero = jnp.zeros((HEAD_DIM, 4 * HEAD_DIM), F32)
    w2rep = []
    for kv in range(2):
        rep = jnp.concatenate([cmp_w2[kv]] * 4, axis=-1)
        w2rep.append(jnp.stack([jnp.concatenate([rep, zero], axis=0),
                                jnp.concatenate([zero, rep], axis=0)]))
    w2rep = jnp.stack(w2rep).astype(BF16)
    w2heads = jnp.concatenate([w2rep[:, 0], w2rep[:, 1]], axis=-1)
    return pe2, w1bd, w2rep, w2heads


def kernel(x_prompt, x_sample, cache_sb_kv, cache_cmp_kv, cache_slc_kv, cache_win_kv, page_table,
           rel_bias, norm_mix_pre, norm_mix_post, w_in, cmp_pe, cmp_w1, cmp_w2, gmlp_ws, gmlp_b,
           norm_group_out, w_out, norm_ffn_pre, norm_ffn_post, w_ffn_up, w_ffn_down):
    depth = w_in.shape[0]
    bsz, seq, _ = x_prompt.shape
    db = x_sample.shape[0]
    n_pages = page_table.shape[1]
    past = n_pages * PAGE
    n_phys = cache_sb_kv.shape[1]
    w_len = cache_win_kv.shape[2]
    assert x_sample.shape[1] == 1 and seq % 512 == 0 and seq // SLC_BLOCK <= SLC_BLOCK
    assert n_pages % PAGES_PER_STEP == 0 and w_len == WINDOW and past >= WINDOW

    t_prompt = bsz * seq
    tm_p = 512
    n_sb_dec = -(-(past + 1) // SLC_BLOCK)
    n_pad_dec = -(-n_sb_dec // LANES) * LANES
    n_top_dec = min(TOP_N, n_sb_dec)
    n_chunks_dec = past // CMP_STRIDE

    n_q = seq // TQ_ATT
    n_diag = min(n_q, REL_MAX_DIST // TK_ATT + 2)
    tz = _bias_table(rel_bias, n_diag, TK_ATT, TQ_ATT, TK_ATT, -1, 1, 0)
    tz = tz.reshape(KV_C, R_C, n_diag, TK_ATT, TQ_ATT).transpose(2, 0, 1, 3, 4)
    bias_cmp_p = _bias_table(rel_bias, n_q, TQ_ATT, seq // CMP_STRIDE, TQ_ATT, 1, -CMP_STRIDE,
                             -(CMP_LEN - 1))
    bias_cmp_d = _bias_table(rel_bias, 1, 8, n_chunks_dec, 0, 0, -CMP_STRIDE,
                             past - (CMP_LEN - 1))[:, 0, 0, :]
    n_tsel = -(-n_pages // 8) * 8
    tsel = _bias_table(rel_bias, 1, n_tsel, PAGE, 0, -PAGE, -1, past)[:, 0]
    tsel = jnp.swapaxes(tsel, 0, 1)
    twin = _bias_table(rel_bias, 1, 8, w_len, 0, 0, -1, w_len)[:, 0, 0, :]
    b0 = rel_bias[0].reshape(H_C, 1)

    pos_minor = lambda c: c.transpose(0, 1, 3, 4, 5, 2).reshape(c.shape[0], c.shape[1], -1, c.shape[2])
    cache_sb_t = pos_minor(cache_sb_kv)
    cache_cmp_t = pos_minor(cache_cmp_kv)
    cache_slc_t = pos_minor(cache_slc_kv)
    cache_win_t = pos_minor(cache_win_kv)
    from_pos_minor = lambda a, heads: jnp.moveaxis(
        a.reshape(a.shape[:-2] + (2, heads, HEAD_DIM, a.shape[-1])), -1, -4)

    xp = x_prompt.reshape(t_prompt, D_MODEL)
    xs = x_sample.reshape(db, D_MODEL)
    outs = {k: [] for k in ("p_sb", "p_cmp", "p_slc", "p_win", "s_sb", "s_cmp", "s_slc", "s_win", "s_gv")}
    row = lambda v: v.reshape(1, -1)

    for l in range(depth):
        w_in_l = jnp.pad(w_in[l], ((0, 0), (0, NP_IN - N_IN))).astype(BF16)
        w_out_l = w_out[l].astype(BF16)
        w_up_l = w_ffn_up[l].astype(BF16)
        w_down_l = w_ffn_down[l].astype(BF16)
        pe2, w1bd, w2rep, w2heads = _nsa_weights(cmp_pe[l], cmp_w1[l], cmp_w2[l])
        g_pre, g_post = row(norm_mix_pre[l]), row(norm_mix_post[l])
        g_grp = row(norm_group_out[l])
        gf_pre, gf_post = row(norm_ffn_pre[l]), row(norm_ffn_post[l])

        z = _project(xp, g_pre, w_in_l, tm_p)
        z3 = z.reshape(bsz, seq, NP_IN)
        o_a = _sb_prompt(z3)
        bias_rows = jnp.repeat(gmlp_b[l].T, CB, axis=1)
        o_b = _gmlp_prompt(z3, gmlp_ws[l], bias_rows)
        kvc = _compress_prompt(z3, pe2, w1bd, w2rep)
        o_cmp, sel_t = _nsa_cmp_prompt(z3, kvc, bias_cmp_p)
        slc_t = jnp.swapaxes(z3[:, :, C_SK:C_SK + 2 * KV_W], 1, 2)
        win_t = jnp.swapaxes(z3[:, :, C_WK:C_WK + 2 * KV_W], 1, 2)
        o_c = _nsa_attn_prompt(z3, slc_t, win_t, sel_t, tz, o_cmp)
        xp = _mix_out(o_a.reshape(t_prompt, D_A), o_b.reshape(t_prompt, D_B),
                      o_c.reshape(t_prompt, D_C), xp, g_grp, w_out_l, g_post, tm_p)
        xp = _ffn(xp, gf_pre, w_up_l, w_down_l, gf_post, 2 * tm_p, D_FF // 2)
        outs["p_sb"].append(z3[:, :, C_KA:C_KA + 2 * D_A].reshape(bsz, seq, 2, H_A, HEAD_DIM))
        kv_shape = (bsz, seq, 2, KV_C, HEAD_DIM)
        outs["p_cmp"].append(z3[:, :, C_CK:C_CK + 2 * KV_W].reshape(kv_shape))
        outs["p_slc"].append(from_pos_minor(slc_t, KV_C))
        n_win = min(WINDOW, seq)
        outs["p_win"].append(from_pos_minor(win_t[:, :, seq - n_win:], KV_C))

        zs = _project(xs, g_pre, w_in_l, db)
        zs3 = zs.reshape(db, 1, NP_IN)
        o_a_s = _sb_decode(zs3[:, :, C_QA:C_QA + D_A], cache_sb_t, page_table, l)
        w_row = jnp.repeat(gmlp_ws[l][:, 0, 0], CB).reshape(1, D_B)
        b_row = jnp.repeat(gmlp_b[l][:, 0], CB).reshape(1, D_B)
        o_b_s, vn_s = _gmlp_sample(zs[:, C_UB:C_UB + D_B], zs[:, C_VB:C_VB + D_B], w_row, b_row)
        qc_s = zs3[:, :, C_QC:C_QC + D_C]
        f_cmp, imp = _nsa_cmp_decode(qc_s, cache_cmp_t, page_table, l, pe2, w1bd, w2heads,
                                     bias_cmp_d, n_pad_dec)
        idx = _topk_decode(imp[:, :KV_C, :].reshape(db * KV_C, n_pad_dec), past // SLC_BLOCK, n_sb_dec)
        sel_idx = idx[:, :n_top_dec].reshape(-1)
        gate_s = zs3[:, :, C_GATE:C_GATE + LANES]
        new_slc = zs3[:, :, C_SK:C_SK + 2 * KV_W]
        new_win = zs3[:, :, C_WK:C_WK + 2 * KV_W]
        o_c_s, win_out = _nsa_attn_decode(qc_s, gate_s, new_slc, new_win, f_cmp, tsel, twin, b0,
                                          cache_slc_t, cache_win_t, page_table, sel_idx, l, n_top_dec)
        xs = _mix_out(o_a_s.reshape(db, D_A), o_b_s, o_c_s.reshape(db, D_C), xs, g_grp, w_out_l,
                      g_post, db)
        xs = _ffn(xs, gf_pre, w_up_l, w_down_l, gf_post, db, D_FF // 2)
        outs["s_sb"].append(zs[:, C_KA:C_KA + 2 * D_A].reshape(db, 1, 2, H_A, HEAD_DIM))
        outs["s_cmp"].append(zs[:, C_CK:C_CK + 2 * KV_W].reshape(db, 1, 2, KV_C, HEAD_DIM))
        outs["s_slc"].append(zs[:, C_SK:C_SK + 2 * KV_W].reshape(db, 1, 2, KV_C, HEAD_DIM))
        outs["s_win"].append(from_pos_minor(win_out, KV_C))
        outs["s_gv"].append(vn_s.reshape(db, 1, D_B))

    st = lambda k: jnp.stack(outs[k])
    return (xp.reshape(bsz, seq, D_MODEL), xs.reshape(db, 1, D_MODEL), st("p_sb"), st("p_cmp"),
            st("p_slc"), st("p_win"), st("s_sb"), st("s_cmp"), st("s_slc"), st("s_win"), st("s_gv"))
```

```python
import functools
import math

import jax
import jax.numpy as jnp
import numpy as np
from jax import lax
from jax.experimental import pallas as pl
from jax.experimental.pallas import tpu as pltpu

F32 = jnp.float32
BF16 = jnp.bfloat16

D_MODEL = 1024
HEAD_DIM = 64
D_A = 256
H_A = 4
D_B = 256
G_B = 4
CB = 64
D_C = 512
H_C = 8
KV_C = 2
R_C = 4
KV_W = 128
N_IN = 3 * D_A + 2 * D_B + D_C + 6 * KV_W + 3 * H_C
NP_IN = 2688
CHUNK = 128
CMP_LEN = 32
CMP_STRIDE = 16
SLC_BLOCK = 64
TOP_N = 16
N_LOCAL = 2
WINDOW = 512
NUM_BUCKETS = 32
REL_MAX_DIST = 2048
D_FF = 2816
EPS = 1e-6
SCALE = HEAD_DIM ** -0.5
PAGE = 128

C_QA, C_KA, C_VA, C_UB, C_VB, C_QC = 0, 256, 512, 768, 1024, 1280
C_CK, C_CV, C_SK, C_SV, C_WK, C_WV, C_GATE = 1792, 1920, 2048, 2176, 2304, 2432, 2560

LANES = 128
TQ_ATT = 128
TK_ATT = 128
TQ_NSA = 512
TK_NSA = 512
VMEM_LIMIT = 56 * 1024 * 1024
NEG = -1e30
DECAYED = -110.0
BIG_SCORE = 3e38
PAGES_PER_STEP = 16


def _cparams(sem):
    return pltpu.CompilerParams(dimension_semantics=sem, vmem_limit_bytes=VMEM_LIMIT)


def _dot(a, b):
    return jnp.dot(a, b, preferred_element_type=F32)


def _dot_nt(a, b):
    return lax.dot_general(a, b, (((1,), (1,)), ((), ())), preferred_element_type=F32)


def _rms(x, g):
    return x * lax.rsqrt(jnp.mean(x * x, axis=-1, keepdims=True) + EPS) * g


def _sigmoid(x):
    return 1.0 / (1.0 + jnp.exp(-x))


def _gelu_tanh(x):
    return 0.5 * x * (1.0 + jnp.tanh(math.sqrt(2.0 / math.pi) * (x + 0.044715 * (x * x * x))))


def _iota(shape, dim):
    return lax.broadcasted_iota(jnp.int32, shape, dim)


def _bucket_thresholds():
    n = np.arange(0, 1 << 15)
    exact = NUM_BUCKETS // 2
    nf = np.maximum(n, 1).astype(np.float32)
    big = exact + (np.log(nf / np.float32(exact)) / np.float32(math.log(REL_MAX_DIST / exact))
                   * np.float32(NUM_BUCKETS - exact)).astype(np.int32)
    bucket = np.where(n < exact, n, np.minimum(big, NUM_BUCKETS - 1))
    assert np.all(np.diff(bucket) >= 0)
    return [int(np.argmax(bucket >= k)) for k in range(1, NUM_BUCKETS)]


_THR = _bucket_thresholds()


def _bias_table_kernel(tab_ref, o_ref, *, a, rs, cs, c0):
    shape = o_ref.shape[2:]
    dist = a * pl.program_id(0) + rs * _iota(shape, 0) + cs * _iota(shape, 1) + c0
    outs = [jnp.full(shape, tab_ref[h], F32) for h in range(H_C)]
    for k in range(1, NUM_BUCKETS):
        ge = dist >= _THR[k - 1]
        for h in range(H_C):
            outs[h] = jnp.where(ge, tab_ref[k * H_C + h], outs[h])
    for h in range(H_C):
        o_ref[h, 0] = outs[h]


def _bias_table(rel_bias, steps, rows, cols, a, rs, cs, c0):
    return pl.pallas_call(
        functools.partial(_bias_table_kernel, a=a, rs=rs, cs=cs, c0=c0),
        out_shape=jax.ShapeDtypeStruct((H_C, steps, rows, cols), F32),
        grid=(steps,),
        in_specs=[pl.BlockSpec(memory_space=pltpu.SMEM)],
        out_specs=pl.BlockSpec((H_C, 1, rows, cols), lambda i: (0, i, 0, 0)),
        compiler_params=_cparams(("arbitrary",)),
        name="bias_table",
    )(rel_bias.reshape(-1))


def _proj_kernel(x_ref, g_ref, w_ref, o_ref):
    h = _rms(x_ref[...], g_ref[...])
    o_ref[...] = _dot(h.astype(BF16), w_ref[...])


def _project(x, g, w, tm):
    t = x.shape[0]
    return pl.pallas_call(
        _proj_kernel,
        out_shape=jax.ShapeDtypeStruct((t, NP_IN), F32),
        grid=(t // tm,),
        in_specs=[pl.BlockSpec((tm, D_MODEL), lambda i: (i, 0)),
                  pl.BlockSpec((1, D_MODEL), lambda i: (0, 0)),
                  pl.BlockSpec((D_MODEL, NP_IN), lambda i: (0, 0))],
        out_specs=pl.BlockSpec((tm, NP_IN), lambda i: (i, 0)),
        compiler_params=_cparams(("parallel",)),
        name="in_proj",
    )(x, g, w)


def _mixout_kernel(oa_ref, ob_ref, oc_ref, x_ref, gg_ref, w_ref, gp_ref, o_ref):
    gg = gg_ref[...]
    a = _rms(oa_ref[...], gg[:, :D_A]).astype(BF16)
    b = _rms(ob_ref[...], gg[:, D_A:D_A + D_B]).astype(BF16)
    c = _rms(oc_ref[...], gg[:, D_A + D_B:]).astype(BF16)
    y = (_dot(a, w_ref[0:D_A, :]) + _dot(b, w_ref[D_A:D_A + D_B, :])
         + _dot(c, w_ref[D_A + D_B:, :]))
    o_ref[...] = x_ref[...] + _rms(y, gp_ref[...])


def _mix_out(oa, ob, oc, x, gg, w, gp, tm):
    t = x.shape[0]
    row = lambda width: pl.BlockSpec((tm, width), lambda i: (i, 0))
    full = lambda r, c: pl.BlockSpec((r, c), lambda i: (0, 0))
    return pl.pallas_call(
        _mixout_kernel,
        out_shape=jax.ShapeDtypeStruct((t, D_MODEL), F32),
        grid=(t // tm,),
        in_specs=[row(D_A), row(D_B), row(D_C), row(D_MODEL), full(1, D_MODEL),
                  full(D_MODEL, D_MODEL), full(1, D_MODEL)],
        out_specs=row(D_MODEL),
        compiler_params=_cparams(("parallel",)),
        name="mix_out",
    )(oa, ob, oc, x, gg, w, gp)


def _ffn_kernel(x_ref, gpre_ref, wg_ref, wu_ref, wd_ref, gpost_ref, o_ref, h_ref, acc_ref):
    j = pl.program_id(1)

    @pl.when(j == 0)
    def _():
        h_ref[...] = _rms(x_ref[...], gpre_ref[...]).astype(BF16)
        acc_ref[...] = jnp.zeros_like(acc_ref)

    h = h_ref[...]
    g = _dot(h, wg_ref[...])
    u = _dot(h, wu_ref[...])
    act = (g * _sigmoid(g)) * u
    acc_ref[...] += _dot(act.astype(BF16), wd_ref[...])

    @pl.when(j == pl.num_programs(1) - 1)
    def _():
        o_ref[...] = x_ref[...] + _rms(acc_ref[...], gpost_ref[...])


def _ffn(x, gpre, w_up, w_down, gpost, tm, tf):
    t = x.shape[0]
    nf = D_FF // tf
    return pl.pallas_call(
        _ffn_kernel,
        out_shape=jax.ShapeDtypeStruct((t, D_MODEL), F32),
        grid=(t // tm, nf),
        in_specs=[pl.BlockSpec((tm, D_MODEL), lambda i, j: (i, 0)),
                  pl.BlockSpec((1, D_MODEL), lambda i, j: (0, 0)),
                  pl.BlockSpec((D_MODEL, tf), lambda i, j: (0, j)),
                  pl.BlockSpec((D_MODEL, tf), lambda i, j: (0, j + nf)),
                  pl.BlockSpec((tf, D_MODEL), lambda i, j: (j, 0)),
                  pl.BlockSpec((1, D_MODEL), lambda i, j: (0, 0))],
        out_specs=pl.BlockSpec((tm, D_MODEL), lambda i, j: (i, 0)),
        scratch_shapes=[pltpu.VMEM((tm, D_MODEL), BF16), pltpu.VMEM((tm, D_MODEL), F32)],
        compiler_params=_cparams(("parallel", "arbitrary")),
        name="ffn",
    )(x, gpre, w_up, w_up, w_down, gpost)


def _log_sig_pair(z):
    soft = jnp.log(1.0 + jnp.exp(-jnp.abs(z)))
    log_beta = jnp.minimum(z, 0.0) - soft
    return log_beta, log_beta - z


def _times_01_matrix(x, mat_bf16, pieces=3):
    out = None
    rem = x
    for k in range(pieces):
        part = rem.astype(BF16)
        term = _dot(part, mat_bf16)
        out = term if out is None else out + term
        if k + 1 < pieces:
            rem = rem - part.astype(F32)
    return out


def _suffix_sum_exclusive(x, upper_bf16):
    return _times_01_matrix(x, upper_bf16, pieces=2)


def _strict_upper(n):
    return jnp.where(_iota((n, n), 0) > _iota((n, n), 1), 1.0, 0.0).astype(BF16)


def _sb_prompt_kernel(q_ref, k_ref, v_ref, o_ref, run_ref, acc_ref, *, tq, tk):
    i = pl.program_id(1)
    lane_head = _iota((tq, D_A), 1) // HEAD_DIM
    qs = q_ref[0] * SCALE
    qh = [jnp.where(lane_head == h, qs, 0.0).astype(BF16) for h in range(H_A)]
    upper = _strict_upper(tk)
    run_ref[...] = jnp.zeros_like(run_ref)
    acc_ref[...] = jnp.zeros_like(acc_ref)

    def sweep_tile(j, diagonal):
        ks = pl.multiple_of(j * tk, tk)
        kt = k_ref[0, pl.ds(ks, tk), :].astype(BF16)
        vt = v_ref[0, pl.ds(ks, tk), :].astype(BF16)
        mask = _iota((tq, tk), 1) < _iota((tq, tk), 0)
        heads = range(H_A)
        z = [_dot_nt(qh[h], kt) for h in heads]
        pairs = [_log_sig_pair(z[h]) for h in heads]
        lb = [pairs[h][0] for h in heads]
        lk = [jnp.where(mask, pairs[h][1], 0.0) if diagonal else pairs[h][1] for h in heads]
        after = [_suffix_sum_exclusive(lk[h], upper) + run_ref[h] for h in heads]
        a = [jnp.exp(lb[h] + after[h]) for h in heads]
        if diagonal:
            a = [jnp.where(mask, a[h], 0.0) for h in heads]
        pv = [_dot(a[h].astype(BF16), vt) for h in heads]
        worst = jnp.float32(-jnp.inf)
        for h in heads:
            acc_ref[h] += pv[h]
            new_run = after[h][:, 0:1] + lk[h][:, 0:1]
            run_ref[h] = new_run
            worst = jnp.maximum(worst, jnp.max(new_run))
        return (worst > DECAYED).astype(jnp.int32)

    def cond(c):
        j, alive = c
        return jnp.logical_and(j >= 0, alive > 0)

    def body(c):
        return c[0] - 1, sweep_tile(c[0], False)

    assert tq == tk
    lax.while_loop(cond, body, (i - 1, sweep_tile(i, True)))
    out = acc_ref[0]
    for h in range(1, H_A):
        out = jnp.where(lane_head == h, acc_ref[h], out)
    o_ref[0] = out


def _sb_prompt(z3, tq=256, tk=256):
    b, s, _ = z3.shape
    return pl.pallas_call(
        functools.partial(_sb_prompt_kernel, tq=tq, tk=tk),
        out_shape=jax.ShapeDtypeStruct((b, s, D_A), F32),
        grid=(b, s // tq),
        in_specs=[pl.BlockSpec((1, tq, D_A), lambda bi, i: (bi, i, C_QA // D_A)),
                  pl.BlockSpec((1, s, D_A), lambda bi, i: (bi, 0, C_KA // D_A)),
                  pl.BlockSpec((1, s, D_A), lambda bi, i: (bi, 0, C_VA // D_A))],
        out_specs=pl.BlockSpec((1, tq, D_A), lambda bi, i: (bi, i, 0)),
        scratch_shapes=[pltpu.VMEM((H_A, tq, 1), F32), pltpu.VMEM((H_A, tq, D_A), F32)],
        compiler_params=_cparams(("parallel", "arbitrary")),
        name="sb_prompt",
    )(z3, z3, z3)


def _head_rows(row_vec, width):
    full = jnp.broadcast_to(row_vec, (8, width))
    return jnp.where(_iota((8, width), 1) // HEAD_DIM == _iota((8, width), 0), full, 0.0)


def _sb_decode_kernel(pt_ref, alive_ref, q_ref, run_in_ref, acc_in_ref, *rest, n_pages_step, n_steps):
    del pt_ref
    pages = rest[:n_pages_step]
    run_ref, acc_ref, o_ref, live_ref = rest[n_pages_step:]
    b = pl.program_id(0)
    step = pl.program_id(1)

    @pl.when(step == 0)
    def _():
        run_ref[...] = run_in_ref[...]
        acc_ref[...] = acc_in_ref[...]
        live_ref[0] = alive_ref[b]

    @pl.when(live_ref[0] > 0)
    def _():
        q8 = (_head_rows(q_ref[0], D_A) * SCALE).astype(BF16)
        upper = _strict_upper(PAGE)
        run = run_ref[0][:, 0:1]
        acc = acc_ref[0]
        for page in pages:
            kt = page[0, 0, 0:D_A, :].astype(BF16)
            vt = page[0, 0, D_A:2 * D_A, :].astype(BF16)
            z = _dot(q8, kt)
            lb, lk = _log_sig_pair(z)
            after = _suffix_sum_exclusive(lk, upper) + run
            a = jnp.exp(lb + after)
            acc = acc + _dot_nt(a.astype(BF16), vt)
            run = after[:, 0:1] + lk[:, 0:1]
        run_ref[0] = jnp.broadcast_to(run, (8, LANES))
        acc_ref[0] = acc
        head_rows = _iota((8, 1), 0) < H_A
        live_ref[0] = (jnp.max(jnp.where(head_rows, run, -jnp.inf)) > DECAYED).astype(jnp.int32)

    @pl.when(step == n_steps - 1)
    def _():
        o_ref[0] = jnp.sum(jnp.where(_own_lanes(D_A), acc_ref[0], 0.0), axis=0, keepdims=True)


def _sb_decode_phase(qa, cache_t, page_table, layer, alive, run, acc, first_back, n_pages_step,
                     n_steps):
    db, n_pages = page_table.shape

    def page_spec(k):
        def imap(b, s, pt, al):
            page = pt[b, n_pages - 1 - first_back - (s * n_pages_step + k)]
            return (layer, jnp.where(al[b] > 0, page, 0), 0, 0)
        return pl.BlockSpec((1, 1, 2 * D_A, PAGE), imap)

    per_b = lambda shape: pl.BlockSpec(shape, lambda b, s, pt, al: (b,) + (0,) * (len(shape) - 1))
    grid_spec = pltpu.PrefetchScalarGridSpec(
        num_scalar_prefetch=2,
        grid=(db, n_steps),
        in_specs=[per_b((1, 1, D_A)), per_b((1, 8, LANES)), per_b((1, 8, D_A))]
                 + [page_spec(k) for k in range(n_pages_step)],
        out_specs=(per_b((1, 8, LANES)), per_b((1, 8, D_A)), per_b((1, 1, D_A))),
        scratch_shapes=[pltpu.SMEM((1,), jnp.int32)],
    )
    return pl.pallas_call(
        functools.partial(_sb_decode_kernel, n_pages_step=n_pages_step, n_steps=n_steps),
        out_shape=(jax.ShapeDtypeStruct((db, 8, LANES), F32), jax.ShapeDtypeStruct((db, 8, D_A), F32),
                   jax.ShapeDtypeStruct((db, 1, D_A), F32)),
        grid_spec=grid_spec,
        compiler_params=_cparams(("arbitrary", "arbitrary")),
        name="sb_decode",
    )(page_table, alive, qa, run, acc, *([cache_t] * n_pages_step))


SB_FIRST_PAGES = 4


def _sb_decode(qa, cache_t, page_table, layer):
    db, n_pages = page_table.shape
    first = min(SB_FIRST_PAGES, n_pages)
    ones = jnp.ones((db,), jnp.int32)
    run0 = jnp.zeros((db, 8, LANES), F32)
    acc0 = jnp.zeros((db, 8, D_A), F32)
    run1, acc1, o1 = _sb_decode_phase(qa, cache_t, page_table, layer, ones, run0, acc0, 0, first, 1)
    rest = n_pages - first
    if rest == 0:
        return o1
    per_step = max(d for d in range(1, 13) if rest % d == 0)
    alive = (jnp.max(run1[:, :H_A, 0], axis=1) > DECAYED).astype(jnp.int32)
    return lax.cond(
        jnp.any(alive > 0),
        lambda: _sb_decode_phase(qa, cache_t, page_table, layer, alive, run1, acc1, first, per_step,
                                 rest // per_step)[2],
        lambda: o1)


def _layernorm(x):
    xc = x - jnp.mean(x, axis=-1, keepdims=True)
    return xc * lax.rsqrt(jnp.mean(xc * xc, axis=-1, keepdims=True) + EPS)


def _gmlp_prompt_kernel(u_ref, v_ref, w_ref, b_ref, o_ref, *, n_chunks):
    tril = _iota((CHUNK, CHUNK), 0) >= _iota((CHUNK, CHUNK), 1)
    ws = [jnp.where(tril, w_ref[g], 0.0).astype(BF16) for g in range(G_B)]
    group = _iota((CHUNK, D_B), 1) // CB
    for c in range(n_chunks):
        rows = slice(c * CHUNK, (c + 1) * CHUNK)
        vn = _layernorm(v_ref[0, rows, :]).astype(BF16)
        mixed = b_ref[...]
        for g in range(G_B):
            mixed = mixed + jnp.where(group == g, _dot(ws[g], vn), 0.0)
        o_ref[0, rows, :] = u_ref[0, rows, :] * mixed


def _gmlp_prompt(z3, ws, bias_rows, n_chunks=4):
    b, s, _ = z3.shape
    tr = n_chunks * CHUNK
    return pl.pallas_call(
        functools.partial(_gmlp_prompt_kernel, n_chunks=n_chunks),
        out_shape=jax.ShapeDtypeStruct((b, s, D_B), F32),
        grid=(b, s // tr),
        in_specs=[pl.BlockSpec((1, tr, D_B), lambda bi, i: (bi, i, C_UB // D_B)),
                  pl.BlockSpec((1, tr, D_B), lambda bi, i: (bi, i, C_VB // D_B)),
                  pl.BlockSpec((G_B, CHUNK, CHUNK), lambda bi, i: (0, 0, 0)),
                  pl.BlockSpec((CHUNK, D_B), lambda bi, i: (0, 0))],
        out_specs=pl.BlockSpec((1, tr, D_B), lambda bi, i: (bi, i, 0)),
        compiler_params=_cparams(("parallel", "parallel")),
        name="gmlp_prompt",
    )(z3, z3, ws, bias_rows)


def _gmlp_sample_kernel(u_ref, v_ref, w_ref, b_ref, o_ref, vn_ref):
    vn = _layernorm(v_ref[...])
    vn_ref[...] = vn
    o_ref[...] = u_ref[...] * (w_ref[...] * vn + b_ref[...])


def _gmlp_sample(u, v, w_row, b_row):
    return pl.pallas_call(
        _gmlp_sample_kernel,
        out_shape=(jax.ShapeDtypeStruct(u.shape, F32), jax.ShapeDtypeStruct(u.shape, F32)),
        name="gmlp_sample",
    )(u, v, w_row, b_row)


def _compress_hidden(load_rows, pe_ref, w1_fn, n_chunks, width):
    hid_a = jnp.zeros((n_chunks, width), F32)
    hid_b = jnp.zeros((n_chunks, width), F32)
    for j in range(CMP_STRIDE):
        x = load_rows(j)
        hid_a += _dot((x + pe_ref[j:j + 1, :]).astype(BF16), w1_fn(j))
        hid_b += _dot((x + pe_ref[CMP_STRIDE + j:CMP_STRIDE + j + 1, :]).astype(BF16),
                      w1_fn(CMP_STRIDE + j))
    return hid_a + pltpu.roll(hid_b, n_chunks - 1, 0)


def _compress_prompt_kernel(z_ref, pe_ref, w1_ref, w2_ref, o_ref, *, n_chunks):
    hid = _compress_hidden(lambda j: z_ref[0, pl.ds(j, n_chunks, stride=CMP_STRIDE), :],
                           pe_ref.at[0], lambda j: w1_ref[0, j], n_chunks, KV_W)
    act = _gelu_tanh(hid).astype(BF16)
    valid = _iota((n_chunks, 4 * HEAD_DIM), 0) < n_chunks - 1
    for g in range(KV_C):
        o_ref[0, 0, g] = jnp.where(valid, _dot(act, w2_ref[0, g]), 0.0)


def _compress_prompt(z3, pe2, w1bd, w2rep):
    b, s, _ = z3.shape
    n_chunks = s // CMP_STRIDE
    return pl.pallas_call(
        functools.partial(_compress_prompt_kernel, n_chunks=n_chunks),
        out_shape=jax.ShapeDtypeStruct((b, 2, KV_C, n_chunks, 4 * HEAD_DIM), F32),
        grid=(b, 2),
        in_specs=[pl.BlockSpec((1, s, KV_W), lambda bi, kv: (bi, 0, C_CK // KV_W + kv)),
                  pl.BlockSpec((1, CMP_LEN, KV_W), lambda bi, kv: (kv, 0, 0)),
                  pl.BlockSpec((1, CMP_LEN, KV_W, KV_W), lambda bi, kv: (kv, 0, 0, 0)),
                  pl.BlockSpec((1, KV_C, KV_W, 4 * HEAD_DIM), lambda bi, kv: (kv, 0, 0, 0))],
        out_specs=pl.BlockSpec((1, 1, KV_C, n_chunks, 4 * HEAD_DIM),
                               lambda bi, kv: (bi, kv, 0, 0, 0)),
        compiler_params=_cparams(("parallel", "parallel")),
        name="nsa_compress_prompt",
    )(z3, pe2, w1bd, w2rep)


def _select_blocks_t(score, causal, n_top):
    n_sb = score.shape[0]
    jt = _iota(score.shape, 0)
    rank = jnp.zeros(score.shape, F32)
    for jp in range(n_sb):
        row = score[jp:jp + 1, :]
        ge = jnp.where(row >= score, 1.0, 0.0)
        gt = jnp.where(row > score, 1.0, 0.0)
        rank += jnp.where(jt > jp, ge, gt)
    return jnp.where(causal, jnp.where(rank < n_top, 1.0, 0.0), 0.0)


def _nsa_cmp_prompt_kernel(q0_ref, q1_ref, kvc_ref, bias_ref, ocmp_ref, sel_ref, *, n_sb, n_cb):
    i = pl.program_id(1)
    tq = TQ_ATT
    n_pad = kvc_ref.shape[3]
    lane_head = _iota((tq, 4 * HEAD_DIM), 1) // HEAD_DIM
    n_idx = _iota((tq, n_pad), 1)
    dist = (i * tq + _iota((tq, n_pad), 0)) - (n_idx * CMP_STRIDE + (CMP_LEN - 1))
    mask = jnp.logical_and(dist >= 0, n_idx < n_cb)
    c0 = _iota((n_sb, n_pad), 1) * CMP_STRIDE
    s0 = _iota((n_sb, n_pad), 0) * SLC_BLOCK
    ov_t = jnp.where(jnp.logical_and(c0 < s0 + SLC_BLOCK, c0 + CMP_LEN > s0), 1.0, 0.0).astype(BF16)
    t_row = i * tq + _iota((n_sb, tq), 1)
    cur = t_row // SLC_BLOCK
    jt = _iota((n_sb, tq), 0)
    causal = jt <= cur
    forced = jnp.logical_or(jt == 0, jt > cur - N_LOCAL)
    s_all = []
    for g, q_ref in enumerate((q0_ref, q1_ref)):
        qs = q_ref[0] * SCALE
        qm = jnp.concatenate([jnp.where(lane_head == r, qs, 0.0) for r in range(R_C)],
                             axis=0).astype(BF16)
        s_all.append(_dot_nt(qm, kvc_ref[0, 0, g].astype(BF16)))
    probs = []
    for h in range(H_C):
        g, r = h // R_C, h % R_C
        sm = jnp.where(mask, s_all[g][r * tq:(r + 1) * tq] + bias_ref[h, 0], NEG)
        m = jnp.max(sm, axis=-1, keepdims=True)
        p = jnp.where(mask, jnp.exp(sm - m), 0.0)
        p = p / jnp.maximum(jnp.sum(p, axis=-1, keepdims=True), 1e-30)
        probs.append(p.astype(BF16))
    sel_t = []
    for g in range(KV_C):
        vc = kvc_ref[0, 1, g].astype(BF16)
        o_acc = jnp.zeros((tq, 4 * HEAD_DIM), F32)
        imp_t = jnp.zeros((n_sb, tq), F32)
        for r in range(R_C):
            pb = probs[R_C * g + r]
            o_acc += jnp.where(lane_head == r, _dot(pb, vc), 0.0)
            imp_t += _dot_nt(ov_t, pb)
        ocmp_ref[0, :, g * 4 * HEAD_DIM:(g + 1) * 4 * HEAD_DIM] = o_acc
        score = jnp.where(causal, jnp.where(forced, BIG_SCORE, imp_t), -1.0)
        sel_t.append(_select_blocks_t(score, causal, min(TOP_N, n_sb)))
    pad = jnp.zeros((SLC_BLOCK - n_sb, tq), F32)
    sel_ref[0] = jnp.concatenate([sel_t[0], pad, sel_t[1], pad] if n_sb < SLC_BLOCK else sel_t,
                                 axis=0)


def _nsa_cmp_prompt(z3, kvc, bias_cmp):
    b, s, _ = z3.shape
    n_sb = s // SLC_BLOCK
    n_cb = (s - CMP_LEN) // CMP_STRIDE + 1
    n_pad = kvc.shape[3]
    wq = 4 * HEAD_DIM
    return pl.pallas_call(
        functools.partial(_nsa_cmp_prompt_kernel, n_sb=n_sb, n_cb=n_cb),
        out_shape=(jax.ShapeDtypeStruct((b, s, D_C), F32),
                   jax.ShapeDtypeStruct((b, 2 * SLC_BLOCK, s), F32)),
        grid=(b, s // TQ_ATT),
        in_specs=[pl.BlockSpec((1, TQ_ATT, wq), lambda bi, i: (bi, i, C_QC // wq)),
                  pl.BlockSpec((1, TQ_ATT, wq), lambda bi, i: (bi, i, C_QC // wq + 1)),
                  pl.BlockSpec((1, 2, KV_C, n_pad, wq), lambda bi, i: (bi, 0, 0, 0, 0)),
                  pl.BlockSpec((H_C, 1, TQ_ATT, n_pad), lambda bi, i: (0, i, 0, 0))],
        out_specs=(pl.BlockSpec((1, TQ_ATT, D_C), lambda bi, i: (bi, i, 0)),
                   pl.BlockSpec((1, 2 * SLC_BLOCK, TQ_ATT), lambda bi, i: (bi, 0, i))),
        compiler_params=_cparams(("parallel", "parallel")),
        name="nsa_cmp_prompt",
    )(z3, z3, kvc, bias_cmp)


def _score_tile(args, slot, s_ref, peak_ref, tz_ref, n_diag):
    raw, valid_f, d0 = args
    tk, tb = raw.shape[0], TK_ATT
    nq = valid_f.shape[1] // tb
    tabs = {e: tz_ref[jnp.clip(d0 + e, 0, n_diag - 1), 0] for e in range(-(tk // tb - 1), nq)}
    bias = jnp.concatenate(
        [jnp.concatenate([tabs[cq - ck][r] for r in range(R_C) for cq in range(nq)], axis=1)
         for ck in range(tk // tb)], axis=0)
    valid = jnp.concatenate([valid_f] * R_C, axis=1) > 0.5
    sm = jnp.where(valid, raw + bias, NEG)
    s_ref[slot] = sm
    peak_ref[slot] = jnp.broadcast_to(jnp.max(sm, axis=0, keepdims=True), peak_ref.shape[1:])


def _absorb_tile(vt, slot, s_ref, peak_ref, m_ref, l_ref, acc_ref):
    sm = s_ref[slot]
    m_prev = m_ref[...]
    m_new = jnp.maximum(m_prev, peak_ref[slot][0:1])
    alpha = jnp.exp(m_prev - m_new)
    p = jnp.exp((sm - m_new).astype(BF16))
    pv = _dot(jnp.concatenate([vt, jnp.ones((16, vt.shape[1]), BF16)], axis=0), p)
    l_ref[...] = alpha * l_ref[...] + pv[HEAD_DIM:HEAD_DIM + 1]
    acc_ref[...] = alpha * acc_ref[...] + pv[:HEAD_DIM]
    m_ref[...] = m_new


def _nsa_attn_prompt_kernel(q_ref, sk_ref, svt_ref, wk_ref, wvt_ref, selt_ref, tz_ref, gate_ref,
                            ocmp_ref, o_ref, s_ref, peak_ref, m_ref, l_ref, acc_ref, *, n_diag):
    g = pl.program_id(1)
    i = pl.program_id(2)
    tq, tk, tb = TQ_NSA, TK_NSA, TK_ATT
    wq = 4 * HEAD_DIM
    lane = _iota((tq, LANES), 1)
    own_half = (lane // HEAD_DIM) == g
    q = q_ref[0]
    cols = []
    for r in range(R_C):
        a = pltpu.roll(q, (wq - r * HEAD_DIM) % wq, 1)[:, :LANES]
        both = jnp.where(g == 0, a, pltpu.roll(a, HEAD_DIM, 1))
        cols.append((jnp.where(own_half, both, 0.0) * SCALE).T)
    qt = jnp.concatenate(cols, axis=1).astype(BF16)
    selt = selt_ref[0].astype(BF16)
    qpos = i * tq + _iota((tk, tq), 1)
    key_in_tile = _iota((tk, tq), 0)
    own_rows = pl.multiple_of(g * HEAD_DIM, HEAD_DIM)

    def reset():
        m_ref[...] = jnp.full(m_ref.shape, NEG, F32)
        l_ref[...] = jnp.zeros_like(l_ref)
        acc_ref[...] = jnp.zeros_like(acc_ref)

    def slc_products(jt):
        ks = pl.multiple_of(jt * tk, tk)
        kt = sk_ref[0, pl.ds(ks, tk), :].astype(BF16)
        target = (g * SLC_BLOCK + (tk // SLC_BLOCK) * jt + (_iota((tk, LANES), 0) // SLC_BLOCK))
        expand = jnp.where(_iota((tk, LANES), 1) == target, 1.0, 0.0).astype(BF16)
        chosen = _dot(expand, selt)
        valid_f = jnp.where(ks + key_in_tile <= qpos, chosen, 0.0)
        return _dot(kt, qt), valid_f, (i * tq - ks) // tb

    def win_products(jt):
        ks = pl.multiple_of(jt * tk, tk)
        kt = wk_ref[0, pl.ds(ks, tk), :].astype(BF16)
        dist = qpos - (ks + key_in_tile)
        valid_f = jnp.where(jnp.logical_and(dist >= 0, dist <= WINDOW), 1.0, 0.0)
        return _dot(kt, qt), valid_f, (i * tq - ks) // tb

    def sweep(products, values_ref, first, last):
        def absorb(jt):
            vt = values_ref[0, jt, pl.ds(own_rows, HEAD_DIM), :].astype(BF16)
            _absorb_tile(vt, jt % 2, s_ref, peak_ref, m_ref, l_ref, acc_ref)

        def body(jt, carry):
            started = products(jt)
            absorb(jt - 1)
            _score_tile(started, jt % 2, s_ref, peak_ref, tz_ref, n_diag)
            return carry

        _score_tile(products(first), first % 2, s_ref, peak_ref, tz_ref, n_diag)
        lax.fori_loop(first + 1, last + 1, body, 0)
        absorb(last)

    def normalised_out():
        acc_t = acc_ref[...] / l_ref[...]
        out = jnp.zeros((tq, wq), F32)
        pad_rows = jnp.zeros((LANES - HEAD_DIM, tq), F32)
        for r in range(R_C):
            x = jnp.concatenate([acc_t[:, r * tq:(r + 1) * tq], pad_rows], axis=0).T
            x = jnp.concatenate([x, jnp.zeros((tq, wq - LANES), F32)], axis=1)
            out += x if r == 0 else pltpu.roll(x, r * HEAD_DIM, 1)
        return out

    last_tile = ((i + 1) * tq - 1) // tk
    reset()
    sweep(slc_products, svt_ref, 0, last_tile)
    o_slc = normalised_out()
    reset()
    sweep(win_products, wvt_ref, jnp.maximum(i * tq - WINDOW, 0) // tk, last_tile)
    o_win = normalised_out()

    sig = _sigmoid(gate_ref[0])
    gate_lane = R_C * g + _iota((LANES, wq), 1) // HEAD_DIM
    out = jnp.zeros((tq, wq), F32)
    for k, branch in enumerate((ocmp_ref[0], o_slc, o_win)):
        place = jnp.where(_iota((LANES, wq), 0) == k * H_C + gate_lane, 1.0, 0.0).astype(BF16)
        out += _times_01_matrix(sig, place) * branch
    o_ref[0] = out


def _value_tiles(kv_t):
    b, _, s = kv_t.shape
    v_t = kv_t[:, KV_W:, :].reshape(b, KV_W, s // TK_NSA, TK_NSA)
    return jnp.swapaxes(v_t, 1, 2)


def _nsa_attn_prompt(z3, slc_t, win_t, sel_t, tz, ocmp):
    b, s, _ = z3.shape
    wq = 4 * HEAD_DIM
    n_diag = tz.shape[0]
    n_kt = s // TK_NSA
    k_spec = lambda col: pl.BlockSpec((1, s, KV_W), lambda bi, g, i: (bi, 0, col // KV_W))
    vt_spec = pl.BlockSpec((1, n_kt, KV_W, TK_NSA), lambda bi, g, i: (bi, 0, 0, 0))
    return pl.pallas_call(
        functools.partial(_nsa_attn_prompt_kernel, n_diag=n_diag),
        out_shape=jax.ShapeDtypeStruct((b, s, D_C), F32),
        grid=(b, KV_C, s // TQ_NSA),
        in_specs=[pl.BlockSpec((1, TQ_NSA, wq), lambda bi, g, i: (bi, i, C_QC // wq + g)),
                  k_spec(C_SK), vt_spec, k_spec(C_WK), vt_spec,
                  pl.BlockSpec((1, 2 * SLC_BLOCK, TQ_NSA), lambda bi, g, i: (bi, 0, i)),
                  pl.BlockSpec((n_diag, 1, R_C, TK_ATT, TQ_ATT), lambda bi, g, i: (0, g, 0, 0, 0)),
                  pl.BlockSpec((1, TQ_NSA, LANES), lambda bi, g, i: (bi, i, C_GATE // LANES)),
                  pl.BlockSpec((1, TQ_NSA, wq), lambda bi, g, i: (bi, i, g))],
        out_specs=pl.BlockSpec((1, TQ_NSA, wq), lambda bi, g, i: (bi, i, g)),
        scratch_shapes=[pltpu.VMEM((2, TK_NSA, R_C * TQ_NSA), F32),
                        pltpu.VMEM((2, 8, R_C * TQ_NSA), F32),
                        pltpu.VMEM((1, R_C * TQ_NSA), F32),
                        pltpu.VMEM((1, R_C * TQ_NSA), F32),
                        pltpu.VMEM((HEAD_DIM, R_C * TQ_NSA), F32)],
        compiler_params=_cparams(("parallel", "parallel", "arbitrary")),
        name="nsa_attn_prompt",
    )(z3, z3, _value_tiles(slc_t), z3, _value_tiles(win_t), sel_t, tz, z3, ocmp)


def _own_lanes(width):
    return _iota((8, width), 1) // HEAD_DIM == _iota((8, width), 0)


def _nsa_cmp_decode_kernel(pt_ref, q_ref, pe_ref, w1_ref, w2_ref, bias_ref, *rest,
                           n_steps, n_sb):
    del pt_ref
    pages = rest[:PAGES_PER_STEP]
    f_ref, imp_ref, buf_ref = rest[PAGES_PER_STEP:]
    step = pl.program_id(1)
    rows_per_page = PAGE // CMP_STRIDE
    out_row = _iota((PAGE, PAGE), 0)
    regroup = jnp.where(_iota((PAGE, PAGE), 1)
                        == CMP_STRIDE * (out_row % rows_per_page) + out_row // rows_per_page,
                        1.0, 0.0).astype(BF16)
    for k, page in enumerate(pages):
        row0 = pl.multiple_of((step * PAGES_PER_STEP + k) * rows_per_page, rows_per_page)
        x = _dot_nt(regroup, page[0, 0].astype(BF16))
        for kv in range(2):
            for j in range(CMP_STRIDE):
                buf_ref[kv, j, pl.ds(row0, rows_per_page), :] = (
                    x[j * rows_per_page:(j + 1) * rows_per_page, kv * KV_W:(kv + 1) * KV_W])

    @pl.when(step == n_steps - 1)
    def _():
        n_chunks = buf_ref.shape[2]
        n_cb = n_chunks - 1
        kv8 = []
        for kv in range(2):
            hid = _compress_hidden(
                lambda j, kv=kv: buf_ref[kv, j],
                pe_ref.at[kv], lambda j, kv=kv: w1_ref[kv, j], n_chunks, KV_W)
            kv8.append(_dot(_gelu_tanh(hid).astype(BF16), w2_ref[kv]).astype(BF16))
        k8, v8 = kv8
        q8 = (_head_rows(q_ref[0], D_C) * SCALE).astype(BF16)
        s = _dot_nt(q8, k8) + bias_ref[...]
        n_idx = _iota((8, n_chunks), 1)
        mask = n_idx < n_cb
        sm = jnp.where(mask, s, NEG)
        m = jnp.max(sm, axis=-1, keepdims=True)
        p = jnp.where(mask, jnp.exp(sm - m), 0.0)
        p = p / jnp.maximum(jnp.sum(p, axis=-1, keepdims=True), 1e-30)
        pb = p.astype(BF16)
        f_ref[0] = jnp.where(_own_lanes(D_C), _dot(pb, v8), 0.0)
        n_pad = imp_ref.shape[2]
        c0 = _iota((n_chunks, n_pad), 0) * CMP_STRIDE
        j_idx = _iota((n_chunks, n_pad), 1)
        s0 = j_idx * SLC_BLOCK
        ov = jnp.logical_and(jnp.logical_and(c0 < s0 + SLC_BLOCK, c0 + CMP_LEN > s0),
                             jnp.logical_and(_iota((n_chunks, n_pad), 0) < n_cb, j_idx < n_sb))
        imp8 = _dot(pb, jnp.where(ov, 1.0, 0.0).astype(BF16))
        row = _iota((8, n_pad), 0)
        g0 = jnp.sum(jnp.where(row < R_C, imp8, 0.0), axis=0, keepdims=True)
        g1 = jnp.sum(jnp.where(row >= R_C, imp8, 0.0), axis=0, keepdims=True)
        imp_ref[0] = jnp.where(row == 0, g0, jnp.where(row == 1, g1, 0.0))


def _nsa_cmp_decode(qc, cache4, page_table, layer, pe2, w1bd, w2heads, bias_cmp, n_pad):
    db, n_pages = page_table.shape
    past = n_pages * PAGE
    n_steps = n_pages // PAGES_PER_STEP
    n_sb = -(-(past + 1) // SLC_BLOCK)
    n_chunks = past // CMP_STRIDE
    const = lambda shape: pl.BlockSpec(shape, lambda b, s, pt: (0,) * len(shape))

    def page_spec(k):
        def imap(b, s, pt):
            return (layer, pt[b, s * PAGES_PER_STEP + k], 0, 0)
        return pl.BlockSpec((1, 1, 2 * KV_W, PAGE), imap)

    grid_spec = pltpu.PrefetchScalarGridSpec(
        num_scalar_prefetch=1,
        grid=(db, n_steps),
        in_specs=[pl.BlockSpec((1, 1, D_C), lambda b, s, pt: (b, 0, 0)),
                  const((2, CMP_LEN, KV_W)), const((2, CMP_LEN, KV_W, KV_W)),
                  const((2, KV_W, D_C)), const((8, n_chunks))]
                 + [page_spec(k) for k in range(PAGES_PER_STEP)],
        out_specs=(pl.BlockSpec((1, 8, D_C), lambda b, s, pt: (b, 0, 0)),
                   pl.BlockSpec((1, 8, n_pad), lambda b, s, pt: (b, 0, 0))),
        scratch_shapes=[pltpu.VMEM((2, CMP_STRIDE, n_chunks, KV_W), F32)],
    )
    return pl.pallas_call(
        functools.partial(_nsa_cmp_decode_kernel, n_steps=n_steps, n_sb=n_sb),
        out_shape=(jax.ShapeDtypeStruct((db, 8, D_C), F32),
                   jax.ShapeDtypeStruct((db, 8, n_pad), F32)),
        grid_spec=grid_spec,
        compiler_params=_cparams(("parallel", "arbitrary")),
        name="nsa_cmp_decode",
    )(page_table, qc, pe2, w1bd, w2heads, bias_cmp, *([cache4] * PAGES_PER_STEP))


def _topk_decode_kernel(imp_ref, idx_ref, *, cur, n_sb, n_top):
    shape = imp_ref.shape
    lane = _iota(shape, 1)
    lane_f = lane.astype(F32)
    causal = jnp.logical_and(lane <= cur, lane < n_sb)
    forced = jnp.logical_or(lane == 0, lane > cur - N_LOCAL)
    score = jnp.where(causal, jnp.where(forced, BIG_SCORE, imp_ref[...]), -1.0)
    out_lane = _iota(idx_ref.shape, 1)
    out = jnp.full(idx_ref.shape, -1, jnp.int32)
    for k in range(n_top):
        m = jnp.max(score, axis=-1, keepdims=True)
        first = jnp.min(jnp.where(score == m, lane_f, 1e9), axis=-1, keepdims=True)
        pick = jnp.where(m > -0.5, first, -1.0).astype(jnp.int32)
        out = jnp.where(out_lane == k, pick, out)
        score = jnp.where(lane_f == first, -2.0, score)
    idx_ref[...] = out


def _topk_decode(imp_rows, cur, n_sb):
    return pl.pallas_call(
        functools.partial(_topk_decode_kernel, cur=cur, n_sb=n_sb, n_top=min(TOP_N, n_sb)),
        out_shape=jax.ShapeDtypeStruct((imp_rows.shape[0], LANES), jnp.int32),
        name="nsa_topk_decode",
    )(imp_rows)


def _softmax_with_new(s, valid, s_new, has_new):
    sm = jnp.where(valid, s, NEG)
    sn = jnp.where(has_new, s_new, NEG)
    m = jnp.maximum(jnp.max(sm, axis=-1, keepdims=True), sn)
    p = jnp.where(valid, jnp.exp(sm - m), 0.0)
    pn = jnp.where(has_new, jnp.exp(sn - m), 0.0)
    inv = 1.0 / jnp.maximum(jnp.sum(p, axis=-1, keepdims=True) + pn, 1e-30)
    return p * inv, pn * inv


def _to_f_form(o8):
    x = jnp.concatenate([o8, jnp.zeros((8, D_C - 2 * KV_W), F32)], axis=1)
    row = _iota((8, D_C), 0)
    out = jnp.zeros((8, D_C), F32)
    for h in range(H_C):
        shift = (h * HEAD_DIM - (KV_W + (h // R_C) * HEAD_DIM)) % D_C
        out = jnp.where(row == h, x if shift == 0 else pltpu.roll(x, shift, 1), out)
    return jnp.where(_own_lanes(D_C), out, 0.0)


def _nsa_attn_decode_kernel(pt_ref, idx_ref, q_ref, gate_ref, nslc_ref, nwin_ref, fcmp_ref, tsel_ref,
                            twin_ref, b0_ref, win_ref, *rest, n_top, new_blk):
    del pt_ref
    pages = rest[:KV_C * n_top]
    o_ref, nw_ref = rest[KV_C * n_top:]
    b = pl.program_id(0)
    w_len = win_ref.shape[3]
    per_page = PAGE // SLC_BLOCK

    qb = jnp.broadcast_to(q_ref[0], (8, D_C))
    row5 = _iota((8, D_C), 0)
    qr = jnp.zeros((8, D_C), F32)
    for h in range(H_C):
        shift = ((h // R_C) * HEAD_DIM - h * HEAD_DIM) % D_C
        qr = jnp.where(row5 == h, qb if shift == 0 else pltpu.roll(qb, shift, 1), qr)
    lane2 = _iota((8, 2 * KV_W), 1)
    row2 = _iota((8, 2 * KV_W), 0)
    q8k_f = jnp.where(lane2 // HEAD_DIM == row2 // R_C, qr[:, :2 * KV_W], 0.0) * SCALE
    q8k = q8k_f.astype(BF16)
    b0 = b0_ref[...]

    lane1 = _iota((8, LANES), 1)
    o_groups = []
    for g in range(KV_C):
        ids = [idx_ref[(b * KV_C + g) * n_top + k] for k in range(n_top)]
        kst = jnp.concatenate([pages[g * n_top + k][0, 0] for k in range(n_top)], axis=1).astype(BF16)
        s = _dot(q8k, kst)
        bias_parts, valid_parts = [], []
        has_new = jnp.int32(0)
        for k in range(n_top):
            blk = ids[k]
            in_cache = jnp.clip(blk, 0, new_blk - 1)
            bias_parts.append(tsel_ref[in_cache // per_page])
            ok = jnp.logical_and(blk >= 0, blk < new_blk).astype(F32)
            valid_parts.append(jnp.where(lane1 // SLC_BLOCK == in_cache % per_page, ok, 0.0))
            has_new = has_new | (blk == new_blk).astype(jnp.int32)
        s = s + jnp.concatenate(bias_parts, axis=1)
        valid = jnp.concatenate(valid_parts, axis=1) > 0.5
        new_row = nslc_ref[0]
        s_new = jnp.sum(q8k_f * new_row, axis=-1, keepdims=True) + b0
        p, pn = _softmax_with_new(s, valid, s_new, has_new > 0)
        o_groups.append(_dot_nt(p.astype(BF16), kst) + pn * new_row)
    f_slc = _to_f_form(jnp.where(row2 < R_C, o_groups[0], o_groups[1]))

    win_t = win_ref[0, 0]
    new_w = nwin_ref[0]
    s = _dot(q8k, win_t.astype(BF16)) + twin_ref[...]
    s_new = jnp.sum(q8k_f * new_w, axis=-1, keepdims=True) + b0
    p, pn = _softmax_with_new(s, s == s, s_new, True)
    f_win = _to_f_form(_dot_nt(p.astype(BF16), win_t.astype(BF16)) + pn * new_w)
    sq = (2 * KV_W, 2 * KV_W)
    new_col = jnp.sum(jnp.where(_iota(sq, 0) == _iota(sq, 1), jnp.broadcast_to(new_w, sq), 0.0),
                      axis=-1, keepdims=True)
    nw_ref[0] = jnp.where(_iota((2 * KV_W, w_len), 1) == w_len - 1, new_col,
                          pltpu.roll(win_t, w_len - 1, 1))

    sig = jnp.broadcast_to(_sigmoid(gate_ref[0]), (8, LANES))
    row1 = _iota((8, LANES), 0)
    total = jnp.zeros((8, D_C), F32)
    for k, f in enumerate((fcmp_ref[0], f_slc, f_win)):
        gk = jnp.sum(jnp.where(lane1 == k * H_C + row1, sig, 0.0), axis=-1, keepdims=True)
        total += gk * f
    o_ref[0] = jnp.sum(total, axis=0, keepdims=True)


def _nsa_attn_decode(qc, gate, new_slc, new_win, fcmp, tsel, twin, b0, cache_slc_t, cache_win_t,
                     page_table, sel_idx, layer, n_top):
    db, n_pages = page_table.shape
    new_blk = n_pages * PAGE // SLC_BLOCK
    w_len = cache_win_t.shape[3]
    per_page = PAGE // SLC_BLOCK
    const = lambda shape: pl.BlockSpec(shape, lambda b, pt, si: (0,) * len(shape))
    per_b = lambda shape: pl.BlockSpec(shape, lambda b, pt, si: (b,) + (0,) * (len(shape) - 1))

    def page_spec(g, k):
        def imap(b, pt, si):
            blk = jnp.clip(si[(b * KV_C + g) * n_top + k], 0, new_blk - 1)
            return (layer, pt[b, blk // per_page], 0, 0)
        return pl.BlockSpec((1, 1, 2 * KV_W, PAGE), imap)

    grid_spec = pltpu.PrefetchScalarGridSpec(
        num_scalar_prefetch=2,
        grid=(db,),
        in_specs=[per_b((1, 1, D_C)), per_b((1, 1, LANES)), per_b((1, 1, 2 * KV_W)),
                  per_b((1, 1, 2 * KV_W)), per_b((1, 8, D_C)),
                  const(tsel.shape), const(twin.shape), const((8, 1)),
                  pl.BlockSpec((1, 1, 2 * KV_W, w_len), lambda b, pt, si: (layer, b, 0, 0))]
                 + [page_spec(g, k) for g in range(KV_C) for k in range(n_top)],
        out_specs=(per_b((1, 1, D_C)), per_b((1, 2 * KV_W, w_len))),
    )
    return pl.pallas_call(
        functools.partial(_nsa_attn_decode_kernel, n_top=n_top, new_blk=new_blk),
        out_shape=(jax.ShapeDtypeStruct((db, 1, D_C), F32),
                   jax.ShapeDtypeStruct((db, 2 * KV_W, w_len), F32)),
        grid_spec=grid_spec,
        compiler_params=_cparams(("arbitrary",)),
        name="nsa_attn_decode",
    )(page_table, sel_idx, qc, gate, new_slc, new_win, fcmp, tsel, twin, b0, cache_win_t,
      *([cache_slc_t] * (KV_C * n_top)))


def _block_diag(mats):
    n = len(mats)
    rows = []
    for a, m in enumerate(mats):
        rows.append(jnp.concatenate([m if a == c else jnp.zeros_like(m) for c in range(n)], axis=-1))
    return jnp.concatenate(rows, axis=-2)


def _nsa_weights(cmp_pe, cmp_w1, cmp_w2):
    pe2 = jnp.concatenate([cmp_pe, cmp_pe], axis=-1)
    w1bd = jnp.stack([_block_diag([cmp_w1[kv]] * KV_C) for kv in range(2)]).astype(BF16)
    zero = jnp.zeros((HEAD_DIM, 4 * HEAD_DIM), F32)
    w2rep = []
    for kv in range(2):
        rep = jnp.concatenate([cmp_w2[kv]] * 4, axis=-1)
        w2rep.append(jnp.stack([jnp.concatenate([rep, zero], axis=0),
                                jnp.concatenate([zero, rep], axis=0)]))
    w2rep = jnp.stack(w2rep).astype(BF16)
    w2heads = jnp.concatenate([w2rep[:, 0], w2rep[:, 1]], axis=-1)
    return pe2, w1bd, w2rep, w2heads


def kernel(x_prompt, x_sample, cache_sb_kv, cache_cmp_kv, cache_slc_kv, cache_win_kv, page_table,
           rel_bias, norm_mix_pre, norm_mix_post, w_in, cmp_pe, cmp_w1, cmp_w2, gmlp_ws, gmlp_b,
           norm_group_out, w_out, norm_ffn_pre, norm_ffn_post, w_ffn_up, w_ffn_down):
    depth = w_in.shape[0]
    bsz, seq, _ = x_prompt.shape
    db = x_sample.shape[0]
    n_pages = page_table.shape[1]
    past = n_pages * PAGE
    n_phys = cache_sb_kv.shape[1]
    w_len = cache_win_kv.shape[2]
    assert x_sample.shape[1] == 1 and seq % 512 == 0 and seq // SLC_BLOCK <= SLC_BLOCK
    assert n_pages % PAGES_PER_STEP == 0 and w_len == WINDOW and past >= WINDOW

    t_prompt = bsz * seq
    tm_p = 512
    n_sb_dec = -(-(past + 1) // SLC_BLOCK)
    n_pad_dec = -(-n_sb_dec // LANES) * LANES
    n_top_dec = min(TOP_N, n_sb_dec)
    n_chunks_dec = past // CMP_STRIDE

    n_q = seq // TQ_ATT
    n_diag = min(n_q, REL_MAX_DIST // TK_ATT + 2)
    tz = _bias_table(rel_bias, n_diag, TK_ATT, TQ_ATT, TK_ATT, -1, 1, 0)
    tz = tz.reshape(KV_C, R_C, n_diag, TK_ATT, TQ_ATT).transpose(2, 0, 1, 3, 4)
    bias_cmp_p = _bias_table(rel_bias, n_q, TQ_ATT, seq // CMP_STRIDE, TQ_ATT, 1, -CMP_STRIDE,
                             -(CMP_LEN - 1))
    bias_cmp_d = _bias_table(rel_bias, 1, 8, n_chunks_dec, 0, 0, -CMP_STRIDE,
                             past - (CMP_LEN - 1))[:, 0, 0, :]
    n_tsel = -(-n_pages // 8) * 8
    tsel = _bias_table(rel_bias, 1, n_tsel, PAGE, 0, -PAGE, -1, past)[:, 0]
    tsel = jnp.swapaxes(tsel, 0, 1)
    twin = _bias_table(rel_bias, 1, 8, w_len, 0, 0, -1, w_len)[:, 0, 0, :]
    b0 = rel_bias[0].reshape(H_C, 1)

    pos_minor = lambda c: c.transpose(0, 1, 3, 4, 5, 2).reshape(c.shape[0], c.shape[1], -1, c.shape[2])
    cache_sb_t = pos_minor(cache_sb_kv)
    cache_cmp_t = pos_minor(cache_cmp_kv)
    cache_slc_t = pos_minor(cache_slc_kv)
    cache_win_t = pos_minor(cache_win_kv)
    from_pos_minor = lambda a, heads: jnp.moveaxis(
        a.reshape(a.shape[:-2] + (2, heads, HEAD_DIM, a.shape[-1])), -1, -4)

    xp = x_prompt.reshape(t_prompt, D_MODEL)
    xs = x_sample.reshape(db, D_MODEL)
    outs = {k: [] for k in ("p_sb", "p_cmp", "p_slc", "p_win", "s_sb", "s_cmp", "s_slc", "s_win", "s_gv")}
    row = lambda v: v.reshape(1, -1)

    for l in range(depth):
        w_in_l = jnp.pad(w_in[l], ((0, 0), (0, NP_IN - N_IN))).astype(BF16)
        w_out_l = w_out[l].astype(BF16)
        w_up_l = w_ffn_up[l].astype(BF16)
        w_down_l = w_ffn_down[l].astype(BF16)
        pe2, w1bd, w2rep, w2heads = _nsa_weights(cmp_pe[l], cmp_w1[l], cmp_w2[l])
        g_pre, g_post = row(norm_mix_pre[l]), row(norm_mix_post[l])
        g_grp = row(norm_group_out[l])
        gf_pre, gf_post = row(norm_ffn_pre[l]), row(norm_ffn_post[l])

        z = _project(xp, g_pre, w_in_l, tm_p)
        z3 = z.reshape(bsz, seq, NP_IN)
        o_a = _sb_prompt(z3)
        bias_rows = jnp.repeat(gmlp_b[l].T, CB, axis=1)
        o_b = _gmlp_prompt(z3, gmlp_ws[l], bias_rows)
        kvc = _compress_prompt(z3, pe2, w1bd, w2rep)
        o_cmp, sel_t = _nsa_cmp_prompt(z3, kvc, bias_cmp_p)
        slc_t = jnp.swapaxes(z3[:, :, C_SK:C_SK + 2 * KV_W], 1, 2)
        win_t = jnp.swapaxes(z3[:, :, C_WK:C_WK + 2 * KV_W], 1, 2)
        o_c = _nsa_attn_prompt(z3, slc_t, win_t, sel_t, tz, o_cmp)
        xp = _mix_out(o_a.reshape(t_prompt, D_A), o_b.reshape(t_prompt, D_B),
                      o_c.reshape(t_prompt, D_C), xp, g_grp, w_out_l, g_post, tm_p)
        xp = _ffn(xp, gf_pre, w_up_l, w_down_l, gf_post, 2 * tm_p, D_FF // 2)
        outs["p_sb"].append(z3[:, :, C_KA:C_KA + 2 * D_A].reshape(bsz, seq, 2, H_A, HEAD_DIM))
        kv_shape = (bsz, seq, 2, KV_C, HEAD_DIM)
        outs["p_cmp"].append(z3[:, :, C_CK:C_CK + 2 * KV_W].reshape(kv_shape))
        outs["p_slc"].append(from_pos_minor(slc_t, KV_C))
        n_win = min(WINDOW, seq)
        outs["p_win"].append(from_pos_minor(win_t[:, :, seq - n_win:], KV_C))

        zs = _project(xs, g_pre, w_in_l, db)
        zs3 = zs.reshape(db, 1, NP_IN)
        o_a_s = _sb_decode(zs3[:, :, C_QA:C_QA + D_A], cache_sb_t, page_table, l)
        w_row = jnp.repeat(gmlp_ws[l][:, 0, 0], CB).reshape(1, D_B)
        b_row = jnp.repeat(gmlp_b[l][:, 0], CB).reshape(1, D_B)
        o_b_s, vn_s = _gmlp_sample(zs[:, C_UB:C_UB + D_B], zs[:, C_VB:C_VB + D_B], w_row, b_row)
        qc_s = zs3[:, :, C_QC:C_QC + D_C]
        f_cmp, imp = _nsa_cmp_decode(qc_s, cache_cmp_t, page_table, l, pe2, w1bd, w2heads,
                                     bias_cmp_d, n_pad_dec)
        idx = _topk_decode(imp[:, :KV_C, :].reshape(db * KV_C, n_pad_dec), past // SLC_BLOCK, n_sb_dec)
        sel_idx = idx[:, :n_top_dec].reshape(-1)
        gate_s = zs3[:, :, C_GATE:C_GATE + LANES]
        new_slc = zs3[:, :, C_SK:C_SK + 2 * KV_W]
        new_win = zs3[:, :, C_WK:C_WK + 2 * KV_W]
        o_c_s, win_out = _nsa_attn_decode(qc_s, gate_s, new_slc, new_win, f_cmp, tsel, twin, b0,
                                          cache_slc_t, cache_win_t, page_table, sel_idx, l, n_top_dec)
        xs = _mix_out(o_a_s.reshape(db, D_A), o_b_s, o_c_s.reshape(db, D_C), xs, g_grp, w_out_l,
                      g_post, db)
        xs = _ffn(xs, gf_pre, w_up_l, w_down_l, gf_post, db, D_FF // 2)
        outs["s_sb"].append(zs[:, C_KA:C_KA + 2 * D_A].reshape(db, 1, 2, H_A, HEAD_DIM))
        outs["s_cmp"].append(zs[:, C_CK:C_CK + 2 * KV_W].reshape(db, 1, 2, KV_C, HEAD_DIM))
        outs["s_slc"].append(zs[:, C_SK:C_SK + 2 * KV_W].reshape(db, 1, 2, KV_C, HEAD_DIM))
        outs["s_win"].append(from_pos_minor(win_out, KV_C))
        outs["s_gv"].append(vn_s.reshape(db, 1, D_B))

    st = lambda k: jnp.stack(outs[k])
    return (xp.reshape(bsz, seq, D_MODEL), xs.reshape(db, 1, D_MODEL), st("p_sb"), st("p_cmp"),
            st("p_slc"), st("p_win"), st("s_sb"), st("s_cmp"), st("s_slc"), st("s_win"), st("s_gv"))
```

```python
import functools
import math

import jax
import jax.numpy as jnp
import numpy as np
from jax import lax
from jax.experimental import pallas as pl
from jax.experimental.pallas import tpu as pltpu

F32 = jnp.float32
BF16 = jnp.bfloat16

D_MODEL = 1024
HEAD_DIM = 64
D_A = 256
H_A = 4
D_B = 256
G_B = 4
CB = 64
D_C = 512
H_C = 8
KV_C = 2
R_C = 4
KV_W = 128
N_IN = 3 * D_A + 2 * D_B + D_C + 6 * KV_W + 3 * H_C
NP_IN = 2688
CHUNK = 128
CMP_LEN = 32
CMP_STRIDE = 16
SLC_BLOCK = 64
TOP_N = 16
N_LOCAL = 2
WINDOW = 512
NUM_BUCKETS = 32
REL_MAX_DIST = 2048
D_FF = 2816
EPS = 1e-6
SCALE = HEAD_DIM ** -0.5
PAGE = 128

C_QA, C_KA, C_VA, C_UB, C_VB, C_QC = 0, 256, 512, 768, 1024, 1280
C_CK, C_CV, C_SK, C_SV, C_WK, C_WV, C_GATE = 1792, 1920, 2048, 2176, 2304, 2432, 2560

LANES = 128
TQ_ATT = 128
TK_ATT = 128
TQ_NSA = 512
TK_NSA = 512
VMEM_LIMIT = 56 * 1024 * 1024
NEG = -1e30
DECAYED = -110.0
BIG_SCORE = 3e38
PAGES_PER_STEP = 32


def _cparams(sem):
    return pltpu.CompilerParams(dimension_semantics=sem, vmem_limit_bytes=VMEM_LIMIT)


def _dot(a, b):
    return jnp.dot(a, b, preferred_element_type=F32)


def _dot_nt(a, b):
    return lax.dot_general(a, b, (((1,), (1,)), ((), ())), preferred_element_type=F32)


def _rms(x, g):
    return x * lax.rsqrt(jnp.mean(x * x, axis=-1, keepdims=True) + EPS) * g


def _sigmoid(x):
    return 1.0 / (1.0 + jnp.exp(-x))


def _gelu_tanh(x):
    return 0.5 * x * (1.0 + jnp.tanh(math.sqrt(2.0 / math.pi) * (x + 0.044715 * (x * x * x))))


def _iota(shape, dim):
    return lax.broadcasted_iota(jnp.int32, shape, dim)


def _bucket_thresholds():
    n = np.arange(0, 1 << 15)
    exact = NUM_BUCKETS // 2
    nf = np.maximum(n, 1).astype(np.float32)
    big = exact + (np.log(nf / np.float32(exact)) / np.float32(math.log(REL_MAX_DIST / exact))
                   * np.float32(NUM_BUCKETS - exact)).astype(np.int32)
    bucket = np.where(n < exact, n, np.minimum(big, NUM_BUCKETS - 1))
    assert np.all(np.diff(bucket) >= 0)
    return [int(np.argmax(bucket >= k)) for k in range(1, NUM_BUCKETS)]


_THR = _bucket_thresholds()


def _bias_table_kernel(tab_ref, o_ref, *, a, rs, cs, c0):
    shape = o_ref.shape[2:]
    dist = a * pl.program_id(0) + rs * _iota(shape, 0) + cs * _iota(shape, 1) + c0
    outs = [jnp.full(shape, tab_ref[h], F32) for h in range(H_C)]
    for k in range(1, NUM_BUCKETS):
        ge = dist >= _THR[k - 1]
        for h in range(H_C):
            outs[h] = jnp.where(ge, tab_ref[k * H_C + h], outs[h])
    for h in range(H_C):
        o_ref[h, 0] = outs[h]


def _bias_table(rel_bias, steps, rows, cols, a, rs, cs, c0):
    return pl.pallas_call(
        functools.partial(_bias_table_kernel, a=a, rs=rs, cs=cs, c0=c0),
        out_shape=jax.ShapeDtypeStruct((H_C, steps, rows, cols), F32),
        grid=(steps,),
        in_specs=[pl.BlockSpec(memory_space=pltpu.SMEM)],
        out_specs=pl.BlockSpec((H_C, 1, rows, cols), lambda i: (0, i, 0, 0)),
        compiler_params=_cparams(("arbitrary",)),
        name="bias_table",
    )(rel_bias.reshape(-1))


def _proj_kernel(x_ref, g_ref, w_ref, o_ref):
    h = _rms(x_ref[...], g_ref[...])
    o_ref[...] = _dot(h.astype(BF16), w_ref[...])


def _project(x, g, w, tm):
    t = x.shape[0]
    return pl.pallas_call(
        _proj_kernel,
        out_shape=jax.ShapeDtypeStruct((t, NP_IN), F32),
        grid=(t // tm,),
        in_specs=[pl.BlockSpec((tm, D_MODEL), lambda i: (i, 0)),
                  pl.BlockSpec((1, D_MODEL), lambda i: (0, 0)),
                  pl.BlockSpec((D_MODEL, NP_IN), lambda i: (0, 0))],
        out_specs=pl.BlockSpec((tm, NP_IN), lambda i: (i, 0)),
        compiler_params=_cparams(("parallel",)),
        name="in_proj",
    )(x, g, w)


def _mixout_kernel(oa_ref, ob_ref, oc_ref, x_ref, gg_ref, w_ref, gp_ref, o_ref):
    gg = gg_ref[...]
    a = _rms(oa_ref[...], gg[:, :D_A]).astype(BF16)
    b = _rms(ob_ref[...], gg[:, D_A:D_A + D_B]).astype(BF16)
    c = _rms(oc_ref[...], gg[:, D_A + D_B:]).astype(BF16)
    y = (_dot(a, w_ref[0:D_A, :]) + _dot(b, w_ref[D_A:D_A + D_B, :])
         + _dot(c, w_ref[D_A + D_B:, :]))
    o_ref[...] = x_ref[...] + _rms(y, gp_ref[...])


def _mix_out(oa, ob, oc, x, gg, w, gp, tm):
    t = x.shape[0]
    row = lambda width: pl.BlockSpec((tm, width), lambda i: (i, 0))
    full = lambda r, c: pl.BlockSpec((r, c), lambda i: (0, 0))
    return pl.pallas_call(
        _mixout_kernel,
        out_shape=jax.ShapeDtypeStruct((t, D_MODEL), F32),
        grid=(t // tm,),
        in_specs=[row(D_A), row(D_B), row(D_C), row(D_MODEL), full(1, D_MODEL),
                  full(D_MODEL, D_MODEL), full(1, D_MODEL)],
        out_specs=row(D_MODEL),
        compiler_params=_cparams(("parallel",)),
        name="mix_out",
    )(oa, ob, oc, x, gg, w, gp)


def _ffn_kernel(x_ref, gpre_ref, wg_ref, wu_ref, wd_ref, gpost_ref, o_ref, h_ref, acc_ref):
    j = pl.program_id(1)

    @pl.when(j == 0)
    def _():
        h_ref[...] = _rms(x_ref[...], gpre_ref[...]).astype(BF16)
        acc_ref[...] = jnp.zeros_like(acc_ref)

    h = h_ref[...]
    g = _dot(h, wg_ref[...])
    u = _dot(h, wu_ref[...])
    act = (g * _sigmoid(g)) * u
    acc_ref[...] += _dot(act.astype(BF16), wd_ref[...])

    @pl.when(j == pl.num_programs(1) - 1)
    def _():
        o_ref[...] = x_ref[...] + _rms(acc_ref[...], gpost_ref[...])


def _ffn(x, gpre, w_up, w_down, gpost, tm, tf):
    t = x.shape[0]
    nf = D_FF // tf
    return pl.pallas_call(
        _ffn_kernel,
        out_shape=jax.ShapeDtypeStruct((t, D_MODEL), F32),
        grid=(t // tm, nf),
        in_specs=[pl.BlockSpec((tm, D_MODEL), lambda i, j: (i, 0)),
                  pl.BlockSpec((1, D_MODEL), lambda i, j: (0, 0)),
                  pl.BlockSpec((D_MODEL, tf), lambda i, j: (0, j)),
                  pl.BlockSpec((D_MODEL, tf), lambda i, j: (0, j + nf)),
                  pl.BlockSpec((tf, D_MODEL), lambda i, j: (j, 0)),
                  pl.BlockSpec((1, D_MODEL), lambda i, j: (0, 0))],
        out_specs=pl.BlockSpec((tm, D_MODEL), lambda i, j: (i, 0)),
        scratch_shapes=[pltpu.VMEM((tm, D_MODEL), BF16), pltpu.VMEM((tm, D_MODEL), F32)],
        compiler_params=_cparams(("parallel", "arbitrary")),
        name="ffn",
    )(x, gpre, w_up, w_up, w_down, gpost)


def _log_sig_pair(z):
    soft = jnp.log(1.0 + jnp.exp(-jnp.abs(z)))
    log_beta = jnp.minimum(z, 0.0) - soft
    return log_beta, log_beta - z


def _times_01_matrix(x, mat_bf16, pieces=3):
    out = None
    rem = x
    for k in range(pieces):
        part = rem.astype(BF16)
        term = _dot(part, mat_bf16)
        out = term if out is None else out + term
        if k + 1 < pieces:
            rem = rem - part.astype(F32)
    return out


def _suffix_sum_exclusive(x, upper_bf16):
    return _times_01_matrix(x, upper_bf16, pieces=2)


def _strict_upper(n):
    return jnp.where(_iota((n, n), 0) > _iota((n, n), 1), 1.0, 0.0).astype(BF16)


def _sb_prompt_kernel(q_ref, k_ref, v_ref, o_ref, run_ref, acc_ref, *, tq, tk):
    i = pl.program_id(1)
    lane_head = _iota((tq, D_A), 1) // HEAD_DIM
    qs = q_ref[0] * SCALE
    qh = [jnp.where(lane_head == h, qs, 0.0).astype(BF16) for h in range(H_A)]
    upper = _strict_upper(tk)
    run_ref[...] = jnp.zeros_like(run_ref)
    acc_ref[...] = jnp.zeros_like(acc_ref)

    def sweep_tile(j, diagonal):
        ks = pl.multiple_of(j * tk, tk)
        kt = k_ref[0, pl.ds(ks, tk), :].astype(BF16)
        vt = v_ref[0, pl.ds(ks, tk), :].astype(BF16)
        mask = _iota((tq, tk), 1) < _iota((tq, tk), 0)
        heads = range(H_A)
        z = [_dot_nt(qh[h], kt) for h in heads]
        pairs = [_log_sig_pair(z[h]) for h in heads]
        lb = [pairs[h][0] for h in heads]
        lk = [jnp.where(mask, pairs[h][1], 0.0) if diagonal else pairs[h][1] for h in heads]
        after = [_suffix_sum_exclusive(lk[h], upper) + run_ref[h] for h in heads]
        a = [jnp.exp(lb[h] + after[h]) for h in heads]
        if diagonal:
            a = [jnp.where(mask, a[h], 0.0) for h in heads]
        pv = [_dot(a[h].astype(BF16), vt) for h in heads]
        worst = jnp.float32(-jnp.inf)
        for h in heads:
            acc_ref[h] += pv[h]
            new_run = after[h][:, 0:1] + lk[h][:, 0:1]
            run_ref[h] = new_run
            worst = jnp.maximum(worst, jnp.max(new_run))
        return (worst > DECAYED).astype(jnp.int32)

    def cond(c):
        j, alive = c
        return jnp.logical_and(j >= 0, alive > 0)

    def body(c):
        return c[0] - 1, sweep_tile(c[0], False)

    assert tq == tk
    lax.while_loop(cond, body, (i - 1, sweep_tile(i, True)))
    out = acc_ref[0]
    for h in range(1, H_A):
        out = jnp.where(lane_head == h, acc_ref[h], out)
    o_ref[0] = out


def _sb_prompt(z3, tq=256, tk=256):
    b, s, _ = z3.shape
    return pl.pallas_call(
        functools.partial(_sb_prompt_kernel, tq=tq, tk=tk),
        out_shape=jax.ShapeDtypeStruct((b, s, D_A), F32),
        grid=(b, s // tq),
        in_specs=[pl.BlockSpec((1, tq, D_A), lambda bi, i: (bi, i, C_QA // D_A)),
                  pl.BlockSpec((1, s, D_A), lambda bi, i: (bi, 0, C_KA // D_A)),
                  pl.BlockSpec((1, s, D_A), lambda bi, i: (bi, 0, C_VA // D_A))],
        out_specs=pl.BlockSpec((1, tq, D_A), lambda bi, i: (bi, i, 0)),
        scratch_shapes=[pltpu.VMEM((H_A, tq, 1), F32), pltpu.VMEM((H_A, tq, D_A), F32)],
        compiler_params=_cparams(("parallel", "arbitrary")),
        name="sb_prompt",
    )(z3, z3, z3)


def _head_rows(row_vec, width):
    full = jnp.broadcast_to(row_vec, (8, width))
    return jnp.where(_iota((8, width), 1) // HEAD_DIM == _iota((8, width), 0), full, 0.0)


def _sb_decode_kernel(pt_ref, alive_ref, q_ref, run_in_ref, acc_in_ref, *rest, n_pages_step, n_steps):
    del pt_ref
    pages = rest[:n_pages_step]
    run_ref, acc_ref, o_ref, live_ref = rest[n_pages_step:]
    b = pl.program_id(0)
    step = pl.program_id(1)

    @pl.when(step == 0)
    def _():
        run_ref[...] = run_in_ref[...]
        acc_ref[...] = acc_in_ref[...]
        live_ref[0] = alive_ref[b]

    @pl.when(live_ref[0] > 0)
    def _():
        q8 = (_head_rows(q_ref[0], D_A) * SCALE).astype(BF16)
        upper = _strict_upper(PAGE)
        run = run_ref[0][:, 0:1]
        acc = acc_ref[0]
        for page in pages:
            kt = page[0, 0, 0:D_A, :].astype(BF16)
            vt = page[0, 0, D_A:2 * D_A, :].astype(BF16)
            z = _dot(q8, kt)
            lb, lk = _log_sig_pair(z)
            after = _suffix_sum_exclusive(lk, upper) + run
            a = jnp.exp(lb + after)
            acc = acc + _dot_nt(a.astype(BF16), vt)
            run = after[:, 0:1] + lk[:, 0:1]
        run_ref[0] = jnp.broadcast_to(run, (8, LANES))
        acc_ref[0] = acc
        head_rows = _iota((8, 1), 0) < H_A
        live_ref[0] = (jnp.max(jnp.where(head_rows, run, -jnp.inf)) > DECAYED).astype(jnp.int32)

    @pl.when(step == n_steps - 1)
    def _():
        o_ref[0] = jnp.sum(jnp.where(_own_lanes(D_A), acc_ref[0], 0.0), axis=0, keepdims=True)


def _sb_decode_phase(qa, cache_t, page_table, layer, alive, run, acc, first_back, n_pages_step,
                     n_steps):
    db, n_pages = page_table.shape

    def page_spec(k):
        def imap(b, s, pt, al):
            page = pt[b, n_pages - 1 - first_back - (s * n_pages_step + k)]
            return (layer, jnp.where(al[b] > 0, page, 0), 0, 0)
        return pl.BlockSpec((1, 1, 2 * D_A, PAGE), imap)

    per_b = lambda shape: pl.BlockSpec(shape, lambda b, s, pt, al: (b,) + (0,) * (len(shape) - 1))
    grid_spec = pltpu.PrefetchScalarGridSpec(
        num_scalar_prefetch=2,
        grid=(db, n_steps),
        in_specs=[per_b((1, 1, D_A)), per_b((1, 8, LANES)), per_b((1, 8, D_A))]
                 + [page_spec(k) for k in range(n_pages_step)],
        out_specs=(per_b((1, 8, LANES)), per_b((1, 8, D_A)), per_b((1, 1, D_A))),
        scratch_shapes=[pltpu.SMEM((1,), jnp.int32)],
    )
    return pl.pallas_call(
        functools.partial(_sb_decode_kernel, n_pages_step=n_pages_step, n_steps=n_steps),
        out_shape=(jax.ShapeDtypeStruct((db, 8, LANES), F32), jax.ShapeDtypeStruct((db, 8, D_A), F32),
                   jax.ShapeDtypeStruct((db, 1, D_A), F32)),
        grid_spec=grid_spec,
        compiler_params=_cparams(("arbitrary", "arbitrary")),
        name="sb_decode",
    )(page_table, alive, qa, run, acc, *([cache_t] * n_pages_step))


SB_FIRST_PAGES = 4


def _sb_decode(qa, cache_t, page_table, layer):
    db, n_pages = page_table.shape
    first = min(SB_FIRST_PAGES, n_pages)
    ones = jnp.ones((db,), jnp.int32)
    run0 = jnp.zeros((db, 8, LANES), F32)
    acc0 = jnp.zeros((db, 8, D_A), F32)
    run1, acc1, o1 = _sb_decode_phase(qa, cache_t, page_table, layer, ones, run0, acc0, 0, first, 1)
    rest = n_pages - first
    if rest == 0:
        return o1
    per_step = max(d for d in range(1, 13) if rest % d == 0)
    alive = (jnp.max(run1[:, :H_A, 0], axis=1) > DECAYED).astype(jnp.int32)
    return lax.cond(
        jnp.any(alive > 0),
        lambda: _sb_decode_phase(qa, cache_t, page_table, layer, alive, run1, acc1, first, per_step,
                                 rest // per_step)[2],
        lambda: o1)


def _layernorm(x):
    xc = x - jnp.mean(x, axis=-1, keepdims=True)
    return xc * lax.rsqrt(jnp.mean(xc * xc, axis=-1, keepdims=True) + EPS)


def _gmlp_prompt_kernel(u_ref, v_ref, w_ref, b_ref, o_ref, *, n_chunks):
    tril = _iota((CHUNK, CHUNK), 0) >= _iota((CHUNK, CHUNK), 1)
    ws = [jnp.where(tril, w_ref[g], 0.0).astype(BF16) for g in range(G_B)]
    group = _iota((CHUNK, D_B), 1) // CB
    for c in range(n_chunks):
        rows = slice(c * CHUNK, (c + 1) * CHUNK)
        vn = _layernorm(v_ref[0, rows, :]).astype(BF16)
        mixed = b_ref[...]
        for g in range(G_B):
            mixed = mixed + jnp.where(group == g, _dot(ws[g], vn), 0.0)
        o_ref[0, rows, :] = u_ref[0, rows, :] * mixed


def _gmlp_prompt(z3, ws, bias_rows, n_chunks=4):
    b, s, _ = z3.shape
    tr = n_chunks * CHUNK
    return pl.pallas_call(
        functools.partial(_gmlp_prompt_kernel, n_chunks=n_chunks),
        out_shape=jax.ShapeDtypeStruct((b, s, D_B), F32),
        grid=(b, s // tr),
        in_specs=[pl.BlockSpec((1, tr, D_B), lambda bi, i: (bi, i, C_UB // D_B)),
                  pl.BlockSpec((1, tr, D_B), lambda bi, i: (bi, i, C_VB // D_B)),
                  pl.BlockSpec((G_B, CHUNK, CHUNK), lambda bi, i: (0, 0, 0)),
                  pl.BlockSpec((CHUNK, D_B), lambda bi, i: (0, 0))],
        out_specs=pl.BlockSpec((1, tr, D_B), lambda bi, i: (bi, i, 0)),
        compiler_params=_cparams(("parallel", "parallel")),
        name="gmlp_prompt",
    )(z3, z3, ws, bias_rows)


def _gmlp_sample_kernel(u_ref, v_ref, w_ref, b_ref, o_ref, vn_ref):
    vn = _layernorm(v_ref[...])
    vn_ref[...] = vn
    o_ref[...] = u_ref[...] * (w_ref[...] * vn + b_ref[...])


def _gmlp_sample(u, v, w_row, b_row):
    return pl.pallas_call(
        _gmlp_sample_kernel,
        out_shape=(jax.ShapeDtypeStruct(u.shape, F32), jax.ShapeDtypeStruct(u.shape, F32)),
        name="gmlp_sample",
    )(u, v, w_row, b_row)


def _compress_hidden(load_rows, pe_ref, w1_fn, n_chunks, width):
    hid_a = jnp.zeros((n_chunks, width), F32)
    hid_b = jnp.zeros((n_chunks, width), F32)
    for j in range(CMP_STRIDE):
        x = load_rows(j)
        hid_a += _dot((x + pe_ref[j:j + 1, :]).astype(BF16), w1_fn(j))
        hid_b += _dot((x + pe_ref[CMP_STRIDE + j:CMP_STRIDE + j + 1, :]).astype(BF16),
                      w1_fn(CMP_STRIDE + j))
    return hid_a + pltpu.roll(hid_b, n_chunks - 1, 0)


def _compress_prompt_kernel(z_ref, pe_ref, w1_ref, w2_ref, o_ref, *, n_chunks):
    hid = _compress_hidden(lambda j: z_ref[0, pl.ds(j, n_chunks, stride=CMP_STRIDE), :],
                           pe_ref.at[0], lambda j: w1_ref[0, j], n_chunks, KV_W)
    act = _gelu_tanh(hid).astype(BF16)
    valid = _iota((n_chunks, 4 * HEAD_DIM), 0) < n_chunks - 1
    for g in range(KV_C):
        o_ref[0, 0, g] = jnp.where(valid, _dot(act, w2_ref[0, g]), 0.0)


def _compress_prompt(z3, pe2, w1bd, w2rep):
    b, s, _ = z3.shape
    n_chunks = s // CMP_STRIDE
    return pl.pallas_call(
        functools.partial(_compress_prompt_kernel, n_chunks=n_chunks),
        out_shape=jax.ShapeDtypeStruct((b, 2, KV_C, n_chunks, 4 * HEAD_DIM), F32),
        grid=(b, 2),
        in_specs=[pl.BlockSpec((1, s, KV_W), lambda bi, kv: (bi, 0, C_CK // KV_W + kv)),
                  pl.BlockSpec((1, CMP_LEN, KV_W), lambda bi, kv: (kv, 0, 0)),
                  pl.BlockSpec((1, CMP_LEN, KV_W, KV_W), lambda bi, kv: (kv, 0, 0, 0)),
                  pl.BlockSpec((1, KV_C, KV_W, 4 * HEAD_DIM), lambda bi, kv: (kv, 0, 0, 0))],
        out_specs=pl.BlockSpec((1, 1, KV_C, n_chunks, 4 * HEAD_DIM),
                               lambda bi, kv: (bi, kv, 0, 0, 0)),
        compiler_params=_cparams(("parallel", "parallel")),
        name="nsa_compress_prompt",
    )(z3, pe2, w1bd, w2rep)


def _select_blocks_t(score, causal, n_top):
    n_sb = score.shape[0]
    jt = _iota(score.shape, 0)
    rank = jnp.zeros(score.shape, F32)
    for jp in range(n_sb):
        row = score[jp:jp + 1, :]
        ge = jnp.where(row >= score, 1.0, 0.0)
        gt = jnp.where(row > score, 1.0, 0.0)
        rank += jnp.where(jt > jp, ge, gt)
    return jnp.where(causal, jnp.where(rank < n_top, 1.0, 0.0), 0.0)


def _nsa_cmp_prompt_kernel(q0_ref, q1_ref, kvc_ref, bias_ref, ocmp_ref, sel_ref, *, n_sb, n_cb):
    i = pl.program_id(1)
    tq = TQ_ATT
    n_pad = kvc_ref.shape[3]
    lane_head = _iota((tq, 4 * HEAD_DIM), 1) // HEAD_DIM
    n_idx = _iota((tq, n_pad), 1)
    dist = (i * tq + _iota((tq, n_pad), 0)) - (n_idx * CMP_STRIDE + (CMP_LEN - 1))
    mask = jnp.logical_and(dist >= 0, n_idx < n_cb)
    c0 = _iota((n_sb, n_pad), 1) * CMP_STRIDE
    s0 = _iota((n_sb, n_pad), 0) * SLC_BLOCK
    ov_t = jnp.where(jnp.logical_and(c0 < s0 + SLC_BLOCK, c0 + CMP_LEN > s0), 1.0, 0.0).astype(BF16)
    t_row = i * tq + _iota((n_sb, tq), 1)
    cur = t_row // SLC_BLOCK
    jt = _iota((n_sb, tq), 0)
    causal = jt <= cur
    forced = jnp.logical_or(jt == 0, jt > cur - N_LOCAL)
    s_all = []
    for g, q_ref in enumerate((q0_ref, q1_ref)):
        qs = q_ref[0] * SCALE
        qm = jnp.concatenate([jnp.where(lane_head == r, qs, 0.0) for r in range(R_C)],
                             axis=0).astype(BF16)
        s_all.append(_dot_nt(qm, kvc_ref[0, 0, g].astype(BF16)))
    probs = []
    for h in range(H_C):
        g, r = h // R_C, h % R_C
        sm = jnp.where(mask, s_all[g][r * tq:(r + 1) * tq] + bias_ref[h, 0], NEG)
        m = jnp.max(sm, axis=-1, keepdims=True)
        p = jnp.where(mask, jnp.exp(sm - m), 0.0)
        p = p / jnp.maximum(jnp.sum(p, axis=-1, keepdims=True), 1e-30)
        probs.append(p.astype(BF16))
    sel_t = []
    for g in range(KV_C):
        vc = kvc_ref[0, 1, g].astype(BF16)
        o_acc = jnp.zeros((tq, 4 * HEAD_DIM), F32)
        imp_t = jnp.zeros((n_sb, tq), F32)
        for r in range(R_C):
            pb = probs[R_C * g + r]
            o_acc += jnp.where(lane_head == r, _dot(pb, vc), 0.0)
            imp_t += _dot_nt(ov_t, pb)
        ocmp_ref[0, :, g * 4 * HEAD_DIM:(g + 1) * 4 * HEAD_DIM] = o_acc
        score = jnp.where(causal, jnp.where(forced, BIG_SCORE, imp_t), -1.0)
        sel_t.append(_select_blocks_t(score, causal, min(TOP_N, n_sb)))
    pad = jnp.zeros((SLC_BLOCK - n_sb, tq), F32)
    sel_ref[0] = jnp.concatenate([sel_t[0], pad, sel_t[1], pad] if n_sb < SLC_BLOCK else sel_t,
                                 axis=0)


def _nsa_cmp_prompt(z3, kvc, bias_cmp):
    b, s, _ = z3.shape
    n_sb = s // SLC_BLOCK
    n_cb = (s - CMP_LEN) // CMP_STRIDE + 1
    n_pad = kvc.shape[3]
    wq = 4 * HEAD_DIM
    return pl.pallas_call(
        functools.partial(_nsa_cmp_prompt_kernel, n_sb=n_sb, n_cb=n_cb),
        out_shape=(jax.ShapeDtypeStruct((b, s, D_C), F32),
                   jax.ShapeDtypeStruct((b, 2 * SLC_BLOCK, s), F32)),
        grid=(b, s // TQ_ATT),
        in_specs=[pl.BlockSpec((1, TQ_ATT, wq), lambda bi, i: (bi, i, C_QC // wq)),
                  pl.BlockSpec((1, TQ_ATT, wq), lambda bi, i: (bi, i, C_QC // wq + 1)),
                  pl.BlockSpec((1, 2, KV_C, n_pad, wq), lambda bi, i: (bi, 0, 0, 0, 0)),
                  pl.BlockSpec((H_C, 1, TQ_ATT, n_pad), lambda bi, i: (0, i, 0, 0))],
        out_specs=(pl.BlockSpec((1, TQ_ATT, D_C), lambda bi, i: (bi, i, 0)),
                   pl.BlockSpec((1, 2 * SLC_BLOCK, TQ_ATT), lambda bi, i: (bi, 0, i))),
        compiler_params=_cparams(("parallel", "parallel")),
        name="nsa_cmp_prompt",
    )(z3, z3, kvc, bias_cmp)


def _score_tile(args, slot, s_ref, peak_ref, tz_ref, n_diag):
    raw, valid_f, d0 = args
    tk, tb = raw.shape[0], TK_ATT
    nq = valid_f.shape[1] // tb
    tabs = {e: tz_ref[jnp.clip(d0 + e, 0, n_diag - 1), 0] for e in range(-(tk // tb - 1), nq)}
    bias = jnp.concatenate(
        [jnp.concatenate([tabs[cq - ck][r] for r in range(R_C) for cq in range(nq)], axis=1)
         for ck in range(tk // tb)], axis=0)
    valid = jnp.concatenate([valid_f] * R_C, axis=1) > 0.5
    sm = jnp.where(valid, raw + bias, NEG)
    s_ref[slot] = sm
    peak_ref[slot] = jnp.broadcast_to(jnp.max(sm, axis=0, keepdims=True), peak_ref.shape[1:])


def _absorb_tile(vt, slot, s_ref, peak_ref, m_ref, l_ref, acc_ref):
    sm = s_ref[slot]
    m_prev = m_ref[...]
    m_new = jnp.maximum(m_prev, peak_ref[slot][0:1])
    alpha = jnp.exp(m_prev - m_new)
    p = jnp.exp((sm - m_new).astype(BF16))
    pv = _dot(jnp.concatenate([vt, jnp.ones((16, vt.shape[1]), BF16)], axis=0), p)
    l_ref[...] = alpha * l_ref[...] + pv[HEAD_DIM:HEAD_DIM + 1]
    acc_ref[...] = alpha * acc_ref[...] + pv[:HEAD_DIM]
    m_ref[...] = m_new


def _nsa_attn_prompt_kernel(q_ref, sk_ref, svt_ref, wk_ref, wvt_ref, selt_ref, tz_ref, gate_ref,
                            ocmp_ref, o_ref, s_ref, peak_ref, m_ref, l_ref, acc_ref, *, n_diag):
    g = pl.program_id(1)
    i = pl.program_id(2)
    tq, tk, tb = TQ_NSA, TK_NSA, TK_ATT
    wq = 4 * HEAD_DIM
    lane = _iota((tq, LANES), 1)
    own_half = (lane // HEAD_DIM) == g
    q = q_ref[0]
    cols = []
    for r in range(R_C):
        a = pltpu.roll(q, (wq - r * HEAD_DIM) % wq, 1)[:, :LANES]
        both = jnp.where(g == 0, a, pltpu.roll(a, HEAD_DIM, 1))
        cols.append((jnp.where(own_half, both, 0.0) * SCALE).T)
    qt = jnp.concatenate(cols, axis=1).astype(BF16)
    selt = selt_ref[0].astype(BF16)
    qpos = i * tq + _iota((tk, tq), 1)
    key_in_tile = _iota((tk, tq), 0)
    own_rows = pl.multiple_of(g * HEAD_DIM, HEAD_DIM)

    def reset():
        m_ref[...] = jnp.full(m_ref.shape, NEG, F32)
        l_ref[...] = jnp.zeros_like(l_ref)
        acc_ref[...] = jnp.zeros_like(acc_ref)

    def slc_products(jt):
        ks = pl.multiple_of(jt * tk, tk)
        kt = sk_ref[0, pl.ds(ks, tk), :].astype(BF16)
        target = (g * SLC_BLOCK + (tk // SLC_BLOCK) * jt + (_iota((tk, LANES), 0) // SLC_BLOCK))
        expand = jnp.where(_iota((tk, LANES), 1) == target, 1.0, 0.0).astype(BF16)
        chosen = _dot(expand, selt)
        valid_f = jnp.where(ks + key_in_tile <= qpos, chosen, 0.0)
        return _dot(kt, qt), valid_f, (i * tq - ks) // tb

    def win_products(jt):
        ks = pl.multiple_of(jt * tk, tk)
        kt = wk_ref[0, pl.ds(ks, tk), :].astype(BF16)
        dist = qpos - (ks + key_in_tile)
        valid_f = jnp.where(jnp.logical_and(dist >= 0, dist <= WINDOW), 1.0, 0.0)
        return _dot(kt, qt), valid_f, (i * tq - ks) // tb

    def sweep(products, values_ref, first, last):
        def absorb(jt):
            vt = values_ref[0, jt, pl.ds(own_rows, HEAD_DIM), :].astype(BF16)
            _absorb_tile(vt, jt % 2, s_ref, peak_ref, m_ref, l_ref, acc_ref)

        def body(jt, carry):
            started = products(jt)
            absorb(jt - 1)
            _score_tile(started, jt % 2, s_ref, peak_ref, tz_ref, n_diag)
            return carry

        _score_tile(products(first), first % 2, s_ref, peak_ref, tz_ref, n_diag)
        lax.fori_loop(first + 1, last + 1, body, 0)
        absorb(last)

    def normalised_out():
        acc_t = acc_ref[...] / l_ref[...]
        out = jnp.zeros((tq, wq), F32)
        pad_rows = jnp.zeros((LANES - HEAD_DIM, tq), F32)
        for r in range(R_C):
            x = jnp.concatenate([acc_t[:, r * tq:(r + 1) * tq], pad_rows], axis=0).T
            x = jnp.concatenate([x, jnp.zeros((tq, wq - LANES), F32)], axis=1)
            out += x if r == 0 else pltpu.roll(x, r * HEAD_DIM, 1)
        return out

    last_tile = ((i + 1) * tq - 1) // tk
    reset()
    sweep(slc_products, svt_ref, 0, last_tile)
    o_slc = normalised_out()
    reset()
    sweep(win_products, wvt_ref, jnp.maximum(i * tq - WINDOW, 0) // tk, last_tile)
    o_win = normalised_out()

    sig = _sigmoid(gate_ref[0])
    gate_lane = R_C * g + _iota((LANES, wq), 1) // HEAD_DIM
    out = jnp.zeros((tq, wq), F32)
    for k, branch in enumerate((ocmp_ref[0], o_slc, o_win)):
        place = jnp.where(_iota((LANES, wq), 0) == k * H_C + gate_lane, 1.0, 0.0).astype(BF16)
        out += _times_01_matrix(sig, place) * branch
    o_ref[0] = out


def _value_tiles(kv_t):
    b, _, s = kv_t.shape
    v_t = kv_t[:, KV_W:, :].reshape(b, KV_W, s // TK_NSA, TK_NSA)
    return jnp.swapaxes(v_t, 1, 2)


def _nsa_attn_prompt(z3, slc_t, win_t, sel_t, tz, ocmp):
    b, s, _ = z3.shape
    wq = 4 * HEAD_DIM
    n_diag = tz.shape[0]
    n_kt = s // TK_NSA
    k_spec = lambda col: pl.BlockSpec((1, s, KV_W), lambda bi, g, i: (bi, 0, col // KV_W))
    vt_spec = pl.BlockSpec((1, n_kt, KV_W, TK_NSA), lambda bi, g, i: (bi, 0, 0, 0))
    return pl.pallas_call(
        functools.partial(_nsa_attn_prompt_kernel, n_diag=n_diag),
        out_shape=jax.ShapeDtypeStruct((b, s, D_C), F32),
        grid=(b, KV_C, s // TQ_NSA),
        in_specs=[pl.BlockSpec((1, TQ_NSA, wq), lambda bi, g, i: (bi, i, C_QC // wq + g)),
                  k_spec(C_SK), vt_spec, k_spec(C_WK), vt_spec,
                  pl.BlockSpec((1, 2 * SLC_BLOCK, TQ_NSA), lambda bi, g, i: (bi, 0, i)),
                  pl.BlockSpec((n_diag, 1, R_C, TK_ATT, TQ_ATT), lambda bi, g, i: (0, g, 0, 0, 0)),
                  pl.BlockSpec((1, TQ_NSA, LANES), lambda bi, g, i: (bi, i, C_GATE // LANES)),
                  pl.BlockSpec((1, TQ_NSA, wq), lambda bi, g, i: (bi, i, g))],
        out_specs=pl.BlockSpec((1, TQ_NSA, wq), lambda bi, g, i: (bi, i, g)),
        scratch_shapes=[pltpu.VMEM((2, TK_NSA, R_C * TQ_NSA), F32),
                        pltpu.VMEM((2, 8, R_C * TQ_NSA), F32),
                        pltpu.VMEM((1, R_C * TQ_NSA), F32),
                        pltpu.VMEM((1, R_C * TQ_NSA), F32),
                        pltpu.VMEM((HEAD_DIM, R_C * TQ_NSA), F32)],
        compiler_params=_cparams(("parallel", "parallel", "arbitrary")),
        name="nsa_attn_prompt",
    )(z3, z3, _value_tiles(slc_t), z3, _value_tiles(win_t), sel_t, tz, z3, ocmp)


def _own_lanes(width):
    return _iota((8, width), 1) // HEAD_DIM == _iota((8, width), 0)


def _nsa_cmp_decode_kernel(pt_ref, q_ref, pe_ref, w1_ref, w2_ref, bias_ref, *rest,
                           n_steps, n_sb):
    del pt_ref
    pages = rest[:PAGES_PER_STEP]
    f_ref, imp_ref, buf_ref = rest[PAGES_PER_STEP:]
    step = pl.program_id(1)
    rows_per_page = PAGE // CMP_STRIDE
    out_row = _iota((PAGE, PAGE), 0)
    regroup = jnp.where(_iota((PAGE, PAGE), 1)
                        == CMP_STRIDE * (out_row % rows_per_page) + out_row // rows_per_page,
                        1.0, 0.0).astype(BF16)
    for k, page in enumerate(pages):
        row0 = pl.multiple_of((step * PAGES_PER_STEP + k) * rows_per_page, rows_per_page)
        x = _dot_nt(regroup, page[0, 0].astype(BF16))
        for kv in range(2):
            for j in range(CMP_STRIDE):
                buf_ref[kv, j, pl.ds(row0, rows_per_page), :] = (
                    x[j * rows_per_page:(j + 1) * rows_per_page, kv * KV_W:(kv + 1) * KV_W])

    @pl.when(step == n_steps - 1)
    def _():
        n_chunks = buf_ref.shape[2]
        n_cb = n_chunks - 1
        kv8 = []
        for kv in range(2):
            hid = _compress_hidden(
                lambda j, kv=kv: buf_ref[kv, j],
                pe_ref.at[kv], lambda j, kv=kv: w1_ref[kv, j], n_chunks, KV_W)
            kv8.append(_dot(_gelu_tanh(hid).astype(BF16), w2_ref[kv]).astype(BF16))
        k8, v8 = kv8
        q8 = (_head_rows(q_ref[0], D_C) * SCALE).astype(BF16)
        s = _dot_nt(q8, k8) + bias_ref[...]
        n_idx = _iota((8, n_chunks), 1)
        mask = n_idx < n_cb
        sm = jnp.where(mask, s, NEG)
        m = jnp.max(sm, axis=-1, keepdims=True)
        p = jnp.where(mask, jnp.exp(sm - m), 0.0)
        p = p / jnp.maximum(jnp.sum(p, axis=-1, keepdims=True), 1e-30)
        pb = p.astype(BF16)
        f_ref[0] = jnp.where(_own_lanes(D_C), _dot(pb, v8), 0.0)
        n_pad = imp_ref.shape[2]
        c0 = _iota((n_chunks, n_pad), 0) * CMP_STRIDE
        j_idx = _iota((n_chunks, n_pad), 1)
        s0 = j_idx * SLC_BLOCK
        ov = jnp.logical_and(jnp.logical_and(c0 < s0 + SLC_BLOCK, c0 + CMP_LEN > s0),
                             jnp.logical_and(_iota((n_chunks, n_pad), 0) < n_cb, j_idx < n_sb))
        imp8 = _dot(pb, jnp.where(ov, 1.0, 0.0).astype(BF16))
        row = _iota((8, n_pad), 0)
        g0 = jnp.sum(jnp.where(row < R_C, imp8, 0.0), axis=0, keepdims=True)
        g1 = jnp.sum(jnp.where(row >= R_C, imp8, 0.0), axis=0, keepdims=True)
        imp_ref[0] = jnp.where(row == 0, g0, jnp.where(row == 1, g1, 0.0))


def _nsa_cmp_decode(qc, cache4, page_table, layer, pe2, w1bd, w2heads, bias_cmp, n_pad):
    db, n_pages = page_table.shape
    past = n_pages * PAGE
    n_steps = n_pages // PAGES_PER_STEP
    n_sb = -(-(past + 1) // SLC_BLOCK)
    n_chunks = past // CMP_STRIDE
    const = lambda shape: pl.BlockSpec(shape, lambda b, s, pt: (0,) * len(shape))

    def page_spec(k):
        def imap(b, s, pt):
            return (layer, pt[b, s * PAGES_PER_STEP + k], 0, 0)
        return pl.BlockSpec((1, 1, 2 * KV_W, PAGE), imap)

    grid_spec = pltpu.PrefetchScalarGridSpec(
        num_scalar_prefetch=1,
        grid=(db, n_steps),
        in_specs=[pl.BlockSpec((1, 1, D_C), lambda b, s, pt: (b, 0, 0)),
                  const((2, CMP_LEN, KV_W)), const((2, CMP_LEN, KV_W, KV_W)),
                  const((2, KV_W, D_C)), const((8, n_chunks))]
                 + [page_spec(k) for k in range(PAGES_PER_STEP)],
        out_specs=(pl.BlockSpec((1, 8, D_C), lambda b, s, pt: (b, 0, 0)),
                   pl.BlockSpec((1, 8, n_pad), lambda b, s, pt: (b, 0, 0))),
        scratch_shapes=[pltpu.VMEM((2, CMP_STRIDE, n_chunks, KV_W), F32)],
    )
    return pl.pallas_call(
        functools.partial(_nsa_cmp_decode_kernel, n_steps=n_steps, n_sb=n_sb),
        out_shape=(jax.ShapeDtypeStruct((db, 8, D_C), F32),
                   jax.ShapeDtypeStruct((db, 8, n_pad), F32)),
        grid_spec=grid_spec,
        compiler_params=_cparams(("parallel", "arbitrary")),
        name="nsa_cmp_decode",
    )(page_table, qc, pe2, w1bd, w2heads, bias_cmp, *([cache4] * PAGES_PER_STEP))


def _topk_decode_kernel(imp_ref, idx_ref, *, cur, n_sb, n_top):
    shape = imp_ref.shape
    lane = _iota(shape, 1)
    lane_f = lane.astype(F32)
    causal = jnp.logical_and(lane <= cur, lane < n_sb)
    forced = jnp.logical_or(lane == 0, lane > cur - N_LOCAL)
    score = jnp.where(causal, jnp.where(forced, BIG_SCORE, imp_ref[...]), -1.0)
    out_lane = _iota(idx_ref.shape, 1)
    out = jnp.full(idx_ref.shape, -1, jnp.int32)
    for k in range(n_top):
        m = jnp.max(score, axis=-1, keepdims=True)
        first = jnp.min(jnp.where(score == m, lane_f, 1e9), axis=-1, keepdims=True)
        pick = jnp.where(m > -0.5, first, -1.0).astype(jnp.int32)
        out = jnp.where(out_lane == k, pick, out)
        score = jnp.where(lane_f == first, -2.0, score)
    idx_ref[...] = out


def _topk_decode(imp_rows, cur, n_sb):
    return pl.pallas_call(
        functools.partial(_topk_decode_kernel, cur=cur, n_sb=n_sb, n_top=min(TOP_N, n_sb)),
        out_shape=jax.ShapeDtypeStruct((imp_rows.shape[0], LANES), jnp.int32),
        name="nsa_topk_decode",
    )(imp_rows)


def _softmax_with_new(s, valid, s_new, has_new):
    sm = jnp.where(valid, s, NEG)
    sn = jnp.where(has_new, s_new, NEG)
    m = jnp.maximum(jnp.max(sm, axis=-1, keepdims=True), sn)
    p = jnp.where(valid, jnp.exp(sm - m), 0.0)
    pn = jnp.where(has_new, jnp.exp(sn - m), 0.0)
    inv = 1.0 / jnp.maximum(jnp.sum(p, axis=-1, keepdims=True) + pn, 1e-30)
    return p * inv, pn * inv


def _to_f_form(o8):
    x = jnp.concatenate([o8, jnp.zeros((8, D_C - 2 * KV_W), F32)], axis=1)
    row = _iota((8, D_C), 0)
    out = jnp.zeros((8, D_C), F32)
    for h in range(H_C):
        shift = (h * HEAD_DIM - (KV_W + (h // R_C) * HEAD_DIM)) % D_C
        out = jnp.where(row == h, x if shift == 0 else pltpu.roll(x, shift, 1), out)
    return jnp.where(_own_lanes(D_C), out, 0.0)


def _nsa_attn_decode_kernel(pt_ref, idx_ref, q_ref, gate_ref, nslc_ref, nwin_ref, fcmp_ref, tsel_ref,
                            twin_ref, b0_ref, win_ref, *rest, n_top, new_blk):
    del pt_ref
    pages = rest[:KV_C * n_top]
    o_ref, nw_ref = rest[KV_C * n_top:]
    b = pl.program_id(0)
    w_len = win_ref.shape[3]
    per_page = PAGE // SLC_BLOCK

    qb = jnp.broadcast_to(q_ref[0], (8, D_C))
    row5 = _iota((8, D_C), 0)
    qr = jnp.zeros((8, D_C), F32)
    for h in range(H_C):
        shift = ((h // R_C) * HEAD_DIM - h * HEAD_DIM) % D_C
        qr = jnp.where(row5 == h, qb if shift == 0 else pltpu.roll(qb, shift, 1), qr)
    lane2 = _iota((8, 2 * KV_W), 1)
    row2 = _iota((8, 2 * KV_W), 0)
    q8k_f = jnp.where(lane2 // HEAD_DIM == row2 // R_C, qr[:, :2 * KV_W], 0.0) * SCALE
    q8k = q8k_f.astype(BF16)
    b0 = b0_ref[...]

    lane1 = _iota((8, LANES), 1)
    o_groups = []
    for g in range(KV_C):
        ids = [idx_ref[(b * KV_C + g) * n_top + k] for k in range(n_top)]
        kst = jnp.concatenate([pages[g * n_top + k][0, 0] for k in range(n_top)], axis=1).astype(BF16)
        s = _dot(q8k, kst)
        bias_parts, valid_parts = [], []
        has_new = jnp.int32(0)
        for k in range(n_top):
            blk = ids[k]
            in_cache = jnp.clip(blk, 0, new_blk - 1)
            bias_parts.append(tsel_ref[in_cache // per_page])
            ok = jnp.logical_and(blk >= 0, blk < new_blk).astype(F32)
            valid_parts.append(jnp.where(lane1 // SLC_BLOCK == in_cache % per_page, ok, 0.0))
            has_new = has_new | (blk == new_blk).astype(jnp.int32)
        s = s + jnp.concatenate(bias_parts, axis=1)
        valid = jnp.concatenate(valid_parts, axis=1) > 0.5
        new_row = nslc_ref[0]
        s_new = jnp.sum(q8k_f * new_row, axis=-1, keepdims=True) + b0
        p, pn = _softmax_with_new(s, valid, s_new, has_new > 0)
        o_groups.append(_dot_nt(p.astype(BF16), kst) + pn * new_row)
    f_slc = _to_f_form(jnp.where(row2 < R_C, o_groups[0], o_groups[1]))

    win_t = win_ref[0, 0]
    new_w = nwin_ref[0]
    s = _dot(q8k, win_t.astype(BF16)) + twin_ref[...]
    s_new = jnp.sum(q8k_f * new_w, axis=-1, keepdims=True) + b0
    p, pn = _softmax_with_new(s, s == s, s_new, True)
    f_win = _to_f_form(_dot_nt(p.astype(BF16), win_t.astype(BF16)) + pn * new_w)
    sq = (2 * KV_W, 2 * KV_W)
    new_col = jnp.sum(jnp.where(_iota(sq, 0) == _iota(sq, 1), jnp.broadcast_to(new_w, sq), 0.0),
                      axis=-1, keepdims=True)
    nw_ref[0] = jnp.where(_iota((2 * KV_W, w_len), 1) == w_len - 1, new_col,
                          pltpu.roll(win_t, w_len - 1, 1))

    sig = jnp.broadcast_to(_sigmoid(gate_ref[0]), (8, LANES))
    row1 = _iota((8, LANES), 0)
    total = jnp.zeros((8, D_C), F32)
    for k, f in enumerate((fcmp_ref[0], f_slc, f_win)):
        gk = jnp.sum(jnp.where(lane1 == k * H_C + row1, sig, 0.0), axis=-1, keepdims=True)
        total += gk * f
    o_ref[0] = jnp.sum(total, axis=0, keepdims=True)


def _nsa_attn_decode(qc, gate, new_slc, new_win, fcmp, tsel, twin, b0, cache_slc_t, cache_win_t,
                     page_table, sel_idx, layer, n_top):
    db, n_pages = page_table.shape
    new_blk = n_pages * PAGE // SLC_BLOCK
    w_len = cache_win_t.shape[3]
    per_page = PAGE // SLC_BLOCK
    const = lambda shape: pl.BlockSpec(shape, lambda b, pt, si: (0,) * len(shape))
    per_b = lambda shape: pl.BlockSpec(shape, lambda b, pt, si: (b,) + (0,) * (len(shape) - 1))

    def page_spec(g, k):
        def imap(b, pt, si):
            blk = jnp.clip(si[(b * KV_C + g) * n_top + k], 0, new_blk - 1)
            return (layer, pt[b, blk // per_page], 0, 0)
        return pl.BlockSpec((1, 1, 2 * KV_W, PAGE), imap)

    grid_spec = pltpu.PrefetchScalarGridSpec(
        num_scalar_prefetch=2,
        grid=(db,),
        in_specs=[per_b((1, 1, D_C)), per_b((1, 1, LANES)), per_b((1, 1, 2 * KV_W)),
                  per_b((1, 1, 2 * KV_W)), per_b((1, 8, D_C)),
                  const(tsel.shape), const(twin.shape), const((8, 1)),
                  pl.BlockSpec((1, 1, 2 * KV_W, w_len), lambda b, pt, si: (layer, b, 0, 0))]
                 + [page_spec(g, k) for g in range(KV_C) for k in range(n_top)],
        out_specs=(per_b((1, 1, D_C)), per_b((1, 2 * KV_W, w_len))),
    )
    return pl.pallas_call(
        functools.partial(_nsa_attn_decode_kernel, n_top=n_top, new_blk=new_blk),
        out_shape=(jax.ShapeDtypeStruct((db, 1, D_C), F32),
                   jax.ShapeDtypeStruct((db, 2 * KV_W, w_len), F32)),
        grid_spec=grid_spec,
        compiler_params=_cparams(("arbitrary",)),
        name="nsa_attn_decode",
    )(page_table, sel_idx, qc, gate, new_slc, new_win, fcmp, tsel, twin, b0, cache_win_t,
      *([cache_slc_t] * (KV_C * n_top)))


def _block_diag(mats):
    n = len(mats)
    rows = []
    for a, m in enumerate(mats):
        rows.append(jnp.concatenate([m if a == c else jnp.zeros_like(m) for c in range(n)], axis=-1))
    return jnp.concatenate(rows, axis=-2)


def _nsa_weights(cmp_pe, cmp_w1, cmp_w2):
    pe2 = jnp.concatenate([cmp_pe, cmp_pe], axis=-1)
    w1bd = jnp.stack([_block_diag([cmp_w1[kv]] * KV_C) for kv in range(2)]).astype(BF16)
    zero = jnp.zeros((HEAD_DIM, 4 * HEAD_DIM), F32)
    w2rep = []
    for kv in range(2):
        rep = jnp.concatenate([cmp_w2[kv]] * 4, axis=-1)
        w2rep.append(jnp.stack([jnp.concatenate([rep, zero], axis=0),
                                jnp.concatenate([zero, rep], axis=0)]))
    w2rep = jnp.stack(w2rep).astype(BF16)
    w2heads = jnp.concatenate([w2rep[:, 0], w2rep[:, 1]], axis=-1)
    return pe2, w1bd, w2rep, w2heads


def kernel(x_prompt, x_sample, cache_sb_kv, cache_cmp_kv, cache_slc_kv, cache_win_kv, page_table,
           rel_bias, norm_mix_pre, norm_mix_post, w_in, cmp_pe, cmp_w1, cmp_w2, gmlp_ws, gmlp_b,
           norm_group_out, w_out, norm_ffn_pre, norm_ffn_post, w_ffn_up, w_ffn_down):
    depth = w_in.shape[0]
    bsz, seq, _ = x_prompt.shape
    db = x_sample.shape[0]
    n_pages = page_table.shape[1]
    past = n_pages * PAGE
    n_phys = cache_sb_kv.shape[1]
    w_len = cache_win_kv.shape[2]
    assert x_sample.shape[1] == 1 and seq % 512 == 0 and seq // SLC_BLOCK <= SLC_BLOCK
    assert n_pages % PAGES_PER_STEP == 0 and w_len == WINDOW and past >= WINDOW

    t_prompt = bsz * seq
    tm_p = 512
    n_sb_dec = -(-(past + 1) // SLC_BLOCK)
    n_pad_dec = -(-n_sb_dec // LANES) * LANES
    n_top_dec = min(TOP_N, n_sb_dec)
    n_chunks_dec = past // CMP_STRIDE

    n_q = seq // TQ_ATT
    n_diag = min(n_q, REL_MAX_DIST // TK_ATT + 2)
    tz = _bias_table(rel_bias, n_diag, TK_ATT, TQ_ATT, TK_ATT, -1, 1, 0)
    tz = tz.reshape(KV_C, R_C, n_diag, TK_ATT, TQ_ATT).transpose(2, 0, 1, 3, 4)
    bias_cmp_p = _bias_table(rel_bias, n_q, TQ_ATT, seq // CMP_STRIDE, TQ_ATT, 1, -CMP_STRIDE,
                             -(CMP_LEN - 1))
    bias_cmp_d = _bias_table(rel_bias, 1, 8, n_chunks_dec, 0, 0, -CMP_STRIDE,
                             past - (CMP_LEN - 1))[:, 0, 0, :]
    n_tsel = -(-n_pages // 8) * 8
    tsel = _bias_table(rel_bias, 1, n_tsel, PAGE, 0, -PAGE, -1, past)[:, 0]
    tsel = jnp.swapaxes(tsel, 0, 1)
    twin = _bias_table(rel_bias, 1, 8, w_len, 0, 0, -1, w_len)[:, 0, 0, :]
    b0 = rel_bias[0].reshape(H_C, 1)

    pos_minor = lambda c: c.transpose(0, 1, 3, 4, 5, 2).reshape(c.shape[0], c.shape[1], -1, c.shape[2])
    cache_sb_t = pos_minor(cache_sb_kv)
    cache_cmp_t = pos_minor(cache_cmp_kv)
    cache_slc_t = pos_minor(cache_slc_kv)
    cache_win_t = pos_minor(cache_win_kv)
    from_pos_minor = lambda a, heads: jnp.moveaxis(
        a.reshape(a.shape[:-2] + (2, heads, HEAD_DIM, a.shape[-1])), -1, -4)

    xp = x_prompt.reshape(t_prompt, D_MODEL)
    xs = x_sample.reshape(db, D_MODEL)
    outs = {k: [] for k in ("p_sb", "p_cmp", "p_slc", "p_win", "s_sb", "s_cmp", "s_slc", "s_win", "s_gv")}
    row = lambda v: v.reshape(1, -1)

    for l in range(depth):
        w_in_l = jnp.pad(w_in[l], ((0, 0), (0, NP_IN - N_IN))).astype(BF16)
        w_out_l = w_out[l].astype(BF16)
        w_up_l = w_ffn_up[l].astype(BF16)
        w_down_l = w_ffn_down[l].astype(BF16)
        pe2, w1bd, w2rep, w2heads = _nsa_weights(cmp_pe[l], cmp_w1[l], cmp_w2[l])
        g_pre, g_post = row(norm_mix_pre[l]), row(norm_mix_post[l])
        g_grp = row(norm_group_out[l])
        gf_pre, gf_post = row(norm_ffn_pre[l]), row(norm_ffn_post[l])

        z = _project(xp, g_pre, w_in_l, tm_p)
        z3 = z.reshape(bsz, seq, NP_IN)
        o_a = _sb_prompt(z3)
        bias_rows = jnp.repeat(gmlp_b[l].T, CB, axis=1)
        o_b = _gmlp_prompt(z3, gmlp_ws[l], bias_rows)
        kvc = _compress_prompt(z3, pe2, w1bd, w2rep)
        o_cmp, sel_t = _nsa_cmp_prompt(z3, kvc, bias_cmp_p)
        slc_t = jnp.swapaxes(z3[:, :, C_SK:C_SK + 2 * KV_W], 1, 2)
        win_t = jnp.swapaxes(z3[:, :, C_WK:C_WK + 2 * KV_W], 1, 2)
        o_c = _nsa_attn_prompt(z3, slc_t, win_t, sel_t, tz, o_cmp)
        xp = _mix_out(o_a.reshape(t_prompt, D_A), o_b.reshape(t_prompt, D_B),
                      o_c.reshape(t_prompt, D_C), xp, g_grp, w_out_l, g_post, tm_p)
        xp = _ffn(xp, gf_pre, w_up_l, w_down_l, gf_post, 2 * tm_p, D_FF // 2)
        outs["p_sb"].append(z3[:, :, C_KA:C_KA + 2 * D_A].reshape(bsz, seq, 2, H_A, HEAD_DIM))
        kv_shape = (bsz, seq, 2, KV_C, HEAD_DIM)
        outs["p_cmp"].append(z3[:, :, C_CK:C_CK + 2 * KV_W].reshape(kv_shape))
        outs["p_slc"].append(from_pos_minor(slc_t, KV_C))
        n_win = min(WINDOW, seq)
        outs["p_win"].append(from_pos_minor(win_t[:, :, seq - n_win:], KV_C))

        zs = _project(xs, g_pre, w_in_l, db)
        zs3 = zs.reshape(db, 1, NP_IN)
        o_a_s = _sb_decode(zs3[:, :, C_QA:C_QA + D_A], cache_sb_t, page_table, l)
        w_row = jnp.repeat(gmlp_ws[l][:, 0, 0], CB).reshape(1, D_B)
        b_row = jnp.repeat(gmlp_b[l][:, 0], CB).reshape(1, D_B)
        o_b_s, vn_s = _gmlp_sample(zs[:, C_UB:C_UB + D_B], zs[:, C_VB:C_VB + D_B], w_row, b_row)
        qc_s = zs3[:, :, C_QC:C_QC + D_C]
        f_cmp, imp = _nsa_cmp_decode(qc_s, cache_cmp_t, page_table, l, pe2, w1bd, w2heads,
                                     bias_cmp_d, n_pad_dec)
        idx = _topk_decode(imp[:, :KV_C, :].reshape(db * KV_C, n_pad_dec), past // SLC_BLOCK, n_sb_dec)
        sel_idx = idx[:, :n_top_dec].reshape(-1)
        gate_s = zs3[:, :, C_GATE:C_GATE + LANES]
        new_slc = zs3[:, :, C_SK:C_SK + 2 * KV_W]
        new_win = zs3[:, :, C_WK:C_WK + 2 * KV_W]
        o_c_s, win_out = _nsa_attn_decode(qc_s, gate_s, new_slc, new_win, f_cmp, tsel, twin, b0,
                                          cache_slc_t, cache_win_t, page_table, sel_idx, l, n_top_dec)
        xs = _mix_out(o_a_s.reshape(db, D_A), o_b_s, o_c_s.reshape(db, D_C), xs, g_grp, w_out_l,
                      g_post, db)
        xs = _ffn(xs, gf_pre, w_up_l, w_down_l, gf_post, db, D_FF // 2)
        outs["s_sb"].append(zs[:, C_KA:C_KA + 2 * D_A].reshape(db, 1, 2, H_A, HEAD_DIM))
        outs["s_cmp"].append(zs[:, C_CK:C_CK + 2 * KV_W].reshape(db, 1, 2, KV_C, HEAD_DIM))
        outs["s_slc"].append(zs[:, C_SK:C_SK + 2 * KV_W].reshape(db, 1, 2, KV_C, HEAD_DIM))
        outs["s_win"].append(from_pos_minor(win_out, KV_C))
        outs["s_gv"].append(vn_s.reshape(db, 1, D_B))

    st = lambda k: jnp.stack(outs[k])
    return (xp.reshape(bsz, seq, D_MODEL), xs.reshape(db, 1, D_MODEL), st("p_sb"), st("p_cmp"),
            st("p_slc"), st("p_win"), st("s_sb"), st("s_cmp"), st("s_slc"), st("s_win"), st("s_gv"))
```

```python
import functools
import math

import jax
import jax.numpy as jnp
import numpy as np
from jax import lax
from jax.experimental import pallas as pl
from jax.experimental.pallas import tpu as pltpu

F32 = jnp.float32
BF16 = jnp.bfloat16

D_MODEL = 1024
HEAD_DIM = 64
D_A = 256
H_A = 4
D_B = 256
G_B = 4
CB = 64
D_C = 512
H_C = 8
KV_C = 2
R_C = 4
KV_W = 128
N_IN = 3 * D_A + 2 * D_B + D_C + 6 * KV_W + 3 * H_C
NP_IN = 2688
CHUNK = 128
CMP_LEN = 32
CMP_STRIDE = 16
SLC_BLOCK = 64
TOP_N = 16
N_LOCAL = 2
WINDOW = 512
NUM_BUCKETS = 32
REL_MAX_DIST = 2048
D_FF = 2816
EPS = 1e-6
SCALE = HEAD_DIM ** -0.5
PAGE = 128

C_QA, C_KA, C_VA, C_UB, C_VB, C_QC = 0, 256, 512, 768, 1024, 1280
C_CK, C_CV, C_SK, C_SV, C_WK, C_WV, C_GATE = 1792, 1920, 2048, 2176, 2304, 2432, 2560

LANES = 128
TQ_ATT = 128
TK_ATT = 128
TQ_NSA = 512
TK_NSA = 512
VMEM_LIMIT = 56 * 1024 * 1024
NEG = -1e30
DECAYED = -110.0
BIG_SCORE = 3e38
PAGES_PER_STEP = 32


def _cparams(sem):
    return pltpu.CompilerParams(dimension_semantics=sem, vmem_limit_bytes=VMEM_LIMIT)


def _dot(a, b):
    return jnp.dot(a, b, preferred_element_type=F32)


def _dot_nt(a, b):
    return lax.dot_general(a, b, (((1,), (1,)), ((), ())), preferred_element_type=F32)


def _rms(x, g):
    return x * lax.rsqrt(jnp.mean(x * x, axis=-1, keepdims=True) + EPS) * g


def _sigmoid(x):
    return 1.0 / (1.0 + jnp.exp(-x))


def _gelu_tanh(x):
    return 0.5 * x * (1.0 + jnp.tanh(math.sqrt(2.0 / math.pi) * (x + 0.044715 * (x * x * x))))


def _iota(shape, dim):
    return lax.broadcasted_iota(jnp.int32, shape, dim)


def _bucket_thresholds():
    n = np.arange(0, 1 << 15)
    exact = NUM_BUCKETS // 2
    nf = np.maximum(n, 1).astype(np.float32)
    big = exact + (np.log(nf / np.float32(exact)) / np.float32(math.log(REL_MAX_DIST / exact))
                   * np.float32(NUM_BUCKETS - exact)).astype(np.int32)
    bucket = np.where(n < exact, n, np.minimum(big, NUM_BUCKETS - 1))
    assert np.all(np.diff(bucket) >= 0)
    return [int(np.argmax(bucket >= k)) for k in range(1, NUM_BUCKETS)]


_THR = _bucket_thresholds()


def _bias_table_kernel(tab_ref, o_ref, *, a, rs, cs, c0):
    shape = o_ref.shape[2:]
    dist = a * pl.program_id(0) + rs * _iota(shape, 0) + cs * _iota(shape, 1) + c0
    outs = [jnp.full(shape, tab_ref[h], F32) for h in range(H_C)]
    for k in range(1, NUM_BUCKETS):
        ge = dist >= _THR[k - 1]
        for h in range(H_C):
            outs[h] = jnp.where(ge, tab_ref[k * H_C + h], outs[h])
    for h in range(H_C):
        o_ref[h, 0] = outs[h]


def _bias_table(rel_bias, steps, rows, cols, a, rs, cs, c0):
    return pl.pallas_call(
        functools.partial(_bias_table_kernel, a=a, rs=rs, cs=cs, c0=c0),
        out_shape=jax.ShapeDtypeStruct((H_C, steps, rows, cols), F32),
        grid=(steps,),
        in_specs=[pl.BlockSpec(memory_space=pltpu.SMEM)],
        out_specs=pl.BlockSpec((H_C, 1, rows, cols), lambda i: (0, i, 0, 0)),
        compiler_params=_cparams(("arbitrary",)),
        name="bias_table",
    )(rel_bias.reshape(-1))


def _proj_kernel(x_ref, g_ref, w_ref, o_ref):
    h = _rms(x_ref[...], g_ref[...])
    o_ref[...] = _dot(h.astype(BF16), w_ref[...])


def _project(x, g, w, tm):
    t = x.shape[0]
    return pl.pallas_call(
        _proj_kernel,
        out_shape=jax.ShapeDtypeStruct((t, NP_IN), F32),
        grid=(t // tm,),
        in_specs=[pl.BlockSpec((tm, D_MODEL), lambda i: (i, 0)),
                  pl.BlockSpec((1, D_MODEL), lambda i: (0, 0)),
                  pl.BlockSpec((D_MODEL, NP_IN), lambda i: (0, 0))],
        out_specs=pl.BlockSpec((tm, NP_IN), lambda i: (i, 0)),
        compiler_params=_cparams(("parallel",)),
        name="in_proj",
    )(x, g, w)


def _mixout_kernel(oa_ref, ob_ref, oc_ref, x_ref, gg_ref, w_ref, gp_ref, o_ref):
    gg = gg_ref[...]
    a = _rms(oa_ref[...], gg[:, :D_A]).astype(BF16)
    b = _rms(ob_ref[...], gg[:, D_A:D_A + D_B]).astype(BF16)
    c = _rms(oc_ref[...], gg[:, D_A + D_B:]).astype(BF16)
    y = (_dot(a, w_ref[0:D_A, :]) + _dot(b, w_ref[D_A:D_A + D_B, :])
         + _dot(c, w_ref[D_A + D_B:, :]))
    o_ref[...] = x_ref[...] + _rms(y, gp_ref[...])


def _mix_out(oa, ob, oc, x, gg, w, gp, tm):
    t = x.shape[0]
    row = lambda width: pl.BlockSpec((tm, width), lambda i: (i, 0))
    full = lambda r, c: pl.BlockSpec((r, c), lambda i: (0, 0))
    return pl.pallas_call(
        _mixout_kernel,
        out_shape=jax.ShapeDtypeStruct((t, D_MODEL), F32),
        grid=(t // tm,),
        in_specs=[row(D_A), row(D_B), row(D_C), row(D_MODEL), full(1, D_MODEL),
                  full(D_MODEL, D_MODEL), full(1, D_MODEL)],
        out_specs=row(D_MODEL),
        compiler_params=_cparams(("parallel",)),
        name="mix_out",
    )(oa, ob, oc, x, gg, w, gp)


def _ffn_kernel(x_ref, gpre_ref, wg_ref, wu_ref, wd_ref, gpost_ref, o_ref, h_ref, acc_ref):
    j = pl.program_id(1)

    @pl.when(j == 0)
    def _():
        h_ref[...] = _rms(x_ref[...], gpre_ref[...]).astype(BF16)
        acc_ref[...] = jnp.zeros_like(acc_ref)

    h = h_ref[...]
    g = _dot(h, wg_ref[...])
    u = _dot(h, wu_ref[...])
    act = (g * _sigmoid(g)) * u
    acc_ref[...] += _dot(act.astype(BF16), wd_ref[...])

    @pl.when(j == pl.num_programs(1) - 1)
    def _():
        o_ref[...] = x_ref[...] + _rms(acc_ref[...], gpost_ref[...])


def _ffn(x, gpre, w_up, w_down, gpost, tm, tf):
    t = x.shape[0]
    nf = D_FF // tf
    return pl.pallas_call(
        _ffn_kernel,
        out_shape=jax.ShapeDtypeStruct((t, D_MODEL), F32),
        grid=(t // tm, nf),
        in_specs=[pl.BlockSpec((tm, D_MODEL), lambda i, j: (i, 0)),
                  pl.BlockSpec((1, D_MODEL), lambda i, j: (0, 0)),
                  pl.BlockSpec((D_MODEL, tf), lambda i, j: (0, j)),
                  pl.BlockSpec((D_MODEL, tf), lambda i, j: (0, j + nf)),
                  pl.BlockSpec((tf, D_MODEL), lambda i, j: (j, 0)),
                  pl.BlockSpec((1, D_MODEL), lambda i, j: (0, 0))],
        out_specs=pl.BlockSpec((tm, D_MODEL), lambda i, j: (i, 0)),
        scratch_shapes=[pltpu.VMEM((tm, D_MODEL), BF16), pltpu.VMEM((tm, D_MODEL), F32)],
        compiler_params=_cparams(("parallel", "arbitrary")),
        name="ffn",
    )(x, gpre, w_up, w_up, w_down, gpost)


def _mix_ffn_kernel(oa_ref, ob_ref, oc_ref, x_ref, gg_ref, wo_ref, gp_ref, gpre_ref, wg_ref, wu_ref,
                    wd_ref, gpost_ref, o_ref, x1_ref, h_ref, acc_ref):
    j = pl.program_id(1)

    @pl.when(j == 0)
    def _():
        gg = gg_ref[...]
        a = _rms(oa_ref[...], gg[:, :D_A]).astype(BF16)
        b = _rms(ob_ref[...], gg[:, D_A:D_A + D_B]).astype(BF16)
        c = _rms(oc_ref[...], gg[:, D_A + D_B:]).astype(BF16)
        y = (_dot(a, wo_ref[0:D_A, :]) + _dot(b, wo_ref[D_A:D_A + D_B, :])
             + _dot(c, wo_ref[D_A + D_B:, :]))
        x1 = x_ref[...] + _rms(y, gp_ref[...])
        x1_ref[...] = x1
        h_ref[...] = _rms(x1, gpre_ref[...]).astype(BF16)
        acc_ref[...] = jnp.zeros_like(acc_ref)

    h = h_ref[...]
    g = _dot(h, wg_ref[...])
    u = _dot(h, wu_ref[...])
    act = (g * _sigmoid(g)) * u
    acc_ref[...] += _dot(act.astype(BF16), wd_ref[...])

    @pl.when(j == pl.num_programs(1) - 1)
    def _():
        o_ref[...] = x1_ref[...] + _rms(acc_ref[...], gpost_ref[...])


def _mix_ffn(oa, ob, oc, x, gg, w_out, gp, gpre, w_up, w_down, gpost, tm, tf):
    t = x.shape[0]
    nf = D_FF // tf
    row = lambda width: pl.BlockSpec((tm, width), lambda i, j: (i, 0))
    vec = pl.BlockSpec((1, D_MODEL), lambda i, j: (0, 0))
    return pl.pallas_call(
        _mix_ffn_kernel,
        out_shape=jax.ShapeDtypeStruct((t, D_MODEL), F32),
        grid=(t // tm, nf),
        in_specs=[row(D_A), row(D_B), row(D_C), row(D_MODEL), vec,
                  pl.BlockSpec((D_MODEL, D_MODEL), lambda i, j: (0, 0)), vec, vec,
                  pl.BlockSpec((D_MODEL, tf), lambda i, j: (0, j)),
                  pl.BlockSpec((D_MODEL, tf), lambda i, j: (0, j + nf)),
                  pl.BlockSpec((tf, D_MODEL), lambda i, j: (j, 0)), vec],
        out_specs=row(D_MODEL),
        scratch_shapes=[pltpu.VMEM((tm, D_MODEL), F32), pltpu.VMEM((tm, D_MODEL), BF16),
                        pltpu.VMEM((tm, D_MODEL), F32)],
        compiler_params=_cparams(("parallel", "arbitrary")),
        name="mix_ffn",
    )(oa, ob, oc, x, gg, w_out, gp, gpre, w_up, w_up, w_down, gpost)


def _log_sig_pair(z):
    soft = jnp.log(1.0 + jnp.exp(-jnp.abs(z)))
    log_beta = jnp.minimum(z, 0.0) - soft
    return log_beta, log_beta - z


def _times_01_matrix(x, mat_bf16, pieces=3):
    out = None
    rem = x
    for k in range(pieces):
        part = rem.astype(BF16)
        term = _dot(part, mat_bf16)
        out = term if out is None else out + term
        if k + 1 < pieces:
            rem = rem - part.astype(F32)
    return out


def _suffix_sum_exclusive(x, upper_bf16):
    return _times_01_matrix(x, upper_bf16, pieces=2)


def _strict_upper(n):
    return jnp.where(_iota((n, n), 0) > _iota((n, n), 1), 1.0, 0.0).astype(BF16)


def _sb_prompt_kernel(q_ref, k_ref, v_ref, o_ref, run_ref, acc_ref, *, tq, tk):
    i = pl.program_id(1)
    lane_head = _iota((tq, D_A), 1) // HEAD_DIM
    qs = q_ref[0] * SCALE
    qh = [jnp.where(lane_head == h, qs, 0.0).astype(BF16) for h in range(H_A)]
    upper = _strict_upper(tk)
    run_ref[...] = jnp.zeros_like(run_ref)
    acc_ref[...] = jnp.zeros_like(acc_ref)

    def sweep_tile(j, diagonal):
        ks = pl.multiple_of(j * tk, tk)
        kt = k_ref[0, pl.ds(ks, tk), :].astype(BF16)
        vt = v_ref[0, pl.ds(ks, tk), :].astype(BF16)
        mask = _iota((tq, tk), 1) < _iota((tq, tk), 0)
        heads = range(H_A)
        z = [_dot_nt(qh[h], kt) for h in heads]
        pairs = [_log_sig_pair(z[h]) for h in heads]
        lb = [pairs[h][0] for h in heads]
        lk = [jnp.where(mask, pairs[h][1], 0.0) if diagonal else pairs[h][1] for h in heads]
        after = [_suffix_sum_exclusive(lk[h], upper) + run_ref[h] for h in heads]
        a = [jnp.exp(lb[h] + after[h]) for h in heads]
        if diagonal:
            a = [jnp.where(mask, a[h], 0.0) for h in heads]
        pv = [_dot(a[h].astype(BF16), vt) for h in heads]
        worst = jnp.float32(-jnp.inf)
        for h in heads:
            acc_ref[h] += pv[h]
            new_run = after[h][:, 0:1] + lk[h][:, 0:1]
            run_ref[h] = new_run
            worst = jnp.maximum(worst, jnp.max(new_run))
        return (worst > DECAYED).astype(jnp.int32)

    def cond(c):
        j, alive = c
        return jnp.logical_and(j >= 0, alive > 0)

    def body(c):
        return c[0] - 1, sweep_tile(c[0], False)

    assert tq == tk
    lax.while_loop(cond, body, (i - 1, sweep_tile(i, True)))
    out = acc_ref[0]
    for h in range(1, H_A):
        out = jnp.where(lane_head == h, acc_ref[h], out)
    o_ref[0] = out


def _sb_prompt(z3, tq=256, tk=256):
    b, s, _ = z3.shape
    return pl.pallas_call(
        functools.partial(_sb_prompt_kernel, tq=tq, tk=tk),
        out_shape=jax.ShapeDtypeStruct((b, s, D_A), F32),
        grid=(b, s // tq),
        in_specs=[pl.BlockSpec((1, tq, D_A), lambda bi, i: (bi, i, C_QA // D_A)),
                  pl.BlockSpec((1, s, D_A), lambda bi, i: (bi, 0, C_KA // D_A)),
                  pl.BlockSpec((1, s, D_A), lambda bi, i: (bi, 0, C_VA // D_A))],
        out_specs=pl.BlockSpec((1, tq, D_A), lambda bi, i: (bi, i, 0)),
        scratch_shapes=[pltpu.VMEM((H_A, tq, 1), F32), pltpu.VMEM((H_A, tq, D_A), F32)],
        compiler_params=_cparams(("parallel", "arbitrary")),
        name="sb_prompt",
    )(z3, z3, z3)


def _head_rows(row_vec, width):
    full = jnp.broadcast_to(row_vec, (8, width))
    return jnp.where(_iota((8, width), 1) // HEAD_DIM == _iota((8, width), 0), full, 0.0)


def _sb_decode_kernel(pt_ref, alive_ref, q_ref, run_in_ref, acc_in_ref, *rest, n_pages_step, n_steps):
    del pt_ref
    pages = rest[:n_pages_step]
    run_ref, acc_ref, o_ref, live_ref = rest[n_pages_step:]
    b = pl.program_id(0)
    step = pl.program_id(1)

    @pl.when(step == 0)
    def _():
        run_ref[...] = run_in_ref[...]
        acc_ref[...] = acc_in_ref[...]
        live_ref[0] = alive_ref[b]

    @pl.when(live_ref[0] > 0)
    def _():
        q8 = (_head_rows(q_ref[0], D_A) * SCALE).astype(BF16)
        upper = _strict_upper(PAGE)
        run = run_ref[0][:, 0:1]
        acc = acc_ref[0]
        for page in pages:
            kt = page[0, 0, 0:D_A, :].astype(BF16)
            vt = page[0, 0, D_A:2 * D_A, :].astype(BF16)
            z = _dot(q8, kt)
            lb, lk = _log_sig_pair(z)
            after = _suffix_sum_exclusive(lk, upper) + run
            a = jnp.exp(lb + after)
            acc = acc + _dot_nt(a.astype(BF16), vt)
            run = after[:, 0:1] + lk[:, 0:1]
        run_ref[0] = jnp.broadcast_to(run, (8, LANES))
        acc_ref[0] = acc
        head_rows = _iota((8, 1), 0) < H_A
        live_ref[0] = (jnp.max(jnp.where(head_rows, run, -jnp.inf)) > DECAYED).astype(jnp.int32)

    @pl.when(step == n_steps - 1)
    def _():
        o_ref[0] = jnp.sum(jnp.where(_own_lanes(D_A), acc_ref[0], 0.0), axis=0, keepdims=True)


def _sb_decode_phase(qa, cache_t, page_table, layer, alive, run, acc, first_back, n_pages_step,
                     n_steps):
    db, n_pages = page_table.shape

    def page_spec(k):
        def imap(b, s, pt, al):
            page = pt[b, n_pages - 1 - first_back - (s * n_pages_step + k)]
            return (layer, jnp.where(al[b] > 0, page, 0), 0, 0)
        return pl.BlockSpec((1, 1, 2 * D_A, PAGE), imap)

    per_b = lambda shape: pl.BlockSpec(shape, lambda b, s, pt, al: (b,) + (0,) * (len(shape) - 1))
    grid_spec = pltpu.PrefetchScalarGridSpec(
        num_scalar_prefetch=2,
        grid=(db, n_steps),
        in_specs=[per_b((1, 1, D_A)), per_b((1, 8, LANES)), per_b((1, 8, D_A))]
                 + [page_spec(k) for k in range(n_pages_step)],
        out_specs=(per_b((1, 8, LANES)), per_b((1, 8, D_A)), per_b((1, 1, D_A))),
        scratch_shapes=[pltpu.SMEM((1,), jnp.int32)],
    )
    return pl.pallas_call(
        functools.partial(_sb_decode_kernel, n_pages_step=n_pages_step, n_steps=n_steps),
        out_shape=(jax.ShapeDtypeStruct((db, 8, LANES), F32), jax.ShapeDtypeStruct((db, 8, D_A), F32),
                   jax.ShapeDtypeStruct((db, 1, D_A), F32)),
        grid_spec=grid_spec,
        compiler_params=_cparams(("arbitrary", "arbitrary")),
        name="sb_decode",
    )(page_table, alive, qa, run, acc, *([cache_t] * n_pages_step))


SB_FIRST_PAGES = 4


def _sb_decode(qa, cache_t, page_table, layer):
    db, n_pages = page_table.shape
    first = min(SB_FIRST_PAGES, n_pages)
    ones = jnp.ones((db,), jnp.int32)
    run0 = jnp.zeros((db, 8, LANES), F32)
    acc0 = jnp.zeros((db, 8, D_A), F32)
    run1, acc1, o1 = _sb_decode_phase(qa, cache_t, page_table, layer, ones, run0, acc0, 0, first, 1)
    rest = n_pages - first
    if rest == 0:
        return o1
    per_step = max(d for d in range(1, 13) if rest % d == 0)
    alive = (jnp.max(run1[:, :H_A, 0], axis=1) > DECAYED).astype(jnp.int32)
    return lax.cond(
        jnp.any(alive > 0),
        lambda: _sb_decode_phase(qa, cache_t, page_table, layer, alive, run1, acc1, first, per_step,
                                 rest // per_step)[2],
        lambda: o1)


def _layernorm(x):
    xc = x - jnp.mean(x, axis=-1, keepdims=True)
    return xc * lax.rsqrt(jnp.mean(xc * xc, axis=-1, keepdims=True) + EPS)


def _gmlp_prompt_kernel(u_ref, v_ref, w_ref, b_ref, o_ref, *, n_chunks):
    tril = _iota((CHUNK, CHUNK), 0) >= _iota((CHUNK, CHUNK), 1)
    ws = [jnp.where(tril, w_ref[g], 0.0).astype(BF16) for g in range(G_B)]
    group = _iota((CHUNK, D_B), 1) // CB
    for c in range(n_chunks):
        rows = slice(c * CHUNK, (c + 1) * CHUNK)
        vn = _layernorm(v_ref[0, rows, :]).astype(BF16)
        mixed = b_ref[...]
        for g in range(G_B):
            mixed = mixed + jnp.where(group == g, _dot(ws[g], vn), 0.0)
        o_ref[0, rows, :] = u_ref[0, rows, :] * mixed


def _gmlp_prompt(z3, ws, bias_rows, n_chunks=4):
    b, s, _ = z3.shape
    tr = n_chunks * CHUNK
    return pl.pallas_call(
        functools.partial(_gmlp_prompt_kernel, n_chunks=n_chunks),
        out_shape=jax.ShapeDtypeStruct((b, s, D_B), F32),
        grid=(b, s // tr),
        in_specs=[pl.BlockSpec((1, tr, D_B), lambda bi, i: (bi, i, C_UB // D_B)),
                  pl.BlockSpec((1, tr, D_B), lambda bi, i: (bi, i, C_VB // D_B)),
                  pl.BlockSpec((G_B, CHUNK, CHUNK), lambda bi, i: (0, 0, 0)),
                  pl.BlockSpec((CHUNK, D_B), lambda bi, i: (0, 0))],
        out_specs=pl.BlockSpec((1, tr, D_B), lambda bi, i: (bi, i, 0)),
        compiler_params=_cparams(("parallel", "parallel")),
        name="gmlp_prompt",
    )(z3, z3, ws, bias_rows)


def _gmlp_sample_kernel(u_ref, v_ref, w_ref, b_ref, o_ref, vn_ref):
    vn = _layernorm(v_ref[...])
    vn_ref[...] = vn
    o_ref[...] = u_ref[...] * (w_ref[...] * vn + b_ref[...])


def _gmlp_sample(u, v, w_row, b_row):
    return pl.pallas_call(
        _gmlp_sample_kernel,
        out_shape=(jax.ShapeDtypeStruct(u.shape, F32), jax.ShapeDtypeStruct(u.shape, F32)),
        name="gmlp_sample",
    )(u, v, w_row, b_row)


def _compress_hidden(load_rows, pe_ref, w1_fn, n_chunks, width):
    hid_a = jnp.zeros((n_chunks, width), F32)
    hid_b = jnp.zeros((n_chunks, width), F32)
    for j in range(CMP_STRIDE):
        x = load_rows(j)
        hid_a += _dot((x + pe_ref[j:j + 1, :]).astype(BF16), w1_fn(j))
        hid_b += _dot((x + pe_ref[CMP_STRIDE + j:CMP_STRIDE + j + 1, :]).astype(BF16),
                      w1_fn(CMP_STRIDE + j))
    return hid_a + pltpu.roll(hid_b, n_chunks - 1, 0)


def _compress_prompt_kernel(z_ref, pe_ref, w1_ref, w2_ref, o_ref, *, n_chunks):
    hid = _compress_hidden(lambda j: z_ref[0, pl.ds(j, n_chunks, stride=CMP_STRIDE), :],
                           pe_ref.at[0], lambda j: w1_ref[0, j], n_chunks, KV_W)
    act = _gelu_tanh(hid).astype(BF16)
    valid = _iota((n_chunks, 4 * HEAD_DIM), 0) < n_chunks - 1
    for g in range(KV_C):
        o_ref[0, 0, g] = jnp.where(valid, _dot(act, w2_ref[0, g]), 0.0)


def _compress_prompt(z3, pe2, w1bd, w2rep):
    b, s, _ = z3.shape
    n_chunks = s // CMP_STRIDE
    return pl.pallas_call(
        functools.partial(_compress_prompt_kernel, n_chunks=n_chunks),
        out_shape=jax.ShapeDtypeStruct((b, 2, KV_C, n_chunks, 4 * HEAD_DIM), F32),
        grid=(b, 2),
        in_specs=[pl.BlockSpec((1, s, KV_W), lambda bi, kv: (bi, 0, C_CK // KV_W + kv)),
                  pl.BlockSpec((1, CMP_LEN, KV_W), lambda bi, kv: (kv, 0, 0)),
                  pl.BlockSpec((1, CMP_LEN, KV_W, KV_W), lambda bi, kv: (kv, 0, 0, 0)),
                  pl.BlockSpec((1, KV_C, KV_W, 4 * HEAD_DIM), lambda bi, kv: (kv, 0, 0, 0))],
        out_specs=pl.BlockSpec((1, 1, KV_C, n_chunks, 4 * HEAD_DIM),
                               lambda bi, kv: (bi, kv, 0, 0, 0)),
        compiler_params=_cparams(("parallel", "parallel")),
        name="nsa_compress_prompt",
    )(z3, pe2, w1bd, w2rep)


def _select_blocks_t(score, causal, n_top):
    n_sb = score.shape[0]
    jt = _iota(score.shape, 0)
    rank = jnp.zeros(score.shape, F32)
    for jp in range(n_sb):
        row = score[jp:jp + 1, :]
        ge = jnp.where(row >= score, 1.0, 0.0)
        gt = jnp.where(row > score, 1.0, 0.0)
        rank += jnp.where(jt > jp, ge, gt)
    return jnp.where(causal, jnp.where(rank < n_top, 1.0, 0.0), 0.0)


def _nsa_cmp_prompt_kernel(q0_ref, q1_ref, kvc_ref, bias_ref, ocmp_ref, sel_ref, *, n_sb, n_cb):
    i = pl.program_id(1)
    tq = TQ_ATT
    n_pad = kvc_ref.shape[3]
    lane_head = _iota((tq, 4 * HEAD_DIM), 1) // HEAD_DIM
    n_idx = _iota((tq, n_pad), 1)
    dist = (i * tq + _iota((tq, n_pad), 0)) - (n_idx * CMP_STRIDE + (CMP_LEN - 1))
    mask = jnp.logical_and(dist >= 0, n_idx < n_cb)
    c0 = _iota((n_sb, n_pad), 1) * CMP_STRIDE
    s0 = _iota((n_sb, n_pad), 0) * SLC_BLOCK
    ov_t = jnp.where(jnp.logical_and(c0 < s0 + SLC_BLOCK, c0 + CMP_LEN > s0), 1.0, 0.0).astype(BF16)
    t_row = i * tq + _iota((n_sb, tq), 1)
    cur = t_row // SLC_BLOCK
    jt = _iota((n_sb, tq), 0)
    causal = jt <= cur
    forced = jnp.logical_or(jt == 0, jt > cur - N_LOCAL)
    s_all = []
    for g, q_ref in enumerate((q0_ref, q1_ref)):
        qs = q_ref[0] * SCALE
        qm = jnp.concatenate([jnp.where(lane_head == r, qs, 0.0) for r in range(R_C)],
                             axis=0).astype(BF16)
        s_all.append(_dot_nt(qm, kvc_ref[0, 0, g].astype(BF16)))
    probs = []
    for h in range(H_C):
        g, r = h // R_C, h % R_C
        sm = jnp.where(mask, s_all[g][r * tq:(r + 1) * tq] + bias_ref[h, 0], NEG)
        m = jnp.max(sm, axis=-1, keepdims=True)
        p = jnp.where(mask, jnp.exp(sm - m), 0.0)
        p = p / jnp.maximum(jnp.sum(p, axis=-1, keepdims=True), 1e-30)
        probs.append(p.astype(BF16))
    sel_t = []
    for g in range(KV_C):
        vc = kvc_ref[0, 1, g].astype(BF16)
        o_acc = jnp.zeros((tq, 4 * HEAD_DIM), F32)
        imp_t = jnp.zeros((n_sb, tq), F32)
        for r in range(R_C):
            pb = probs[R_C * g + r]
            o_acc += jnp.where(lane_head == r, _dot(pb, vc), 0.0)
            imp_t += _dot_nt(ov_t, pb)
        ocmp_ref[0, :, g * 4 * HEAD_DIM:(g + 1) * 4 * HEAD_DIM] = o_acc
        score = jnp.where(causal, jnp.where(forced, BIG_SCORE, imp_t), -1.0)
        sel_t.append(_select_blocks_t(score, causal, min(TOP_N, n_sb)))
    pad = jnp.zeros((SLC_BLOCK - n_sb, tq), F32)
    sel_ref[0] = jnp.concatenate([sel_t[0], pad, sel_t[1], pad] if n_sb < SLC_BLOCK else sel_t,
                                 axis=0)


def _nsa_cmp_prompt(z3, kvc, bias_cmp):
    b, s, _ = z3.shape
    n_sb = s // SLC_BLOCK
    n_cb = (s - CMP_LEN) // CMP_STRIDE + 1
    n_pad = kvc.shape[3]
    wq = 4 * HEAD_DIM
    return pl.pallas_call(
        functools.partial(_nsa_cmp_prompt_kernel, n_sb=n_sb, n_cb=n_cb),
        out_shape=(jax.ShapeDtypeStruct((b, s, D_C), F32),
                   jax.ShapeDtypeStruct((b, 2 * SLC_BLOCK, s), F32)),
        grid=(b, s // TQ_ATT),
        in_specs=[pl.BlockSpec((1, TQ_ATT, wq), lambda bi, i: (bi, i, C_QC // wq)),
                  pl.BlockSpec((1, TQ_ATT, wq), lambda bi, i: (bi, i, C_QC // wq + 1)),
                  pl.BlockSpec((1, 2, KV_C, n_pad, wq), lambda bi, i: (bi, 0, 0, 0, 0)),
                  pl.BlockSpec((H_C, 1, TQ_ATT, n_pad), lambda bi, i: (0, i, 0, 0))],
        out_specs=(pl.BlockSpec((1, TQ_ATT, D_C), lambda bi, i: (bi, i, 0)),
                   pl.BlockSpec((1, 2 * SLC_BLOCK, TQ_ATT), lambda bi, i: (bi, 0, i))),
        compiler_params=_cparams(("parallel", "parallel")),
        name="nsa_cmp_prompt",
    )(z3, z3, kvc, bias_cmp)


def _score_tile(args, slot, s_ref, peak_ref, tz_ref, n_diag):
    raw, valid_f, d0 = args
    tk, tb = raw.shape[0], TK_ATT
    nq = valid_f.shape[1] // tb
    tabs = {e: tz_ref[jnp.clip(d0 + e, 0, n_diag - 1), 0] for e in range(-(tk // tb - 1), nq)}
    bias = jnp.concatenate(
        [jnp.concatenate([tabs[cq - ck][r] for r in range(R_C) for cq in range(nq)], axis=1)
         for ck in range(tk // tb)], axis=0)
    valid = jnp.concatenate([valid_f] * R_C, axis=1) > 0.5
    sm = jnp.where(valid, raw + bias, NEG)
    s_ref[slot] = sm
    peak_ref[slot] = jnp.broadcast_to(jnp.max(sm, axis=0, keepdims=True), peak_ref.shape[1:])


def _absorb_tile(vt, slot, s_ref, peak_ref, m_ref, l_ref, acc_ref):
    sm = s_ref[slot]
    m_prev = m_ref[...]
    m_new = jnp.maximum(m_prev, peak_ref[slot][0:1])
    alpha = jnp.exp(m_prev - m_new)
    p = jnp.exp((sm - m_new).astype(BF16))
    pv = _dot(jnp.concatenate([vt, jnp.ones((16, vt.shape[1]), BF16)], axis=0), p)
    l_ref[...] = alpha * l_ref[...] + pv[HEAD_DIM:HEAD_DIM + 1]
    acc_ref[...] = alpha * acc_ref[...] + pv[:HEAD_DIM]
    m_ref[...] = m_new


def _nsa_attn_prompt_kernel(q_ref, sk_ref, svt_ref, wk_ref, wvt_ref, selt_ref, tz_ref, gate_ref,
                            ocmp_ref, o_ref, s_ref, peak_ref, m_ref, l_ref, acc_ref, *, n_diag):
    g = pl.program_id(1)
    i = pl.program_id(2)
    tq, tk, tb = TQ_NSA, TK_NSA, TK_ATT
    wq = 4 * HEAD_DIM
    lane = _iota((tq, LANES), 1)
    own_half = (lane // HEAD_DIM) == g
    q = q_ref[0]
    cols = []
    for r in range(R_C):
        a = pltpu.roll(q, (wq - r * HEAD_DIM) % wq, 1)[:, :LANES]
        both = jnp.where(g == 0, a, pltpu.roll(a, HEAD_DIM, 1))
        cols.append((jnp.where(own_half, both, 0.0) * SCALE).T)
    qt = jnp.concatenate(cols, axis=1).astype(BF16)
    selt = selt_ref[0].astype(BF16)
    qpos = i * tq + _iota((tk, tq), 1)
    key_in_tile = _iota((tk, tq), 0)
    own_rows = pl.multiple_of(g * HEAD_DIM, HEAD_DIM)

    def reset():
        m_ref[...] = jnp.full(m_ref.shape, NEG, F32)
        l_ref[...] = jnp.zeros_like(l_ref)
        acc_ref[...] = jnp.zeros_like(acc_ref)

    def slc_products(jt):
        ks = pl.multiple_of(jt * tk, tk)
        kt = sk_ref[0, pl.ds(ks, tk), :].astype(BF16)
        target = (g * SLC_BLOCK + (tk // SLC_BLOCK) * jt + (_iota((tk, LANES), 0) // SLC_BLOCK))
        expand = jnp.where(_iota((tk, LANES), 1) == target, 1.0, 0.0).astype(BF16)
        chosen = _dot(expand, selt)
        valid_f = jnp.where(ks + key_in_tile <= qpos, chosen, 0.0)
        return _dot(kt, qt), valid_f, (i * tq - ks) // tb

    def win_products(jt):
        ks = pl.multiple_of(jt * tk, tk)
        kt = wk_ref[0, pl.ds(ks, tk), :].astype(BF16)
        dist = qpos - (ks + key_in_tile)
        valid_f = jnp.where(jnp.logical_and(dist >= 0, dist <= WINDOW), 1.0, 0.0)
        return _dot(kt, qt), valid_f, (i * tq - ks) // tb

    def sweep(products, values_ref, first, last):
        def absorb(jt):
            vt = values_ref[0, jt, pl.ds(own_rows, HEAD_DIM), :].astype(BF16)
            _absorb_tile(vt, jt % 2, s_ref, peak_ref, m_ref, l_ref, acc_ref)

        def body(jt, carry):
            started = products(jt)
            absorb(jt - 1)
            _score_tile(started, jt % 2, s_ref, peak_ref, tz_ref, n_diag)
            return carry

        _score_tile(products(first), first % 2, s_ref, peak_ref, tz_ref, n_diag)
        lax.fori_loop(first + 1, last + 1, body, 0)
        absorb(last)

    def normalised_out():
        acc_t = acc_ref[...] / l_ref[...]
        out = jnp.zeros((tq, wq), F32)
        pad_rows = jnp.zeros((LANES - HEAD_DIM, tq), F32)
        for r in range(R_C):
            x = jnp.concatenate([acc_t[:, r * tq:(r + 1) * tq], pad_rows], axis=0).T
            x = jnp.concatenate([x, jnp.zeros((tq, wq - LANES), F32)], axis=1)
            out += x if r == 0 else pltpu.roll(x, r * HEAD_DIM, 1)
        return out

    last_tile = ((i + 1) * tq - 1) // tk
    reset()
    sweep(slc_products, svt_ref, 0, last_tile)
    o_slc = normalised_out()
    reset()
    sweep(win_products, wvt_ref, jnp.maximum(i * tq - WINDOW, 0) // tk, last_tile)
    o_win = normalised_out()

    sig = _sigmoid(gate_ref[0])
    gate_lane = R_C * g + _iota((LANES, wq), 1) // HEAD_DIM
    out = jnp.zeros((tq, wq), F32)
    for k, branch in enumerate((ocmp_ref[0], o_slc, o_win)):
        place = jnp.where(_iota((LANES, wq), 0) == k * H_C + gate_lane, 1.0, 0.0).astype(BF16)
        out += _times_01_matrix(sig, place) * branch
    o_ref[0] = out


def _value_tiles(kv_t):
    b, _, s = kv_t.shape
    v_t = kv_t[:, KV_W:, :].reshape(b, KV_W, s // TK_NSA, TK_NSA)
    return jnp.swapaxes(v_t, 1, 2)


def _nsa_attn_prompt(z3, slc_t, win_t, sel_t, tz, ocmp):
    b, s, _ = z3.shape
    wq = 4 * HEAD_DIM
    n_diag = tz.shape[0]
    n_kt = s // TK_NSA
    k_spec = lambda col: pl.BlockSpec((1, s, KV_W), lambda bi, g, i: (bi, 0, col // KV_W))
    vt_spec = pl.BlockSpec((1, n_kt, KV_W, TK_NSA), lambda bi, g, i: (bi, 0, 0, 0))
    return pl.pallas_call(
        functools.partial(_nsa_attn_prompt_kernel, n_diag=n_diag),
        out_shape=jax.ShapeDtypeStruct((b, s, D_C), F32),
        grid=(b, KV_C, s // TQ_NSA),
        in_specs=[pl.BlockSpec((1, TQ_NSA, wq), lambda bi, g, i: (bi, i, C_QC // wq + g)),
                  k_spec(C_SK), vt_spec, k_spec(C_WK), vt_spec,
                  pl.BlockSpec((1, 2 * SLC_BLOCK, TQ_NSA), lambda bi, g, i: (bi, 0, i)),
                  pl.BlockSpec((n_diag, 1, R_C, TK_ATT, TQ_ATT), lambda bi, g, i: (0, g, 0, 0, 0)),
                  pl.BlockSpec((1, TQ_NSA, LANES), lambda bi, g, i: (bi, i, C_GATE // LANES)),
                  pl.BlockSpec((1, TQ_NSA, wq), lambda bi, g, i: (bi, i, g))],
        out_specs=pl.BlockSpec((1, TQ_NSA, wq), lambda bi, g, i: (bi, i, g)),
        scratch_shapes=[pltpu.VMEM((2, TK_NSA, R_C * TQ_NSA), F32),
                        pltpu.VMEM((2, 8, R_C * TQ_NSA), F32),
                        pltpu.VMEM((1, R_C * TQ_NSA), F32),
                        pltpu.VMEM((1, R_C * TQ_NSA), F32),
                        pltpu.VMEM((HEAD_DIM, R_C * TQ_NSA), F32)],
        compiler_params=_cparams(("parallel", "parallel", "arbitrary")),
        name="nsa_attn_prompt",
    )(z3, z3, _value_tiles(slc_t), z3, _value_tiles(win_t), sel_t, tz, z3, ocmp)


def _own_lanes(width):
    return _iota((8, width), 1) // HEAD_DIM == _iota((8, width), 0)


def _nsa_cmp_decode_kernel(pt_ref, q_ref, pe_ref, w1_ref, w2_ref, bias_ref, *rest,
                           n_steps, n_sb):
    del pt_ref
    pages = rest[:PAGES_PER_STEP]
    f_ref, imp_ref, buf_ref = rest[PAGES_PER_STEP:]
    step = pl.program_id(1)
    rows_per_page = PAGE // CMP_STRIDE
    out_row = _iota((PAGE, PAGE), 0)
    regroup = jnp.where(_iota((PAGE, PAGE), 1)
                        == CMP_STRIDE * (out_row % rows_per_page) + out_row // rows_per_page,
                        1.0, 0.0).astype(BF16)
    for k, page in enumerate(pages):
        row0 = pl.multiple_of((step * PAGES_PER_STEP + k) * rows_per_page, rows_per_page)
        x = _dot_nt(regroup, page[0, 0].astype(BF16))
        for kv in range(2):
            for j in range(CMP_STRIDE):
                buf_ref[kv, j, pl.ds(row0, rows_per_page), :] = (
                    x[j * rows_per_page:(j + 1) * rows_per_page, kv * KV_W:(kv + 1) * KV_W])

    @pl.when(step == n_steps - 1)
    def _():
        n_chunks = buf_ref.shape[2]
        n_cb = n_chunks - 1
        kv8 = []
        for kv in range(2):
            hid = _compress_hidden(
                lambda j, kv=kv: buf_ref[kv, j],
                pe_ref.at[kv], lambda j, kv=kv: w1_ref[kv, j], n_chunks, KV_W)
            kv8.append(_dot(_gelu_tanh(hid).astype(BF16), w2_ref[kv]).astype(BF16))
        k8, v8 = kv8
        q8 = (_head_rows(q_ref[0], D_C) * SCALE).astype(BF16)
        s = _dot_nt(q8, k8) + bias_ref[...]
        n_idx = _iota((8, n_chunks), 1)
        mask = n_idx < n_cb
        sm = jnp.where(mask, s, NEG)
        m = jnp.max(sm, axis=-1, keepdims=True)
        p = jnp.where(mask, jnp.exp(sm - m), 0.0)
        p = p / jnp.maximum(jnp.sum(p, axis=-1, keepdims=True), 1e-30)
        pb = p.astype(BF16)
        f_ref[0] = jnp.where(_own_lanes(D_C), _dot(pb, v8), 0.0)
        n_pad = imp_ref.shape[2]
        c0 = _iota((n_chunks, n_pad), 0) * CMP_STRIDE
        j_idx = _iota((n_chunks, n_pad), 1)
        s0 = j_idx * SLC_BLOCK
        ov = jnp.logical_and(jnp.logical_and(c0 < s0 + SLC_BLOCK, c0 + CMP_LEN > s0),
                             jnp.logical_and(_iota((n_chunks, n_pad), 0) < n_cb, j_idx < n_sb))
        imp8 = _dot(pb, jnp.where(ov, 1.0, 0.0).astype(BF16))
        row = _iota((8, n_pad), 0)
        g0 = jnp.sum(jnp.where(row < R_C, imp8, 0.0), axis=0, keepdims=True)
        g1 = jnp.sum(jnp.where(row >= R_C, imp8, 0.0), axis=0, keepdims=True)
        imp_ref[0] = jnp.where(row == 0, g0, jnp.where(row == 1, g1, 0.0))


def _nsa_cmp_decode(qc, cache4, page_table, layer, pe2, w1bd, w2heads, bias_cmp, n_pad):
    db, n_pages = page_table.shape
    past = n_pages * PAGE
    n_steps = n_pages // PAGES_PER_STEP
    n_sb = -(-(past + 1) // SLC_BLOCK)
    n_chunks = past // CMP_STRIDE
    const = lambda shape: pl.BlockSpec(shape, lambda b, s, pt: (0,) * len(shape))

    def page_spec(k):
        def imap(b, s, pt):
            return (layer, pt[b, s * PAGES_PER_STEP + k], 0, 0)
        return pl.BlockSpec((1, 1, 2 * KV_W, PAGE), imap)

    grid_spec = pltpu.PrefetchScalarGridSpec(
        num_scalar_prefetch=1,
        grid=(db, n_steps),
        in_specs=[pl.BlockSpec((1, 1, D_C), lambda b, s, pt: (b, 0, 0)),
                  const((2, CMP_LEN, KV_W)), const((2, CMP_LEN, KV_W, KV_W)),
                  const((2, KV_W, D_C)), const((8, n_chunks))]
                 + [page_spec(k) for k in range(PAGES_PER_STEP)],
        out_specs=(pl.BlockSpec((1, 8, D_C), lambda b, s, pt: (b, 0, 0)),
                   pl.BlockSpec((1, 8, n_pad), lambda b, s, pt: (b, 0, 0))),
        scratch_shapes=[pltpu.VMEM((2, CMP_STRIDE, n_chunks, KV_W), F32)],
    )
    return pl.pallas_call(
        functools.partial(_nsa_cmp_decode_kernel, n_steps=n_steps, n_sb=n_sb),
        out_shape=(jax.ShapeDtypeStruct((db, 8, D_C), F32),
                   jax.ShapeDtypeStruct((db, 8, n_pad), F32)),
        grid_spec=grid_spec,
        compiler_params=_cparams(("parallel", "arbitrary")),
        name="nsa_cmp_decode",
    )(page_table, qc, pe2, w1bd, w2heads, bias_cmp, *([cache4] * PAGES_PER_STEP))


def _topk_decode_kernel(imp_ref, idx_ref, *, cur, n_sb, n_top):
    shape = imp_ref.shape
    lane = _iota(shape, 1)
    lane_f = lane.astype(F32)
    causal = jnp.logical_and(lane <= cur, lane < n_sb)
    forced = jnp.logical_or(lane == 0, lane > cur - N_LOCAL)
    score = jnp.where(causal, jnp.where(forced, BIG_SCORE, imp_ref[...]), -1.0)
    out_lane = _iota(idx_ref.shape, 1)
    out = jnp.full(idx_ref.shape, -1, jnp.int32)
    for k in range(n_top):
        m = jnp.max(score, axis=-1, keepdims=True)
        first = jnp.min(jnp.where(score == m, lane_f, 1e9), axis=-1, keepdims=True)
        pick = jnp.where(m > -0.5, first, -1.0).astype(jnp.int32)
        out = jnp.where(out_lane == k, pick, out)
        score = jnp.where(lane_f == first, -2.0, score)
    idx_ref[...] = out


def _topk_decode(imp_rows, cur, n_sb):
    return pl.pallas_call(
        functools.partial(_topk_decode_kernel, cur=cur, n_sb=n_sb, n_top=min(TOP_N, n_sb)),
        out_shape=jax.ShapeDtypeStruct((imp_rows.shape[0], LANES), jnp.int32),
        name="nsa_topk_decode",
    )(imp_rows)


def _softmax_with_new(s, valid, s_new, has_new):
    sm = jnp.where(valid, s, NEG)
    sn = jnp.where(has_new, s_new, NEG)
    m = jnp.maximum(jnp.max(sm, axis=-1, keepdims=True), sn)
    p = jnp.where(valid, jnp.exp(sm - m), 0.0)
    pn = jnp.where(has_new, jnp.exp(sn - m), 0.0)
    inv = 1.0 / jnp.maximum(jnp.sum(p, axis=-1, keepdims=True) + pn, 1e-30)
    return p * inv, pn * inv


def _to_f_form(o8):
    x = jnp.concatenate([o8, jnp.zeros((8, D_C - 2 * KV_W), F32)], axis=1)
    row = _iota((8, D_C), 0)
    out = jnp.zeros((8, D_C), F32)
    for h in range(H_C):
        shift = (h * HEAD_DIM - (KV_W + (h // R_C) * HEAD_DIM)) % D_C
        out = jnp.where(row == h, x if shift == 0 else pltpu.roll(x, shift, 1), out)
    return jnp.where(_own_lanes(D_C), out, 0.0)


def _nsa_attn_decode_kernel(pt_ref, idx_ref, q_ref, gate_ref, nslc_ref, nwin_ref, fcmp_ref, tsel_ref,
                            twin_ref, b0_ref, win_ref, *rest, n_top, new_blk):
    del pt_ref
    pages = rest[:KV_C * n_top]
    o_ref, nw_ref = rest[KV_C * n_top:]
    b = pl.program_id(0)
    w_len = win_ref.shape[3]
    per_page = PAGE // SLC_BLOCK

    qb = jnp.broadcast_to(q_ref[0], (8, D_C))
    row5 = _iota((8, D_C), 0)
    qr = jnp.zeros((8, D_C), F32)
    for h in range(H_C):
        shift = ((h // R_C) * HEAD_DIM - h * HEAD_DIM) % D_C
        qr = jnp.where(row5 == h, qb if shift == 0 else pltpu.roll(qb, shift, 1), qr)
    lane2 = _iota((8, 2 * KV_W), 1)
    row2 = _iota((8, 2 * KV_W), 0)
    q8k_f = jnp.where(lane2 // HEAD_DIM == row2 // R_C, qr[:, :2 * KV_W], 0.0) * SCALE
    q8k = q8k_f.astype(BF16)
    b0 = b0_ref[...]

    lane1 = _iota((8, LANES), 1)
    o_groups = []
    for g in range(KV_C):
        ids = [idx_ref[(b * KV_C + g) * n_top + k] for k in range(n_top)]
        kst = jnp.concatenate([pages[g * n_top + k][0, 0] for k in range(n_top)], axis=1).astype(BF16)
        s = _dot(q8k, kst)
        bias_parts, valid_parts = [], []
        has_new = jnp.int32(0)
        for k in range(n_top):
            blk = ids[k]
            in_cache = jnp.clip(blk, 0, new_blk - 1)
            bias_parts.append(tsel_ref[in_cache // per_page])
            ok = jnp.logical_and(blk >= 0, blk < new_blk).astype(F32)
            valid_parts.append(jnp.where(lane1 // SLC_BLOCK == in_cache % per_page, ok, 0.0))
            has_new = has_new | (blk == new_blk).astype(jnp.int32)
        s = s + jnp.concatenate(bias_parts, axis=1)
        valid = jnp.concatenate(valid_parts, axis=1) > 0.5
        new_row = nslc_ref[0]
        s_new = jnp.sum(q8k_f * new_row, axis=-1, keepdims=True) + b0
        p, pn = _softmax_with_new(s, valid, s_new, has_new > 0)
        o_groups.append(_dot_nt(p.astype(BF16), kst) + pn * new_row)
    f_slc = _to_f_form(jnp.where(row2 < R_C, o_groups[0], o_groups[1]))

    win_t = win_ref[0, 0]
    new_w = nwin_ref[0]
    s = _dot(q8k, win_t.astype(BF16)) + twin_ref[...]
    s_new = jnp.sum(q8k_f * new_w, axis=-1, keepdims=True) + b0
    p, pn = _softmax_with_new(s, s == s, s_new, True)
    f_win = _to_f_form(_dot_nt(p.astype(BF16), win_t.astype(BF16)) + pn * new_w)
    sq = (2 * KV_W, 2 * KV_W)
    new_col = jnp.sum(jnp.where(_iota(sq, 0) == _iota(sq, 1), jnp.broadcast_to(new_w, sq), 0.0),
                      axis=-1, keepdims=True)
    nw_ref[0] = jnp.where(_iota((2 * KV_W, w_len), 1) == w_len - 1, new_col,
                          pltpu.roll(win_t, w_len - 1, 1))

    sig = jnp.broadcast_to(_sigmoid(gate_ref[0]), (8, LANES))
    row1 = _iota((8, LANES), 0)
    total = jnp.zeros((8, D_C), F32)
    for k, f in enumerate((fcmp_ref[0], f_slc, f_win)):
        gk = jnp.sum(jnp.where(lane1 == k * H_C + row1, sig, 0.0), axis=-1, keepdims=True)
        total += gk * f
    o_ref[0] = jnp.sum(total, axis=0, keepdims=True)


def _nsa_attn_decode(qc, gate, new_slc, new_win, fcmp, tsel, twin, b0, cache_slc_t, cache_win_t,
                     page_table, sel_idx, layer, n_top):
    db, n_pages = page_table.shape
    new_blk = n_pages * PAGE // SLC_BLOCK
    w_len = cache_win_t.shape[3]
    per_page = PAGE // SLC_BLOCK
    const = lambda shape: pl.BlockSpec(shape, lambda b, pt, si: (0,) * len(shape))
    per_b = lambda shape: pl.BlockSpec(shape, lambda b, pt, si: (b,) + (0,) * (len(shape) - 1))

    def page_spec(g, k):
        def imap(b, pt, si):
            blk = jnp.clip(si[(b * KV_C + g) * n_top + k], 0, new_blk - 1)
            return (layer, pt[b, blk // per_page], 0, 0)
        return pl.BlockSpec((1, 1, 2 * KV_W, PAGE), imap)

    grid_spec = pltpu.PrefetchScalarGridSpec(
        num_scalar_prefetch=2,
        grid=(db,),
        in_specs=[per_b((1, 1, D_C)), per_b((1, 1, LANES)), per_b((1, 1, 2 * KV_W)),
                  per_b((1, 1, 2 * KV_W)), per_b((1, 8, D_C)),
                  const(tsel.shape), const(twin.shape), const((8, 1)),
                  pl.BlockSpec((1, 1, 2 * KV_W, w_len), lambda b, pt, si: (layer, b, 0, 0))]
                 + [page_spec(g, k) for g in range(KV_C) for k in range(n_top)],
        out_specs=(per_b((1, 1, D_C)), per_b((1, 2 * KV_W, w_len))),
    )
    return pl.pallas_call(
        functools.partial(_nsa_attn_decode_kernel, n_top=n_top, new_blk=new_blk),
        out_shape=(jax.ShapeDtypeStruct((db, 1, D_C), F32),
                   jax.ShapeDtypeStruct((db, 2 * KV_W, w_len), F32)),
        grid_spec=grid_spec,
        compiler_params=_cparams(("arbitrary",)),
        name="nsa_attn_decode",
    )(page_table, sel_idx, qc, gate, new_slc, new_win, fcmp, tsel, twin, b0, cache_win_t,
      *([cache_slc_t] * (KV_C * n_top)))


def _block_diag(mats):
    n = len(mats)
    rows = []
    for a, m in enumerate(mats):
        rows.append(jnp.concatenate([m if a == c else jnp.zeros_like(m) for c in range(n)], axis=-1))
    return jnp.concatenate(rows, axis=-2)


def _nsa_weights(cmp_pe, cmp_w1, cmp_w2):
    pe2 = jnp.concatenate([cmp_pe, cmp_pe], axis=-1)
    w1bd = jnp.stack([_block_diag([cmp_w1[kv]] * KV_C) for kv in range(2)]).astype(BF16)
    zero = jnp.zeros((HEAD_DIM, 4 * HEAD_DIM), F32)
    w2rep = []
    for kv in range(2):
        rep = jnp.concatenate([cmp_w2[kv]] * 4, axis=-1)
        w2rep.append(jnp.stack([jnp.concatenate([rep, zero], axis=0),
                                jnp.concatenate([zero, rep], axis=0)]))
    w2rep = jnp.stack(w2rep).astype(BF16)
    w2heads = jnp.concatenate([w2rep[:, 0], w2rep[:, 1]], axis=-1)
    return pe2, w1bd, w2rep, w2heads


def kernel(x_prompt, x_sample, cache_sb_kv, cache_cmp_kv, cache_slc_kv, cache_win_kv, page_table,
           rel_bias, norm_mix_pre, norm_mix_post, w_in, cmp_pe, cmp_w1, cmp_w2, gmlp_ws, gmlp_b,
           norm_group_out, w_out, norm_ffn_pre, norm_ffn_post, w_ffn_up, w_ffn_down):
    depth = w_in.shape[0]
    bsz, seq, _ = x_prompt.shape
    db = x_sample.shape[0]
    n_pages = page_table.shape[1]
    past = n_pages * PAGE
    n_phys = cache_sb_kv.shape[1]
    w_len = cache_win_kv.shape[2]
    assert x_sample.shape[1] == 1 and seq % 512 == 0 and seq // SLC_BLOCK <= SLC_BLOCK
    assert n_pages % PAGES_PER_STEP == 0 and w_len == WINDOW and past >= WINDOW

    t_prompt = bsz * seq
    tm_p = 512
    n_sb_dec = -(-(past + 1) // SLC_BLOCK)
    n_pad_dec = -(-n_sb_dec // LANES) * LANES
    n_top_dec = min(TOP_N, n_sb_dec)
    n_chunks_dec = past // CMP_STRIDE

    n_q = seq // TQ_ATT
    n_diag = min(n_q, REL_MAX_DIST // TK_ATT + 2)
    tz = _bias_table(rel_bias, n_diag, TK_ATT, TQ_ATT, TK_ATT, -1, 1, 0)
    tz = tz.reshape(KV_C, R_C, n_diag, TK_ATT, TQ_ATT).transpose(2, 0, 1, 3, 4)
    bias_cmp_p = _bias_table(rel_bias, n_q, TQ_ATT, seq // CMP_STRIDE, TQ_ATT, 1, -CMP_STRIDE,
                             -(CMP_LEN - 1))
    bias_cmp_d = _bias_table(rel_bias, 1, 8, n_chunks_dec, 0, 0, -CMP_STRIDE,
                             past - (CMP_LEN - 1))[:, 0, 0, :]
    n_tsel = -(-n_pages // 8) * 8
    tsel = _bias_table(rel_bias, 1, n_tsel, PAGE, 0, -PAGE, -1, past)[:, 0]
    tsel = jnp.swapaxes(tsel, 0, 1)
    twin = _bias_table(rel_bias, 1, 8, w_len, 0, 0, -1, w_len)[:, 0, 0, :]
    b0 = rel_bias[0].reshape(H_C, 1)

    pos_minor = lambda c: c.transpose(0, 1, 3, 4, 5, 2).reshape(c.shape[0], c.shape[1], -1, c.shape[2])
    cache_sb_t = pos_minor(cache_sb_kv)
    cache_cmp_t = pos_minor(cache_cmp_kv)
    cache_slc_t = pos_minor(cache_slc_kv)
    cache_win_t = pos_minor(cache_win_kv)
    from_pos_minor = lambda a, heads: jnp.moveaxis(
        a.reshape(a.shape[:-2] + (2, heads, HEAD_DIM, a.shape[-1])), -1, -4)

    xp = x_prompt.reshape(t_prompt, D_MODEL)
    xs = x_sample.reshape(db, D_MODEL)
    outs = {k: [] for k in ("p_sb", "p_cmp", "p_slc", "p_win", "s_sb", "s_cmp", "s_slc", "s_win", "s_gv")}
    row = lambda v: v.reshape(1, -1)

    for l in range(depth):
        w_in_l = jnp.pad(w_in[l], ((0, 0), (0, NP_IN - N_IN))).astype(BF16)
        w_out_l = w_out[l].astype(BF16)
        w_up_l = w_ffn_up[l].astype(BF16)
        w_down_l = w_ffn_down[l].astype(BF16)
        pe2, w1bd, w2rep, w2heads = _nsa_weights(cmp_pe[l], cmp_w1[l], cmp_w2[l])
        g_pre, g_post = row(norm_mix_pre[l]), row(norm_mix_post[l])
        g_grp = row(norm_group_out[l])
        gf_pre, gf_post = row(norm_ffn_pre[l]), row(norm_ffn_post[l])

        z = _project(xp, g_pre, w_in_l, tm_p)
        z3 = z.reshape(bsz, seq, NP_IN)
        o_a = _sb_prompt(z3)
        bias_rows = jnp.repeat(gmlp_b[l].T, CB, axis=1)
        o_b = _gmlp_prompt(z3, gmlp_ws[l], bias_rows)
        kvc = _compress_prompt(z3, pe2, w1bd, w2rep)
        o_cmp, sel_t = _nsa_cmp_prompt(z3, kvc, bias_cmp_p)
        slc_t = jnp.swapaxes(z3[:, :, C_SK:C_SK + 2 * KV_W], 1, 2)
        win_t = jnp.swapaxes(z3[:, :, C_WK:C_WK + 2 * KV_W], 1, 2)
        o_c = _nsa_attn_prompt(z3, slc_t, win_t, sel_t, tz, o_cmp)
        xp = _mix_ffn(o_a.reshape(t_prompt, D_A), o_b.reshape(t_prompt, D_B),
                      o_c.reshape(t_prompt, D_C), xp, g_grp, w_out_l, g_post, gf_pre, w_up_l, w_down_l,
                      gf_post, tm_p, D_FF // 2)
        outs["p_sb"].append(z3[:, :, C_KA:C_KA + 2 * D_A].reshape(bsz, seq, 2, H_A, HEAD_DIM))
        kv_shape = (bsz, seq, 2, KV_C, HEAD_DIM)
        outs["p_cmp"].append(z3[:, :, C_CK:C_CK + 2 * KV_W].reshape(kv_shape))
        outs["p_slc"].append(from_pos_minor(slc_t, KV_C))
        n_win = min(WINDOW, seq)
        outs["p_win"].append(from_pos_minor(win_t[:, :, seq - n_win:], KV_C))

        zs = _project(xs, g_pre, w_in_l, db)
        zs3 = zs.reshape(db, 1, NP_IN)
        o_a_s = _sb_decode(zs3[:, :, C_QA:C_QA + D_A], cache_sb_t, page_table, l)
        w_row = jnp.repeat(gmlp_ws[l][:, 0, 0], CB).reshape(1, D_B)
        b_row = jnp.repeat(gmlp_b[l][:, 0], CB).reshape(1, D_B)
        o_b_s, vn_s = _gmlp_sample(zs[:, C_UB:C_UB + D_B], zs[:, C_VB:C_VB + D_B], w_row, b_row)
        qc_s = zs3[:, :, C_QC:C_QC + D_C]
        f_cmp, imp = _nsa_cmp_decode(qc_s, cache_cmp_t, page_table, l, pe2, w1bd, w2heads,
                                     bias_cmp_d, n_pad_dec)
        idx = _topk_decode(imp[:, :KV_C, :].reshape(db * KV_C, n_pad_dec), past // SLC_BLOCK, n_sb_dec)
        sel_idx = idx[:, :n_top_dec].reshape(-1)
        gate_s = zs3[:, :, C_GATE:C_GATE + LANES]
        new_slc = zs3[:, :, C_SK:C_SK + 2 * KV_W]
        new_win = zs3[:, :, C_WK:C_WK + 2 * KV_W]
        o_c_s, win_out = _nsa_attn_decode(qc_s, gate_s, new_slc, new_win, f_cmp, tsel, twin, b0,
                                          cache_slc_t, cache_win_t, page_table, sel_idx, l, n_top_dec)
        xs = _mix_out(o_a_s.reshape(db, D_A), o_b_s, o_c_s.reshape(db, D_C), xs, g_grp, w_out_l,
                      g_post, db)
        xs = _ffn(xs, gf_pre, w_up_l, w_down_l, gf_post, db, D_FF // 2)
        outs["s_sb"].append(zs[:, C_KA:C_KA + 2 * D_A].reshape(db, 1, 2, H_A, HEAD_DIM))
        outs["s_cmp"].append(zs[:, C_CK:C_CK + 2 * KV_W].reshape(db, 1, 2, KV_C, HEAD_DIM))
        outs["s_slc"].append(zs[:, C_SK:C_SK + 2 * KV_W].reshape(db, 1, 2, KV_C, HEAD_DIM))
        outs["s_win"].append(from_pos_minor(win_out, KV_C))
        outs["s_gv"].append(vn_s.reshape(db, 1, D_B))

    st = lambda k: jnp.stack(outs[k])
    return (xp.reshape(bsz, seq, D_MODEL), xs.reshape(db, 1, D_MODEL), st("p_sb"), st("p_cmp"),
            st("p_slc"), st("p_win"), st("s_sb"), st("s_cmp"), st("s_slc"), st("s_win"), st("s_gv"))
```
